```python
import jax, jax.numpy as jnp
from jax import lax
import numpy as np

D_MODEL = 1024
BATCH = 4
SEQ = 4096
DEPTH = 4
DEC_BATCH = 8
DEC_SEQ = 64
PAST_LEN = 4096

CHUNK = 64
N_EVEN = (DEPTH + 1) // 2
N_ODD = DEPTH // 2
EPS = 1e-6
NEG_INF = -1e30
CONV_DIM = D_MODEL // 2
CONV_WIDTH = 3
GLA_HEADS = 4
GLA_DV = D_MODEL // 2
GLA_DK = GLA_DV // 2
GLA_HEAD_V = GLA_DV // GLA_HEADS
GLA_HEAD_K = GLA_DK // GLA_HEADS
GATE_RANK = 16
GATE_NORM = 16.0
MIX_AB = CONV_DIM + GLA_DV
IN_SIZES = (CONV_DIM, CONV_DIM, CONV_DIM, GLA_DK, GLA_DK, GLA_DV, GLA_DV, GATE_RANK)
IN_AB = sum(IN_SIZES)
ATT_HEADS = 16
ATT_HEAD_DIM = 64
ATT_DIM = ATT_HEADS * ATT_HEAD_DIM
N_PREV_CHUNKS = 8
BAND_ROWS = N_PREV_CHUNKS * CHUNK
MAX_REL = 256
REL_SIZE = MAX_REL + CHUNK
D_FF = ((8 * D_MODEL // 3 + 255) // 256) * 256

kernel_name = "hybrid_streaming_conv_gla_chunkattn_step"


def rms_norm(x, g):
    xf = x.astype(jnp.float32)
    y = xf * lax.rsqrt(jnp.mean(xf * xf, axis=-1, keepdims=True) + EPS)
    return (y * g.astype(jnp.float32)).astype(x.dtype)


def short_conv_mixer(c_gate, b_gate, h, conv_w, prev):
    T = h.shape[1]
    u = c_gate * h
    up = jnp.concatenate([prev.astype(u.dtype), u], axis=1)
    y = sum(conv_w[i] * up[:, i:i + T] for i in range(CONV_WIDTH))
    return b_gate * y, up[:, T:]


def gla_scan(q, k, v, log_a, s0):
    B, T, H, DK = q.shape
    DV = v.shape[-1]
    L = min(T, CHUNK)
    n = T // L

    def blocks(a):
        return a.reshape(B, n, L, H, a.shape[-1]).swapaxes(0, 1)

    causal = jnp.tril(jnp.ones((L, L), dtype=bool))

    def step(S, blk):
        qc, kc, vc, gc = blk
        b = jnp.cumsum(gc, axis=1)
        b_last = b[:, -1]
        q_e = qc * jnp.exp(b)
        k_e = kc * jnp.exp(-b)
        att = jnp.where(causal, jnp.einsum('bihd,bjhd->bhij', q_e, k_e), 0.0)
        o = jnp.einsum('bhij,bjhv->bihv', att, vc) + jnp.einsum('bihd,bhdv->bihv', q_e, S)
        k_dec = kc * jnp.exp(b_last[:, None] - b)
        S = jnp.exp(b_last)[..., None] * S + jnp.einsum('bjhd,bjhv->bhdv', k_dec, vc)
        return S, o

    S, o = lax.scan(step, s0.astype(jnp.float32), (blocks(q), blocks(k), blocks(v), blocks(log_a)))
    return o.swapaxes(0, 1).reshape(B, T, H, DV), S


def gla_mixer(q, k, v, g, gk_low, gk_w2, gk_b, onorm, s0):
    B, T, _ = q.shape
    f32 = jnp.float32
    shp_k = (B, T, GLA_HEADS, GLA_HEAD_K)
    shp_v = (B, T, GLA_HEADS, GLA_HEAD_V)
    qh = q.astype(f32).reshape(shp_k) * GLA_HEAD_K ** -0.5
    kh = k.astype(f32).reshape(shp_k)
    vh = v.astype(f32).reshape(shp_v)
    log_a = jax.nn.log_sigmoid((gk_low @ gk_w2 + gk_b).astype(f32)).reshape(shp_k) / GATE_NORM
    o, s_new = gla_scan(qh, kh, vh, log_a, s0)
    o = rms_norm(o, onorm) * jax.nn.silu(g.astype(f32)).reshape(shp_v)
    return o.reshape(B, T, GLA_DV).astype(q.dtype), s_new


def band_attention(q, k, v, k_hist, v_hist, pos0, rel_bias):
    B, T, H, Dh = q.shape
    W = k_hist.shape[1]
    L = min(T, CHUNK)
    n = T // L
    k_all = jnp.concatenate([k_hist.astype(k.dtype), k], axis=1)
    v_all = jnp.concatenate([v_hist.astype(v.dtype), v], axis=1)
    scale = Dh ** -0.5

    def one_block(c):
        start = c * L
        qc = lax.dynamic_slice_in_dim(q, start, L, axis=1)
        kc = lax.dynamic_slice_in_dim(k_all, start, W + L, axis=1)
        vc = lax.dynamic_slice_in_dim(v_all, start, W + L, axis=1)
        q_pos = pos0 + start + jnp.arange(L)
        k_pos = pos0 - W + start + jnp.arange(W + L)
        q_chunk = q_pos[:, None] // CHUNK
        k_chunk = k_pos[None, :] // CHUNK
        allowed = (k_pos[None, :] >= 0) & (k_chunk <= q_chunk) & (k_chunk >= q_chunk - N_PREV_CHUNKS)
        rel = jnp.clip(q_pos[:, None] - k_pos[None, :], -(CHUNK - 1), MAX_REL) + (CHUNK - 1)
        bias = rel_bias[:, rel].astype(jnp.float32)
        s = jnp.einsum('bqhd,bkhd->bhqk', qc, kc).astype(jnp.float32) * scale + bias
        p = jax.nn.softmax(jnp.where(allowed, s, NEG_INF), axis=-1).astype(vc.dtype)
        return jnp.einsum('bhqk,bkhd->bqhd', p, vc)

    out = lax.map(one_block, jnp.arange(n))
    return out.swapaxes(0, 1).reshape(B, T, H, Dh)


def swiglu_ffn(h, w_in, w_out):
    gate, up = jnp.split(h @ w_in, 2, axis=-1)
    return (jax.nn.silu(gate) * up) @ w_out


def trunk(x, pos0, keep_rows, conv_prev, gla_prev, k_hist, v_hist,
          norm_mix, norm_ffn, w_in_ab, conv_w, gla_gk_w2, gla_gk_b, gla_onorm, w_out_ab,
          w_qkv, q_norm, k_norm, rel_bias, w_o_att, w_ffn_in, w_ffn_out):
    B, T, _ = x.shape
    split_at = [int(s) for s in np.cumsum(IN_SIZES)[:-1]]
    conv_new, gla_new, k_new, v_new = [], [], [], []
    for layer in range(DEPTH):
        h = rms_norm(x, norm_mix[layer])
        if layer % 2 == 0:
            e = layer // 2
            c_g, b_g, hc, q, k, v, g, gk_low = jnp.split(h @ w_in_ab[e], split_at, axis=-1)
            ya, cs = short_conv_mixer(c_g, b_g, hc, conv_w[e], conv_prev[e])
            yb, ss = gla_mixer(q, k, v, g, gk_low, gla_gk_w2[e], gla_gk_b[e], gla_onorm[e], gla_prev[e])
            x = x + jnp.concatenate([ya, yb], axis=-1) @ w_out_ab[e]
            conv_new.append(cs)
            gla_new.append(ss)
        else:
            o = layer // 2
            qkv = (h @ w_qkv[o]).reshape(B, T, 3, ATT_HEADS, ATT_HEAD_DIM)
            q = rms_norm(qkv[:, :, 0], q_norm[o])
            k = rms_norm(qkv[:, :, 1], k_norm[o])
            v = qkv[:, :, 2]
            att = band_attention(q, k, v, k_hist[o], v_hist[o], pos0, rel_bias[o])
            x = x + att.reshape(B, T, ATT_DIM) @ w_o_att[o]
            k_new.append(jnp.concatenate([k_hist[o].astype(k.dtype), k], axis=1)[:, -keep_rows:])
            v_new.append(jnp.concatenate([v_hist[o].astype(v.dtype), v], axis=1)[:, -keep_rows:])
        x = x + swiglu_ffn(rms_norm(x, norm_ffn[layer]), w_ffn_in[layer], w_ffn_out[layer])
    return x, jnp.stack(conv_new), jnp.stack(gla_new), jnp.stack(k_new), jnp.stack(v_new)


def setup_inputs(seed: int = 0) -> dict:
    key = jax.random.key(seed)
    ks = jax.random.split(key, 24)

    def nrm(k, shape, scale):
        return jax.random.normal(k, shape, jnp.float32) * scale

    win_rows = min(BAND_ROWS, PAST_LEN)
    return {
        "x_prompt": nrm(ks[0], (BATCH, SEQ, D_MODEL), 1.0),
        "x_sample": nrm(ks[1], (DEC_BATCH, DEC_SEQ, D_MODEL), 1.0),
        "state_conv": nrm(ks[2], (N_EVEN, DEC_BATCH, CONV_WIDTH - 1, CONV_DIM), 1.0),
        "state_gla": nrm(ks[3], (N_EVEN, DEC_BATCH, GLA_HEADS, GLA_HEAD_K, GLA_HEAD_V), 0.5),
        "cache_k": nrm(ks[4], (N_ODD, DEC_BATCH, win_rows, ATT_HEADS, ATT_HEAD_DIM), 1.0),
        "cache_v": nrm(ks[5], (N_ODD, DEC_BATCH, win_rows, ATT_HEADS, ATT_HEAD_DIM), 1.0),
        "norm_mix": 1.0 + nrm(ks[6], (DEPTH, D_MODEL), 0.02),
        "norm_ffn": 1.0 + nrm(ks[7], (DEPTH, D_MODEL), 0.02),
        "w_in_ab": nrm(ks[8], (N_EVEN, D_MODEL, IN_AB), D_MODEL ** -0.5),
        "conv_w": nrm(ks[9], (N_EVEN, CONV_WIDTH, CONV_DIM), CONV_WIDTH ** -0.5),
        "gla_gk_w2": nrm(ks[10], (N_EVEN, GATE_RANK, GLA_DK), GATE_RANK ** -0.5),
        "gla_gk_b": nrm(ks[11], (N_EVEN, GLA_DK), 0.1),
        "gla_onorm": 1.0 + nrm(ks[12], (N_EVEN, GLA_HEAD_V), 0.02),
        "w_out_ab": nrm(ks[13], (N_EVEN, MIX_AB, D_MODEL), MIX_AB ** -0.5),
        "w_qkv": nrm(ks[14], (N_ODD, D_MODEL, 3 * ATT_DIM), D_MODEL ** -0.5),
        "q_norm": 1.0 + nrm(ks[15], (N_ODD, ATT_HEAD_DIM), 0.02),
        "k_norm": 1.0 + nrm(ks[16], (N_ODD, ATT_HEAD_DIM), 0.02),
        "rel_bias": nrm(ks[17], (N_ODD, ATT_HEADS, REL_SIZE), 0.1),
        "w_o_att": nrm(ks[18], (N_ODD, ATT_DIM, D_MODEL), ATT_DIM ** -0.5),
        "w_ffn_in": nrm(ks[19], (DEPTH, D_MODEL, 2 * D_FF), D_MODEL ** -0.5),
        "w_ffn_out": nrm(ks[20], (DEPTH, D_FF, D_MODEL), D_FF ** -0.5),
    }


def reference(x_prompt, x_sample, state_conv, state_gla, cache_k, cache_v,
              norm_mix, norm_ffn, w_in_ab, conv_w, gla_gk_w2, gla_gk_b, gla_onorm, w_out_ab,
              w_qkv, q_norm, k_norm, rel_bias, w_o_att, w_ffn_in, w_ffn_out):
    B, T, _ = x_prompt.shape
    dt = x_prompt.dtype
    conv0 = jnp.zeros((N_EVEN, B, CONV_WIDTH - 1, CONV_DIM), dt)
    gla0 = jnp.zeros((N_EVEN, B, GLA_HEADS, GLA_HEAD_K, GLA_HEAD_V), jnp.float32)
    kv0 = jnp.zeros((N_ODD, B, BAND_ROWS, ATT_HEADS, ATT_HEAD_DIM), dt)
    y_prompt, conv_p, gla_p, k_p, v_p = trunk(
        x_prompt, 0, min(BAND_ROWS, T), conv0, gla0, kv0, kv0,
        norm_mix, norm_ffn, w_in_ab, conv_w, gla_gk_w2, gla_gk_b, gla_onorm, w_out_ab,
        w_qkv, q_norm, k_norm, rel_bias, w_o_att, w_ffn_in, w_ffn_out)
    y_sample, conv_s, gla_s, k_s, v_s = trunk(
        x_sample, PAST_LEN, cache_k.shape[2], state_conv, state_gla, cache_k, cache_v,
        norm_mix, norm_ffn, w_in_ab, conv_w, gla_gk_w2, gla_gk_b, gla_onorm, w_out_ab,
        w_qkv, q_norm, k_norm, rel_bias, w_o_att, w_ffn_in, w_ffn_out)
    return (y_prompt, y_sample, conv_p, gla_p, k_p, v_p, conv_s, gla_s, k_s, v_s)
```

```python
import functools

import jax
import jax.numpy as jnp
import numpy as np
from jax import lax
from jax.experimental import pallas as pl
from jax.experimental.pallas import tpu as pltpu

F32 = jnp.float32
BF16 = jnp.bfloat16

EPS = 1e-6
NEG_INF = -1e30
CHUNK = 64
CONV_WIDTH = 3
GLA_HEADS = 4
GATE_NORM = 16.0
ATT_HEAD_DIM = 64
BAND_ROWS = 8 * CHUNK
MAX_REL = 256
LANES = 128
VMEM_LIMIT = 56 * 1024 * 1024


def _params(*sem):
    return pltpu.CompilerParams(dimension_semantics=sem, vmem_limit_bytes=VMEM_LIMIT)


def _resident(shape):
    nd = len(shape)
    return pl.BlockSpec(shape, lambda *_: (0,) * nd, pipeline_mode=pl.Buffered(1))


def _dot(a, b):
    return jnp.dot(a, b, preferred_element_type=F32)


def _dot_nt(a, b):
    return lax.dot_general(a, b, (((1,), (1,)), ((), ())), preferred_element_type=F32)


def _dot_tn(a, b):
    return lax.dot_general(a, b, (((0,), (0,)), ((), ())), preferred_element_type=F32)


def _split2(a):
    hi = a.astype(BF16)
    lo = (a - hi.astype(F32)).astype(BF16)
    return hi, lo


def _split3(a):
    hi = a.astype(BF16)
    r = a - hi.astype(F32)
    mid = r.astype(BF16)
    lo = (r - mid.astype(F32)).astype(BF16)
    return hi, mid, lo


def _rms_rows(x, g):
    ms = jnp.mean(x * x, axis=-1, keepdims=True)
    return x * lax.rsqrt(ms + EPS) * g


def _silu(x):
    return x * jax.nn.sigmoid(x)


def _proj_body(x_ref, g_ref, w_ref, o_ref):
    h = _rms_rows(x_ref[...], g_ref[...]).astype(BF16)
    o_ref[...] = _dot(h, w_ref[...])


def _norm_proj(x, g, w, tm):
    n, d = x.shape
    nout = w.shape[1]
    return pl.pallas_call(
        _proj_body,
        grid=(n // tm,),
        in_specs=[
            pl.BlockSpec((tm, d), lambda i: (i, 0)),
            _resident((1, d)),
            _resident((d, nout)),
        ],
        out_specs=pl.BlockSpec((tm, nout), lambda i: (i, 0)),
        out_shape=jax.ShapeDtypeStruct((n, nout), F32),
        compiler_params=_params("parallel"),
    )(x, g, w)


def _qkv_body(x_ref, g_ref, w_ref, pool_ref, poolt_ref, gq_ref, gk_ref,
              q_ref, k_ref, v_ref, *, att_dim, q_scale):
    h = _rms_rows(x_ref[...], g_ref[...]).astype(BF16)
    qkv = _dot(h, w_ref[...])

    def head_norm(a, gt):
        hi, lo = _split2(a * a)
        ms = (_dot(hi, pool_ref[...]) + _dot(lo, pool_ref[...])) * (1.0 / ATT_HEAD_DIM)
        rhi, rlo = _split2(lax.rsqrt(ms + EPS))
        rb = _dot(rhi, poolt_ref[...]) + _dot(rlo, poolt_ref[...])
        return a * rb * gt

    q_ref[...] = (head_norm(qkv[:, :att_dim], gq_ref[...]) * q_scale).astype(BF16)
    k_ref[...] = head_norm(qkv[:, att_dim:2 * att_dim], gk_ref[...])
    v_ref[...] = qkv[:, 2 * att_dim:]


def _norm_qkv(x, g, w, pool, poolt, gq, gk, tm):
    n, d = x.shape
    att_dim = w.shape[1] // 3
    row = lambda i: (i, 0)
    return pl.pallas_call(
        functools.partial(_qkv_body, att_dim=att_dim, q_scale=ATT_HEAD_DIM ** -0.5),
        grid=(n // tm,),
        in_specs=[
            pl.BlockSpec((tm, d), row),
            _resident((1, d)),
            _resident(w.shape),
            _resident(pool.shape),
            _resident(poolt.shape),
            _resident((1, att_dim)),
            _resident((1, att_dim)),
        ],
        out_specs=[pl.BlockSpec((tm, att_dim), row)] * 3,
        out_shape=[
            jax.ShapeDtypeStruct((n, att_dim), BF16),
            jax.ShapeDtypeStruct((n, att_dim), F32),
            jax.ShapeDtypeStruct((n, att_dim), F32),
        ],
        compiler_params=_params("parallel"),
    )(x, g, w, pool, poolt, gq, gk)


def _post_body(x_ref, a_ref, wo_ref, g_ref, win_ref, wout_ref, o_ref, *, d_ff, ffc):
    x1 = x_ref[...] + _dot(a_ref[...], wo_ref[...])
    h = _rms_rows(x1, g_ref[...]).astype(BF16)
    acc = x1
    for c in range(d_ff // ffc):
        gate = _dot(h, win_ref[:, c * ffc:(c + 1) * ffc])
        up = _dot(h, win_ref[:, d_ff + c * ffc:d_ff + (c + 1) * ffc])
        act = (_silu(gate) * up).astype(BF16)
        acc = acc + _dot(act, wout_ref[c * ffc:(c + 1) * ffc, :])
    o_ref[...] = acc


def _post_mixer(x, a, wo, g, win, wout, tm, ffc):
    n, d = x.shape
    d_ff = wout.shape[0]
    row = lambda i: (i, 0)
    return pl.pallas_call(
        functools.partial(_post_body, d_ff=d_ff, ffc=ffc),
        grid=(n // tm,),
        in_specs=[
            pl.BlockSpec((tm, d), row),
            pl.BlockSpec((tm, a.shape[1]), row),
            _resident(wo.shape),
            _resident((1, d)),
            _resident(win.shape),
            _resident(wout.shape),
        ],
        out_specs=pl.BlockSpec((tm, d), row),
        out_shape=jax.ShapeDtypeStruct((n, d), F32),
        compiler_params=_params("parallel"),
    )(x, a, wo, g, win, wout)


def _mixer_body(p_ref, cw_ref, w2_ref, gb_ref, on_ref, conv0_ref, st0_ref,
                y_ref, convo_ref, sto_ref, ubuf, st_sc, *, tb, cdim, dk, dv):
    hk = dk // GLA_HEADS
    hv = dv // GLA_HEADS
    o_q = 3 * cdim
    o_k = o_q + dk
    o_v = o_k + dk
    o_g = o_v + dv
    o_l = o_g + dv
    t = pl.program_id(1)

    @pl.when(t == 0)
    def _():
        ubuf[0:8, :] = jnp.zeros((8, cdim), F32)
        ubuf[6:8, :] = conv0_ref[...]
        st_sc[...] = st0_ref[...]

    u = p_ref[:, 0:cdim] * p_ref[:, 2 * cdim:3 * cdim]
    ubuf[8:8 + tb, :] = u
    y = (cw_ref[0:1, :] * ubuf[6:6 + tb, :] + cw_ref[1:2, :] * ubuf[7:7 + tb, :]
         + cw_ref[2:3, :] * u)
    y_ref[:, 0:cdim] = (p_ref[:, cdim:2 * cdim] * y).astype(y_ref.dtype)
    tail = ubuf[tb + 6:tb + 8, :]
    ubuf[6:8, :] = tail
    convo_ref[...] = tail

    L = CHUNK
    assert hk == L and L & (L - 1) == 0
    sh = L.bit_length() - 1
    nst = GLA_HEADS * L
    r_i = lax.broadcasted_iota(jnp.int32, (nst, dk), 0)
    c_i = lax.broadcasted_iota(jnp.int32, (nst, dk), 1)
    same_head = (r_i >> sh) == (c_i >> sh)
    a_r = lax.broadcasted_iota(jnp.int32, (nst, nst), 0)
    a_c = lax.broadcasted_iota(jnp.int32, (nst, nst), 1)
    att_keep = ((a_r >> sh) == (a_c >> sh)) & ((a_c & (L - 1)) <= (a_r & (L - 1)))
    tri = (lax.broadcasted_iota(jnp.int32, (L, L), 1)
           <= lax.broadcasted_iota(jnp.int32, (L, L), 0)).astype(BF16)
    w2 = w2_ref[...]
    gb = gb_ref[...]
    onorm = on_ref[...]

    def chunk(c, carry):
        r0 = pl.multiple_of(c * L, L)
        rows = pl.ds(r0, L)
        q = p_ref[rows, o_q:o_k] * (hk ** -0.5)
        k = p_ref[rows, o_k:o_v]
        gl = p_ref[rows, o_l:o_l + LANES].astype(BF16)
        gk = _dot(gl, w2) + gb
        la = (jnp.minimum(gk, 0.0) - jnp.log1p(jnp.exp(-jnp.abs(gk)))) * (1.0 / GATE_NORM)
        l_hi, l_mid, l_lo = _split3(la)
        b = _dot(tri, l_hi) + _dot(tri, l_mid) + _dot(tri, l_lo)
        b_last = b[L - 1:L, :]
        q_e = q * jnp.exp(b)
        k_e = (k * jnp.exp(-b)).astype(BF16)
        k_dec = k * jnp.exp(b_last - b)

        q_st = jnp.where(same_head, jnp.concatenate([q_e] * GLA_HEADS, axis=0), 0.0).astype(BF16)
        k_tl = jnp.concatenate([k_e] * GLA_HEADS, axis=0)
        kd_bd = jnp.where(same_head, jnp.concatenate([k_dec] * GLA_HEADS, axis=0), 0.0).astype(BF16)
        v_st = jnp.concatenate(
            [p_ref[rows, o_v + h * hv:o_v + (h + 1) * hv] for h in range(GLA_HEADS)], axis=0
        ).astype(BF16)
        g_st = jnp.concatenate(
            [p_ref[rows, o_g + h * hv:o_g + (h + 1) * hv] for h in range(GLA_HEADS)], axis=0)

        st = st_sc[...]
        att = jnp.where(att_keep, _dot_nt(q_st, k_tl), 0.0).astype(BF16)
        o = _dot(att, v_st) + _dot_nt(q_st, st.astype(BF16))
        st_sc[...] = st * jnp.exp(b_last) + _dot_tn(v_st, kd_bd)

        o = _rms_rows(o, onorm) * _silu(g_st)
        for h in range(GLA_HEADS):
            y_ref[rows, cdim + h * hv:cdim + (h + 1) * hv] = o[h * L:(h + 1) * L, :].astype(y_ref.dtype)
        return carry

    lax.fori_loop(0, tb // L, chunk, 0)
    sto_ref[...] = st_sc[...]


def _mixer_ab(proj, conv_w, w2pad, gb, onorm, conv0, st0, tb):
    bsz, t, width = proj.shape
    cdim = conv_w.shape[1]
    dk = w2pad.shape[1]
    hv = onorm.shape[1]
    dv = hv * GLA_HEADS
    return pl.pallas_call(
        functools.partial(_mixer_body, tb=tb, cdim=cdim, dk=dk, dv=dv),
        grid=(bsz, t // tb),
        in_specs=[
            pl.BlockSpec((None, tb, width), lambda b, i: (b, i, 0)),
            _resident(conv_w.shape),
            _resident(w2pad.shape),
            _resident(gb.shape),
            _resident(onorm.shape),
            pl.BlockSpec((None, CONV_WIDTH - 1, cdim), lambda b, i: (b, 0, 0)),
            pl.BlockSpec((None, hv, dk), lambda b, i: (b, 0, 0)),
        ],
        out_specs=[
            pl.BlockSpec((None, tb, cdim + dv), lambda b, i: (b, i, 0)),
            pl.BlockSpec((None, CONV_WIDTH - 1, cdim), lambda b, i: (b, 0, 0)),
            pl.BlockSpec((None, hv, dk), lambda b, i: (b, 0, 0)),
        ],
        out_shape=[
            jax.ShapeDtypeStruct((bsz, t, cdim + dv), BF16),
            jax.ShapeDtypeStruct((bsz, CONV_WIDTH - 1, cdim), F32),
            jax.ShapeDtypeStruct((bsz, hv, dk), F32),
        ],
        scratch_shapes=[
            pltpu.VMEM((tb + 8, cdim), F32),
            pltpu.VMEM((hv, dk), F32),
        ],
        compiler_params=_params("parallel", "arbitrary"),
    )(proj, conv_w, w2pad, gb, onorm, conv0, st0)


def _attn_body(q_ref, ka_ref, kb_ref, va_ref, vb_ref, tab_ref, o_ref, kw, vw,
               *, tq, tqc, hist_is_padding):
    nwin = BAND_ROWS + tqc
    kw[0:BAND_ROWS, :] = ka_ref[...].astype(BF16)
    kw[BAND_ROWS:, :] = kb_ref[...].astype(BF16)
    vw[0:BAND_ROWS, :] = va_ref[...].astype(BF16)
    vw[BAND_ROWS:, :] = vb_ref[...].astype(BF16)
    n_pairs = q_ref.shape[1] // LANES
    lo = lax.broadcasted_iota(jnp.int32, (tqc, LANES), 1) < ATT_HEAD_DIM
    col = lax.broadcasted_iota(jnp.int32, (tqc, nwin), 1)
    blk0 = pl.program_id(1) * tq

    def sub_block(j, carry):
        r0 = pl.multiple_of(j * tqc, tqc)
        if hist_is_padding:
            valid = col >= (BAND_ROWS - blk0 - r0)
        for p in range(n_pairs):
            lanes = slice(p * LANES, (p + 1) * LANES)
            qp = q_ref[pl.ds(r0, tqc), lanes]
            kp = kw[pl.ds(r0, nwin), lanes]
            vp = vw[pl.ds(r0, nwin), lanes]
            halves = []
            for half in range(2):
                qm = jnp.where(lo, qp, 0.0) if half == 0 else jnp.where(lo, 0.0, qp)
                s = _dot_nt(qm.astype(BF16), kp) + tab_ref[2 * p + half]
                if hist_is_padding:
                    s = jnp.where(valid, s, NEG_INF)
                e = jnp.exp(s - jnp.max(s, axis=-1, keepdims=True))
                den = jnp.sum(e, axis=-1, keepdims=True)
                halves.append(_dot(e.astype(BF16), vp) / den)
            o_ref[pl.ds(r0, tqc), lanes] = jnp.where(lo, halves[0], halves[1]).astype(o_ref.dtype)
        return carry

    lax.fori_loop(0, tq // tqc, sub_block, 0)


def _band_attention(q, k_all, v_all, tab, tq, tqc, hist_is_padding):
    bsz, t, width = q.shape
    nb = BAND_ROWS // tq if tq <= BAND_ROWS else None
    assert nb is not None and BAND_ROWS % tq == 0
    return pl.pallas_call(
        functools.partial(_attn_body, tq=tq, tqc=tqc, hist_is_padding=hist_is_padding),
        grid=(bsz, t // tq),
        in_specs=[
            pl.BlockSpec((None, tq, width), lambda b, i: (b, i, 0)),
            pl.BlockSpec((None, BAND_ROWS, width), lambda b, i: (b, i, 0)),
            pl.BlockSpec((None, tq, width), lambda b, i: (b, i + nb, 0)),
            pl.BlockSpec((None, BAND_ROWS, width), lambda b, i: (b, i, 0)),
            pl.BlockSpec((None, tq, width), lambda b, i: (b, i + nb, 0)),
            _resident(tab.shape),
        ],
        out_specs=pl.BlockSpec((None, tq, width), lambda b, i: (b, i, 0)),
        out_shape=jax.ShapeDtypeStruct((bsz, t, width), BF16),
        scratch_shapes=[
            pltpu.VMEM((BAND_ROWS + tq, width), BF16),
            pltpu.VMEM((BAND_ROWS + tq, width), BF16),
        ],
        compiler_params=_params("parallel", "arbitrary"),
    )(q, k_all, k_all, v_all, v_all, tab)


def _bias_table(rel_bias, tqc):
    i = np.arange(tqc)[:, None]
    c = np.arange(BAND_ROWS + tqc)[None, :]
    rel = np.clip(i + BAND_ROWS - c, -(CHUNK - 1), MAX_REL) + (CHUNK - 1)
    q_chunk = i // CHUNK
    k_chunk = (c - BAND_ROWS) // CHUNK
    allowed = (k_chunk <= q_chunk) & (k_chunk >= q_chunk - BAND_ROWS // CHUNK)
    return jnp.where(allowed[None], rel_bias[:, rel].astype(F32), NEG_INF)


def _trunk(x, hist_is_padding, conv_prev, gla_prev, k_hist, v_hist, w):
    bsz, t, d = x.shape
    n = bsz * t
    depth = w["norm_mix"].shape[0]
    tm = min(512, n)
    tb = min(512, t)
    tq = min(BAND_ROWS, t)
    tqc = min(2 * CHUNK, t)
    keep = k_hist.shape[2] if not hist_is_padding else min(BAND_ROWS, t)
    heads = w["rel_bias"].shape[1]
    xf = x.reshape(n, d)
    conv_new, gla_new, k_new, v_new = [], [], [], []
    for layer in range(depth):
        g_mix = w["norm_mix"][layer][None, :]
        g_ffn = w["norm_ffn"][layer][None, :]
        if layer % 2 == 0:
            e = layer // 2
            proj = _norm_proj(xf, g_mix, w["w_in_ab"][e], tm)
            st0 = gla_prev[e].reshape(bsz, -1, gla_prev.shape[-1]).swapaxes(1, 2)
            mix, cs, st = _mixer_ab(
                proj.reshape(bsz, t, -1), w["conv_w"][e], w["gk_w2"][e], w["gk_b"][e][None, :],
                w["gla_onorm"][e][None, :], conv_prev[e], st0, tb)
            conv_new.append(cs)
            gla_new.append(st.swapaxes(1, 2).reshape(gla_prev.shape[1:]))
            mix = mix.reshape(n, -1)
            w_o = w["w_out_ab"][e]
        else:
            o = layer // 2
            q, k, v = _norm_qkv(xf, g_mix, w["w_qkv"][o], w["pool"], w["poolt"],
                                w["q_norm"][o], w["k_norm"][o], tm)
            att_dim = k.shape[-1]
            k_all = jnp.concatenate([k_hist[o].reshape(bsz, -1, att_dim), k.reshape(bsz, t, -1)], axis=1)
            v_all = jnp.concatenate([v_hist[o].reshape(bsz, -1, att_dim), v.reshape(bsz, t, -1)], axis=1)
            tab = _bias_table(w["rel_bias"][o], tqc)
            att = _band_attention(q.reshape(bsz, t, -1), k_all, v_all, tab, tq, tqc, hist_is_padding)
            k_new.append(k_all[:, -keep:].reshape(bsz, keep, heads, -1))
            v_new.append(v_all[:, -keep:].reshape(bsz, keep, heads, -1))
            mix = att.reshape(n, -1)
            w_o = w["w_o_att"][o]
        xf = _post_mixer(xf, mix, w_o, g_ffn, w["w_ffn_in"][layer], w["w_ffn_out"][layer], tm, 1408)
    return (xf.reshape(bsz, t, d), jnp.stack(conv_new), jnp.stack(gla_new),
            jnp.stack(k_new), jnp.stack(v_new))


def kernel(x_prompt, x_sample, state_conv, state_gla, cache_k, cache_v, norm_mix, norm_ffn,
           w_in_ab, conv_w, gla_gk_w2, gla_gk_b, gla_onorm, w_out_ab, w_qkv, q_norm, k_norm,
           rel_bias, w_o_att, w_ffn_in, w_ffn_out):
    bsz = x_prompt.shape[0]
    n_even = state_conv.shape[0]
    n_odd = cache_k.shape[0]
    heads, head_dim = cache_k.shape[-2:]
    att_dim = heads * head_dim
    in_ab = w_in_ab.shape[-1]
    in_pad = -in_ab % LANES
    rank = gla_gk_w2.shape[1]
    pool = (np.arange(att_dim)[:, None] // head_dim == np.arange(LANES)[None, :])
    w = {
        "norm_mix": norm_mix, "norm_ffn": norm_ffn,
        "w_in_ab": jnp.pad(w_in_ab, ((0, 0), (0, 0), (0, in_pad))).astype(BF16),
        "conv_w": conv_w,
        "gk_w2": jnp.pad(gla_gk_w2, ((0, 0), (0, LANES - rank), (0, 0))).astype(BF16),
        "gk_b": gla_gk_b, "gla_onorm": gla_onorm,
        "w_out_ab": w_out_ab.astype(BF16),
        "w_qkv": w_qkv.astype(BF16),
        "q_norm": jnp.tile(q_norm, (1, heads))[:, None, :],
        "k_norm": jnp.tile(k_norm, (1, heads))[:, None, :],
        "rel_bias": rel_bias,
        "w_o_att": w_o_att.astype(BF16),
        "w_ffn_in": w_ffn_in.astype(BF16), "w_ffn_out": w_ffn_out.astype(BF16),
        "pool": jnp.asarray(pool, BF16), "poolt": jnp.asarray(pool.T, BF16),
    }
    conv0 = jnp.zeros((n_even, bsz) + state_conv.shape[2:], x_prompt.dtype)
    gla0 = jnp.zeros((n_even, bsz) + state_gla.shape[2:], F32)
    kv0 = jnp.zeros((n_odd, bsz, BAND_ROWS, heads, head_dim), x_prompt.dtype)
    y_p, conv_p, gla_p, k_p, v_p = _trunk(x_prompt, True, conv0, gla0, kv0, kv0, w)
    y_s, conv_s, gla_s, k_s, v_s = _trunk(x_sample, False, state_conv, state_gla, cache_k, cache_v, w)
    return (y_p, y_s, conv_p, gla_p, k_p, v_p, conv_s, gla_s, k_s, v_s)
```

```python
import functools

import jax
import jax.numpy as jnp
import numpy as np
from jax import lax
from jax.experimental import pallas as pl
from jax.experimental.pallas import tpu as pltpu

F32 = jnp.float32
BF16 = jnp.bfloat16

EPS = 1e-6
NEG_INF = -1e30
CHUNK = 64
CONV_WIDTH = 3
GLA_HEADS = 4
GATE_NORM = 16.0
ATT_HEAD_DIM = 64
BAND_ROWS = 8 * CHUNK
MAX_REL = 256
LANES = 128
VMEM_LIMIT = 56 * 1024 * 1024


def _params(*sem):
    return pltpu.CompilerParams(dimension_semantics=sem, vmem_limit_bytes=VMEM_LIMIT)


def _resident(shape):
    nd = len(shape)
    return pl.BlockSpec(shape, lambda *_: (0,) * nd, pipeline_mode=pl.Buffered(1))


def _dot(a, b):
    return jnp.dot(a, b, preferred_element_type=F32)


def _dot_nt(a, b):
    return lax.dot_general(a, b, (((1,), (1,)), ((), ())), preferred_element_type=F32)


def _dot_tn(a, b):
    return lax.dot_general(a, b, (((0,), (0,)), ((), ())), preferred_element_type=F32)


def _split2(a):
    hi = a.astype(BF16)
    lo = (a - hi.astype(F32)).astype(BF16)
    return hi, lo


def _split3(a):
    hi = a.astype(BF16)
    r = a - hi.astype(F32)
    mid = r.astype(BF16)
    lo = (r - mid.astype(F32)).astype(BF16)
    return hi, mid, lo


def _rms_rows(x, g):
    ms = jnp.mean(x * x, axis=-1, keepdims=True)
    return x * lax.rsqrt(ms + EPS) * g


def _silu(x):
    return x * jax.nn.sigmoid(x)


def _proj_body(x_ref, g_ref, w_ref, o_ref):
    h = _rms_rows(x_ref[...], g_ref[...]).astype(BF16)
    o_ref[...] = _dot(h, w_ref[...])


def _norm_proj(x, g, w, tm):
    n, d = x.shape
    nout = w.shape[1]
    return pl.pallas_call(
        _proj_body,
        grid=(n // tm,),
        in_specs=[
            pl.BlockSpec((tm, d), lambda i: (i, 0)),
            _resident((1, d)),
            _resident((d, nout)),
        ],
        out_specs=pl.BlockSpec((tm, nout), lambda i: (i, 0)),
        out_shape=jax.ShapeDtypeStruct((n, nout), F32),
        compiler_params=_params("parallel"),
    )(x, g, w)


def _qkv_body(x_ref, g_ref, w_ref, pool_ref, poolt_ref, gq_ref, gk_ref,
              q_ref, k_ref, v_ref, kt_ref, vt_ref, *, att_dim, q_scale):
    h = _rms_rows(x_ref[...], g_ref[...]).astype(BF16)
    qkv = _dot(h, w_ref[...])

    def head_norm(a, gt):
        hi, lo = _split2(a * a)
        ms = (_dot(hi, pool_ref[...]) + _dot(lo, pool_ref[...])) * (1.0 / ATT_HEAD_DIM)
        rhi, rlo = _split2(lax.rsqrt(ms + EPS))
        rb = _dot(rhi, poolt_ref[...]) + _dot(rlo, poolt_ref[...])
        return a * rb * gt

    q_ref[...] = (head_norm(qkv[:, :att_dim], gq_ref[...]) * q_scale).astype(BF16)
    kn = head_norm(qkv[:, att_dim:2 * att_dim], gk_ref[...])
    v = qkv[:, 2 * att_dim:]
    k_ref[...] = kn.astype(BF16)
    v_ref[...] = v.astype(BF16)
    kt_ref[...] = kn
    vt_ref[...] = v


def _norm_qkv(x, g, w, pool, poolt, gq, gk, tm, tail_div):
    n, d = x.shape
    att_dim = w.shape[1] // 3
    row = lambda i: (i, 0)
    tail = lambda i: (i // tail_div, 0)
    return pl.pallas_call(
        functools.partial(_qkv_body, att_dim=att_dim, q_scale=ATT_HEAD_DIM ** -0.5),
        grid=(n // tm,),
        in_specs=[
            pl.BlockSpec((tm, d), row),
            _resident((1, d)),
            _resident(w.shape),
            _resident(pool.shape),
            _resident(poolt.shape),
            _resident((1, att_dim)),
            _resident((1, att_dim)),
        ],
        out_specs=[pl.BlockSpec((tm, att_dim), row)] * 3 + [pl.BlockSpec((tm, att_dim), tail)] * 2,
        out_shape=[jax.ShapeDtypeStruct((n, att_dim), BF16)] * 3
        + [jax.ShapeDtypeStruct((n // tail_div, att_dim), F32)] * 2,
        compiler_params=_params("arbitrary"),
    )(x, g, w, pool, poolt, gq, gk)


def _post_body(x_ref, a_ref, wo_ref, g_ref, win_ref, wout_ref, o_ref, *, d_ff, ffc):
    x1 = x_ref[...] + _dot(a_ref[...], wo_ref[...])
    h = _rms_rows(x1, g_ref[...]).astype(BF16)
    acc = x1
    for c in range(d_ff // ffc):
        gate = _dot(h, win_ref[:, c * ffc:(c + 1) * ffc])
        up = _dot(h, win_ref[:, d_ff + c * ffc:d_ff + (c + 1) * ffc])
        act = (_silu(gate) * up).astype(BF16)
        acc = acc + _dot(act, wout_ref[c * ffc:(c + 1) * ffc, :])
    o_ref[...] = acc


def _post_mixer(x, a, wo, g, win, wout, tm, ffc):
    n, d = x.shape
    d_ff = wout.shape[0]
    row = lambda i: (i, 0)
    return pl.pallas_call(
        functools.partial(_post_body, d_ff=d_ff, ffc=ffc),
        grid=(n // tm,),
        in_specs=[
            pl.BlockSpec((tm, d), row),
            pl.BlockSpec((tm, a.shape[1]), row),
            _resident(wo.shape),
            _resident((1, d)),
            _resident(win.shape),
            _resident(wout.shape),
        ],
        out_specs=pl.BlockSpec((tm, d), row),
        out_shape=jax.ShapeDtypeStruct((n, d), F32),
        compiler_params=_params("parallel"),
    )(x, a, wo, g, win, wout)


def _mixer_body(p_ref, cw_ref, w2_ref, gb_ref, on_ref, conv0_ref, st0_ref,
                y_ref, convo_ref, sto_ref, ubuf, st_sc, *, tb, cdim, dk, dv):
    hk = dk // GLA_HEADS
    hv = dv // GLA_HEADS
    o_q = 3 * cdim
    o_k = o_q + dk
    o_v = o_k + dk
    o_g = o_v + dv
    o_l = o_g + dv
    t = pl.program_id(1)

    @pl.when(t == 0)
    def _():
        ubuf[0:8, :] = jnp.zeros((8, cdim), F32)
        ubuf[6:8, :] = conv0_ref[...]
        st_sc[...] = st0_ref[...]

    u = p_ref[:, 0:cdim] * p_ref[:, 2 * cdim:3 * cdim]
    ubuf[8:8 + tb, :] = u
    y = (cw_ref[0:1, :] * ubuf[6:6 + tb, :] + cw_ref[1:2, :] * ubuf[7:7 + tb, :]
         + cw_ref[2:3, :] * u)
    y_ref[:, 0:cdim] = (p_ref[:, cdim:2 * cdim] * y).astype(y_ref.dtype)
    tail = ubuf[tb + 6:tb + 8, :]
    ubuf[6:8, :] = tail
    convo_ref[...] = tail

    L = CHUNK
    assert hk == L and L & (L - 1) == 0
    sh = L.bit_length() - 1
    nst = GLA_HEADS * L
    r_i = lax.broadcasted_iota(jnp.int32, (nst, dk), 0)
    c_i = lax.broadcasted_iota(jnp.int32, (nst, dk), 1)
    same_head = (r_i >> sh) == (c_i >> sh)
    a_r = lax.broadcasted_iota(jnp.int32, (nst, nst), 0)
    a_c = lax.broadcasted_iota(jnp.int32, (nst, nst), 1)
    att_keep = ((a_r >> sh) == (a_c >> sh)) & ((a_c & (L - 1)) <= (a_r & (L - 1)))
    tri = (lax.broadcasted_iota(jnp.int32, (L, L), 1)
           <= lax.broadcasted_iota(jnp.int32, (L, L), 0)).astype(BF16)
    w2 = w2_ref[...]
    gb = gb_ref[...]
    onorm = on_ref[...]

    st = st_sc[...]
    for c in range(tb // L):
        rows = slice(c * L, (c + 1) * L)
        q = p_ref[rows, o_q:o_k] * (hk ** -0.5)
        k = p_ref[rows, o_k:o_v]
        gl = p_ref[rows, o_l:o_l + LANES].astype(BF16)
        gk = _dot(gl, w2) + gb
        la = (jnp.minimum(gk, 0.0) - jnp.log1p(jnp.exp(-jnp.abs(gk)))) * (1.0 / GATE_NORM)
        l_hi, l_mid, l_lo = _split3(la)
        b = _dot(tri, l_hi) + _dot(tri, l_mid) + _dot(tri, l_lo)
        b_last = b[L - 1:L, :]
        q_e = q * jnp.exp(b)
        k_e = (k * jnp.exp(-b)).astype(BF16)
        k_dec = k * jnp.exp(b_last - b)

        q_st = jnp.where(same_head, jnp.concatenate([q_e] * GLA_HEADS, axis=0), 0.0).astype(BF16)
        k_tl = jnp.concatenate([k_e] * GLA_HEADS, axis=0)
        kd_bd = jnp.where(same_head, jnp.concatenate([k_dec] * GLA_HEADS, axis=0), 0.0).astype(BF16)
        v_st = jnp.concatenate(
            [p_ref[rows, o_v + h * hv:o_v + (h + 1) * hv] for h in range(GLA_HEADS)], axis=0
        ).astype(BF16)
        g_st = jnp.concatenate(
            [p_ref[rows, o_g + h * hv:o_g + (h + 1) * hv] for h in range(GLA_HEADS)], axis=0)

        att = jnp.where(att_keep, _dot_nt(q_st, k_tl), 0.0).astype(BF16)
        o = _dot(att, v_st) + _dot_nt(q_st, st.astype(BF16))
        st = st * jnp.exp(b_last) + _dot_tn(v_st, kd_bd)

        o = _rms_rows(o, onorm) * _silu(g_st)
        for h in range(GLA_HEADS):
            y_ref[rows, cdim + h * hv:cdim + (h + 1) * hv] = o[h * L:(h + 1) * L, :].astype(y_ref.dtype)
    st_sc[...] = st
    sto_ref[...] = st


def _mixer_ab(proj, conv_w, w2pad, gb, onorm, conv0, st0, tb):
    bsz, t, width = proj.shape
    cdim = conv_w.shape[1]
    dk = w2pad.shape[1]
    hv = onorm.shape[1]
    dv = hv * GLA_HEADS
    return pl.pallas_call(
        functools.partial(_mixer_body, tb=tb, cdim=cdim, dk=dk, dv=dv),
        grid=(bsz, t // tb),
        in_specs=[
            pl.BlockSpec((None, tb, width), lambda b, i: (b, i, 0)),
            _resident(conv_w.shape),
            _resident(w2pad.shape),
            _resident(gb.shape),
            _resident(onorm.shape),
            pl.BlockSpec((None, CONV_WIDTH - 1, cdim), lambda b, i: (b, 0, 0)),
            pl.BlockSpec((None, hv, dk), lambda b, i: (b, 0, 0)),
        ],
        out_specs=[
            pl.BlockSpec((None, tb, cdim + dv), lambda b, i: (b, i, 0)),
            pl.BlockSpec((None, CONV_WIDTH - 1, cdim), lambda b, i: (b, 0, 0)),
            pl.BlockSpec((None, hv, dk), lambda b, i: (b, 0, 0)),
        ],
        out_shape=[
            jax.ShapeDtypeStruct((bsz, t, cdim + dv), BF16),
            jax.ShapeDtypeStruct((bsz, CONV_WIDTH - 1, cdim), F32),
            jax.ShapeDtypeStruct((bsz, hv, dk), F32),
        ],
        scratch_shapes=[
            pltpu.VMEM((tb + 8, cdim), F32),
            pltpu.VMEM((hv, dk), F32),
        ],
        compiler_params=_params("parallel", "arbitrary"),
    )(proj, conv_w, w2pad, gb, onorm, conv0, st0)


def _attn_body(q_ref, ka_ref, kb_ref, va_ref, vb_ref, tab_ref, o_ref, kw, vw,
               *, tq, tqc, hist_is_padding):
    nwin = BAND_ROWS + tqc
    kw[0:BAND_ROWS, :] = ka_ref[...]
    kw[BAND_ROWS:, :] = kb_ref[...]
    vw[0:BAND_ROWS, :] = va_ref[...]
    vw[BAND_ROWS:, :] = vb_ref[...]
    n_pairs = q_ref.shape[1] // LANES
    lo = lax.broadcasted_iota(jnp.int32, (tqc, LANES), 1) < ATT_HEAD_DIM
    keep_q = ((lax.broadcasted_iota(jnp.int32, (2 * tqc, LANES), 1) < ATT_HEAD_DIM)
              == (lax.broadcasted_iota(jnp.int32, (2 * tqc, LANES), 0) < tqc))

    def run(mask_padding):
        def sub_block(j, carry):
            r0 = pl.multiple_of(j * tqc, tqc)
            if mask_padding:
                col = lax.broadcasted_iota(jnp.int32, (2 * tqc, nwin), 1)
                valid = col >= (BAND_ROWS - r0)
            for p in range(n_pairs):
                lanes = slice(p * LANES, (p + 1) * LANES)
                qp = q_ref[pl.ds(r0, tqc), lanes]
                kp = kw[pl.ds(r0, nwin), lanes]
                vp = vw[pl.ds(r0, nwin), lanes]
                q2 = jnp.where(keep_q, jnp.concatenate([qp, qp], axis=0), 0.0)
                s = _dot_nt(q2, kp) + tab_ref[p]
                if mask_padding:
                    s = jnp.where(valid, s, NEG_INF)
                e = jnp.exp(s - jnp.max(s, axis=-1, keepdims=True))
                den = jnp.sum(e, axis=-1, keepdims=True)
                pv = _dot(e.astype(BF16), vp) / den
                o_ref[pl.ds(r0, tqc), lanes] = jnp.where(lo, pv[:tqc], pv[tqc:]).astype(o_ref.dtype)
            return carry

        lax.fori_loop(0, tq // tqc, sub_block, 0)

    if hist_is_padding:
        first = pl.program_id(1) == 0
        pl.when(first)(lambda: run(True))
        pl.when(jnp.logical_not(first))(lambda: run(False))
    else:
        run(False)


def _band_attention(q, k_hist, k_cur, v_hist, v_cur, tab, tq, tqc, hist_is_padding):
    bsz, t, width = q.shape
    assert t % tq == 0 and tq % tqc == 0
    if hist_is_padding:
        assert tq == BAND_ROWS
        k_hist, v_hist = k_cur, v_cur
        hist_map = lambda b, i: (b, jnp.maximum(i - 1, 0), 0)
    else:
        assert t == tq and k_hist.shape[1] == BAND_ROWS
        hist_map = lambda b, i: (b, 0, 0)
    cur_map = lambda b, i: (b, i, 0)
    return pl.pallas_call(
        functools.partial(_attn_body, tq=tq, tqc=tqc, hist_is_padding=hist_is_padding),
        grid=(bsz, t // tq),
        in_specs=[
            pl.BlockSpec((None, tq, width), cur_map),
            pl.BlockSpec((None, BAND_ROWS, width), hist_map),
            pl.BlockSpec((None, tq, width), cur_map),
            pl.BlockSpec((None, BAND_ROWS, width), hist_map),
            pl.BlockSpec((None, tq, width), cur_map),
            _resident(tab.shape),
        ],
        out_specs=pl.BlockSpec((None, tq, width), cur_map),
        out_shape=jax.ShapeDtypeStruct((bsz, t, width), BF16),
        scratch_shapes=[
            pltpu.VMEM((BAND_ROWS + tq, width), BF16),
            pltpu.VMEM((BAND_ROWS + tq, width), BF16),
        ],
        compiler_params=_params("parallel", "arbitrary"),
    )(q, k_hist, k_cur, v_hist, v_cur, tab)


def _bias_table(rel_bias, tqc):
    heads = rel_bias.shape[0]
    nwin = BAND_ROWS + tqc
    e = np.concatenate([np.arange(nwin), np.arange(-(tqc - 1), 0)])
    rel = np.clip(BAND_ROWS - e, -(CHUNK - 1), MAX_REL) + (CHUNK - 1)
    period = rel_bias[:, rel].astype(F32)
    p_len = period.shape[1]
    flat = jnp.tile(period, (1, tqc))[:, :tqc * (p_len - 1)]
    toep = flat.reshape(heads, tqc, p_len - 1)[:, :, :nwin]
    i = np.arange(tqc)[:, None]
    c = np.arange(nwin)[None, :]
    q_chunk = i // CHUNK
    k_chunk = (c - BAND_ROWS) // CHUNK
    allowed = (k_chunk <= q_chunk) & (k_chunk >= q_chunk - BAND_ROWS // CHUNK)
    return jnp.where(allowed[None], toep, NEG_INF).reshape(heads // 2, 2 * tqc, nwin)


def _trunk(x, hist_is_padding, conv_prev, gla_prev, k_hist, v_hist, w):
    bsz, t, d = x.shape
    n = bsz * t
    depth = w["norm_mix"].shape[0]
    tm = min(512, n)
    tb = min(512, t)
    tq = min(BAND_ROWS, t)
    tqc = min(2 * CHUNK, t)
    if hist_is_padding:
        keep = BAND_ROWS
        assert t % tm == 0 and tm == keep
        tail_div = t // tm
    else:
        keep = k_hist.shape[2]
        assert n == tm and keep == BAND_ROWS and t <= keep
        tail_div = 1
    heads = w["rel_bias"].shape[1]
    xf = x.reshape(n, d)
    conv_new, gla_new, k_new, v_new = [], [], [], []
    for layer in range(depth):
        g_mix = w["norm_mix"][layer][None, :]
        g_ffn = w["norm_ffn"][layer][None, :]
        if layer % 2 == 0:
            e = layer // 2
            proj = _norm_proj(xf, g_mix, w["w_in_ab"][e], tm)
            st0 = gla_prev[e].reshape(bsz, -1, gla_prev.shape[-1]).swapaxes(1, 2)
            mix, cs, st = _mixer_ab(
                proj.reshape(bsz, t, -1), w["conv_w"][e], w["gk_w2"][e], w["gk_b"][e][None, :],
                w["gla_onorm"][e][None, :], conv_prev[e], st0, tb)
            conv_new.append(cs)
            gla_new.append(st.swapaxes(1, 2).reshape(gla_prev.shape[1:]))
            mix = mix.reshape(n, -1)
            w_o = w["w_out_ab"][e]
        else:
            o = layer // 2
            q, k, v, kt, vt = _norm_qkv(xf, g_mix, w["w_qkv"][o], w["pool"], w["poolt"],
                                        w["q_norm"][o], w["k_norm"][o], tm, tail_div)
            att_dim = k.shape[-1]
            shp = (bsz, t, att_dim)
            if hist_is_padding:
                kh = vh = None
            else:
                kh = k_hist[o].reshape(bsz, -1, att_dim).astype(BF16)
                vh = v_hist[o].reshape(bsz, -1, att_dim).astype(BF16)
            tab = _bias_table(w["rel_bias"][o], tqc)
            att = _band_attention(q.reshape(shp), kh, k.reshape(shp), vh, v.reshape(shp), tab,
                                  tq, tqc, hist_is_padding)
            if hist_is_padding:
                k_new.append(kt.reshape(bsz, keep, heads, -1))
                v_new.append(vt.reshape(bsz, keep, heads, -1))
            else:
                k_new.append(jnp.concatenate([k_hist[o][:, t:], kt.reshape(bsz, t, heads, -1)], axis=1))
                v_new.append(jnp.concatenate([v_hist[o][:, t:], vt.reshape(bsz, t, heads, -1)], axis=1))
            mix = att.reshape(n, -1)
            w_o = w["w_o_att"][o]
        xf = _post_mixer(xf, mix, w_o, g_ffn, w["w_ffn_in"][layer], w["w_ffn_out"][layer], tm, 1408)
    return (xf.reshape(bsz, t, d), jnp.stack(conv_new), jnp.stack(gla_new),
            jnp.stack(k_new), jnp.stack(v_new))


def kernel(x_prompt, x_sample, state_conv, state_gla, cache_k, cache_v, norm_mix, norm_ffn,
           w_in_ab, conv_w, gla_gk_w2, gla_gk_b, gla_onorm, w_out_ab, w_qkv, q_norm, k_norm,
           rel_bias, w_o_att, w_ffn_in, w_ffn_out):
    bsz = x_prompt.shape[0]
    n_even = state_conv.shape[0]
    n_odd = cache_k.shape[0]
    heads, head_dim = cache_k.shape[-2:]
    att_dim = heads * head_dim
    in_ab = w_in_ab.shape[-1]
    in_pad = -in_ab % LANES
    rank = gla_gk_w2.shape[1]
    pool = (np.arange(att_dim)[:, None] // head_dim == np.arange(LANES)[None, :])
    w = {
        "norm_mix": norm_mix, "norm_ffn": norm_ffn,
        "w_in_ab": jnp.pad(w_in_ab, ((0, 0), (0, 0), (0, in_pad))).astype(BF16),
        "conv_w": conv_w,
        "gk_w2": jnp.pad(gla_gk_w2, ((0, 0), (0, LANES - rank), (0, 0))).astype(BF16),
        "gk_b": gla_gk_b, "gla_onorm": gla_onorm,
        "w_out_ab": w_out_ab.astype(BF16),
        "w_qkv": w_qkv.astype(BF16),
        "q_norm": jnp.tile(q_norm, (1, heads))[:, None, :],
        "k_norm": jnp.tile(k_norm, (1, heads))[:, None, :],
        "rel_bias": rel_bias,
        "w_o_att": w_o_att.astype(BF16),
        "w_ffn_in": w_ffn_in.astype(BF16), "w_ffn_out": w_ffn_out.astype(BF16),
        "pool": jnp.asarray(pool, BF16), "poolt": jnp.asarray(pool.T, BF16),
    }
    conv0 = jnp.zeros((n_even, bsz) + state_conv.shape[2:], x_prompt.dtype)
    gla0 = jnp.zeros((n_even, bsz) + state_gla.shape[2:], F32)
    kv0 = jnp.zeros((n_odd, bsz, BAND_ROWS, heads, head_dim), x_prompt.dtype)
    y_p, conv_p, gla_p, k_p, v_p = _trunk(x_prompt, True, conv0, gla0, kv0, kv0, w)
    y_s, conv_s, gla_s, k_s, v_s = _trunk(x_sample, False, state_conv, state_gla, cache_k, cache_v, w)
    return (y_p, y_s, conv_p, gla_p, k_p, v_p, conv_s, gla_s, k_s, v_s)
```

```python
import functools

import jax
import jax.numpy as jnp
import numpy as np
from jax import lax
from jax.experimental import pallas as pl
from jax.experimental.pallas import tpu as pltpu

F32 = jnp.float32
BF16 = jnp.bfloat16

EPS = 1e-6
NEG_INF = -1e30
CHUNK = 64
CONV_WIDTH = 3
GLA_HEADS = 4
GATE_NORM = 16.0
ATT_HEAD_DIM = 64
BAND_ROWS = 8 * CHUNK
MAX_REL = 256
LANES = 128
MXU_DIM = 256
VMEM_LIMIT = 56 * 1024 * 1024


def _params(*sem):
    return pltpu.CompilerParams(dimension_semantics=sem, vmem_limit_bytes=VMEM_LIMIT)


def _resident(shape):
    nd = len(shape)
    return pl.BlockSpec(shape, lambda *_: (0,) * nd, pipeline_mode=pl.Buffered(1))


def _dot(a, b):
    return jnp.dot(a, b, preferred_element_type=F32)


def _dot_nt(a, b):
    return lax.dot_general(a, b, (((1,), (1,)), ((), ())), preferred_element_type=F32)


def _dot_tn(a, b):
    return lax.dot_general(a, b, (((0,), (0,)), ((), ())), preferred_element_type=F32)


def _split2(a):
    hi = a.astype(BF16)
    lo = (a - hi.astype(F32)).astype(BF16)
    return hi, lo


def _split3(a):
    hi = a.astype(BF16)
    r = a - hi.astype(F32)
    mid = r.astype(BF16)
    lo = (r - mid.astype(F32)).astype(BF16)
    return hi, mid, lo


def _rms_rows(x, g):
    ms = jnp.mean(x * x, axis=-1, keepdims=True)
    return x * lax.rsqrt(ms + EPS) * g


def _silu(x):
    return x * jax.nn.sigmoid(x)


def _proj_body(x_ref, g_ref, w_ref, o_ref):
    h = _rms_rows(x_ref[...], g_ref[...]).astype(BF16)
    o_ref[...] = _dot(h, w_ref[...])


def _norm_proj(x, g, w, tm):
    n, d = x.shape
    nout = w.shape[1]
    return pl.pallas_call(
        _proj_body,
        grid=(n // tm,),
        in_specs=[
            pl.BlockSpec((tm, d), lambda i: (i, 0)),
            _resident((1, d)),
            _resident((d, nout)),
        ],
        out_specs=pl.BlockSpec((tm, nout), lambda i: (i, 0)),
        out_shape=jax.ShapeDtypeStruct((n, nout), F32),
        compiler_params=_params("parallel"),
    )(x, g, w)


def _qkv_body(x_ref, g_ref, w_ref, pool_ref, poolt_ref, gq_ref, gk_ref,
              q_ref, k_ref, v_ref, kt_ref, vt_ref, *, att_dim, q_scale):
    h = _rms_rows(x_ref[...], g_ref[...]).astype(BF16)
    qkv = _dot(h, w_ref[...])

    def head_norm(a, gt):
        ms = _dot((a * a).astype(BF16), pool_ref[...]) * (1.0 / ATT_HEAD_DIM)
        rhi, rlo = _split2(lax.rsqrt(ms + EPS))
        rb = _dot(rhi, poolt_ref[...]) + _dot(rlo, poolt_ref[...])
        return a * rb * gt

    q_ref[...] = (head_norm(qkv[:, :att_dim], gq_ref[...]) * q_scale).astype(BF16)
    kn = head_norm(qkv[:, att_dim:2 * att_dim], gk_ref[...])
    v = qkv[:, 2 * att_dim:]
    k_ref[...] = kn.astype(BF16)
    v_ref[...] = v.astype(BF16)
    kt_ref[...] = kn
    vt_ref[...] = v


def _norm_qkv(x, g, w, pool, poolt, gq, gk, tm, tail_div):
    n, d = x.shape
    att_dim = w.shape[1] // 3
    row = lambda i: (i, 0)
    tail = lambda i: (i // tail_div, 0)
    return pl.pallas_call(
        functools.partial(_qkv_body, att_dim=att_dim, q_scale=ATT_HEAD_DIM ** -0.5),
        grid=(n // tm,),
        in_specs=[
            pl.BlockSpec((tm, d), row),
            _resident((1, d)),
            _resident(w.shape),
            _resident(pool.shape),
            _resident(poolt.shape),
            _resident((1, att_dim)),
            _resident((1, att_dim)),
        ],
        out_specs=[pl.BlockSpec((tm, att_dim), row)] * 3 + [pl.BlockSpec((tm, att_dim), tail)] * 2,
        out_shape=[jax.ShapeDtypeStruct((n, att_dim), BF16)] * 3
        + [jax.ShapeDtypeStruct((n // tail_div, att_dim), F32)] * 2,
        compiler_params=_params("arbitrary"),
    )(x, g, w, pool, poolt, gq, gk)


def _post_body(x_ref, a_ref, wo_ref, g_ref, win_ref, wout_ref, o_ref, *, d_ff, edges):
    x1 = x_ref[...] + _dot(a_ref[...], wo_ref[...])
    h = _rms_rows(x1, g_ref[...]).astype(BF16)
    acc = x1
    for lo, hi in zip(edges[:-1], edges[1:]):
        gate = _dot(h, win_ref[:, lo:hi])
        up = _dot(h, win_ref[:, d_ff + lo:d_ff + hi])
        act = (_silu(gate) * up).astype(BF16)
        acc = acc + _dot(act, wout_ref[lo:hi, :])
    o_ref[...] = acc


def _ffn_edges(d_ff, n_chunks):
    assert d_ff % MXU_DIM == 0
    tiles = d_ff // MXU_DIM
    return tuple(MXU_DIM * ((tiles * c + n_chunks - 1) // n_chunks) for c in range(n_chunks + 1))


def _post_mixer(x, a, wo, g, win, wout, tm, n_chunks):
    n, d = x.shape
    d_ff = wout.shape[0]
    row = lambda i: (i, 0)
    return pl.pallas_call(
        functools.partial(_post_body, d_ff=d_ff, edges=_ffn_edges(d_ff, n_chunks)),
        grid=(n // tm,),
        in_specs=[
            pl.BlockSpec((tm, d), row),
            pl.BlockSpec((tm, a.shape[1]), row),
            _resident(wo.shape),
            _resident((1, d)),
            _resident(win.shape),
            _resident(wout.shape),
        ],
        out_specs=pl.BlockSpec((tm, d), row),
        out_shape=jax.ShapeDtypeStruct((n, d), F32),
        compiler_params=_params("parallel"),
    )(x, a, wo, g, win, wout)


def _mixer_body(p_ref, cw_ref, w2_ref, gb_ref, on_ref, conv0_ref, st0_ref,
                y_ref, convo_ref, sto_ref, ubuf, st_sc, *, tb, cdim, dk, dv):
    hk = dk // GLA_HEADS
    hv = dv // GLA_HEADS
    o_q = 3 * cdim
    o_k = o_q + dk
    o_v = o_k + dk
    o_g = o_v + dv
    o_l = o_g + dv
    t = pl.program_id(1)

    @pl.when(t == 0)
    def _():
        ubuf[0:8, :] = jnp.zeros((8, cdim), F32)
        ubuf[6:8, :] = conv0_ref[...]
        st_sc[...] = st0_ref[...]

    u = p_ref[:, 0:cdim] * p_ref[:, 2 * cdim:3 * cdim]
    ubuf[8:8 + tb, :] = u
    y = (cw_ref[0:1, :] * ubuf[6:6 + tb, :] + cw_ref[1:2, :] * ubuf[7:7 + tb, :]
         + cw_ref[2:3, :] * u)
    y_ref[:, 0:cdim] = (p_ref[:, cdim:2 * cdim] * y).astype(y_ref.dtype)
    tail = ubuf[tb + 6:tb + 8, :]
    ubuf[6:8, :] = tail
    convo_ref[...] = tail

    L = CHUNK
    assert hk == L and L & (L - 1) == 0
    sh = L.bit_length() - 1
    nst = GLA_HEADS * L
    r_i = lax.broadcasted_iota(jnp.int32, (nst, dk), 0)
    c_i = lax.broadcasted_iota(jnp.int32, (nst, dk), 1)
    same_head = (r_i >> sh) == (c_i >> sh)
    a_r = lax.broadcasted_iota(jnp.int32, (nst, nst), 0)
    a_c = lax.broadcasted_iota(jnp.int32, (nst, nst), 1)
    att_keep = ((a_r >> sh) == (a_c >> sh)) & ((a_c & (L - 1)) <= (a_r & (L - 1)))
    tri = (lax.broadcasted_iota(jnp.int32, (L, L), 1)
           <= lax.broadcasted_iota(jnp.int32, (L, L), 0)).astype(BF16)
    w2 = w2_ref[...]
    gb = gb_ref[...]
    onorm = on_ref[...]

    st = st_sc[...]
    for c in range(tb // L):
        rows = slice(c * L, (c + 1) * L)
        q = p_ref[rows, o_q:o_k] * (hk ** -0.5)
        k = p_ref[rows, o_k:o_v]
        gl = p_ref[rows, o_l:o_l + LANES].astype(BF16)
        gk = _dot(gl, w2) + gb
        la = (jnp.minimum(gk, 0.0) - jnp.log1p(jnp.exp(-jnp.abs(gk)))) * (1.0 / GATE_NORM)
        l_hi, l_mid, l_lo = _split3(la)
        b = _dot(tri, l_hi) + _dot(tri, l_mid) + _dot(tri, l_lo)
        b_last = b[L - 1:L, :]
        q_e = q * jnp.exp(b)
        k_e = (k * jnp.exp(-b)).astype(BF16)
        k_dec = k * jnp.exp(b_last - b)

        q_st = jnp.where(same_head, jnp.concatenate([q_e] * GLA_HEADS, axis=0), 0.0).astype(BF16)
        k_tl = jnp.concatenate([k_e] * GLA_HEADS, axis=0)
        kd_bd = jnp.where(same_head, jnp.concatenate([k_dec] * GLA_HEADS, axis=0), 0.0).astype(BF16)
        v_st = jnp.concatenate(
            [p_ref[rows, o_v + h * hv:o_v + (h + 1) * hv] for h in range(GLA_HEADS)], axis=0
        ).astype(BF16)
        g_st = jnp.concatenate(
            [p_ref[rows, o_g + h * hv:o_g + (h + 1) * hv] for h in range(GLA_HEADS)], axis=0)

        att = jnp.where(att_keep, _dot_nt(q_st, k_tl), 0.0).astype(BF16)
        o = _dot(att, v_st) + _dot_nt(q_st, st.astype(BF16))
        st = st * jnp.exp(b_last) + _dot_tn(v_st, kd_bd)

        o = _rms_rows(o, onorm) * _silu(g_st)
        for h in range(GLA_HEADS):
            y_ref[rows, cdim + h * hv:cdim + (h + 1) * hv] = o[h * L:(h + 1) * L, :].astype(y_ref.dtype)
    st_sc[...] = st
    sto_ref[...] = st


def _mixer_ab(proj, conv_w, w2pad, gb, onorm, conv0, st0, tb):
    bsz, t, width = proj.shape
    cdim = conv_w.shape[1]
    dk = w2pad.shape[1]
    hv = onorm.shape[1]
    dv = hv * GLA_HEADS
    return pl.pallas_call(
        functools.partial(_mixer_body, tb=tb, cdim=cdim, dk=dk, dv=dv),
        grid=(bsz, t // tb),
        in_specs=[
            pl.BlockSpec((None, tb, width), lambda b, i: (b, i, 0)),
            _resident(conv_w.shape),
            _resident(w2pad.shape),
            _resident(gb.shape),
            _resident(onorm.shape),
            pl.BlockSpec((None, CONV_WIDTH - 1, cdim), lambda b, i: (b, 0, 0)),
            pl.BlockSpec((None, hv, dk), lambda b, i: (b, 0, 0)),
        ],
        out_specs=[
            pl.BlockSpec((None, tb, cdim + dv), lambda b, i: (b, i, 0)),
            pl.BlockSpec((None, CONV_WIDTH - 1, cdim), lambda b, i: (b, 0, 0)),
            pl.BlockSpec((None, hv, dk), lambda b, i: (b, 0, 0)),
        ],
        out_shape=[
            jax.ShapeDtypeStruct((bsz, t, cdim + dv), BF16),
            jax.ShapeDtypeStruct((bsz, CONV_WIDTH - 1, cdim), F32),
            jax.ShapeDtypeStruct((bsz, hv, dk), F32),
        ],
        scratch_shapes=[
            pltpu.VMEM((tb + 8, cdim), F32),
            pltpu.VMEM((hv, dk), F32),
        ],
        compiler_params=_params("parallel", "arbitrary"),
    )(proj, conv_w, w2pad, gb, onorm, conv0, st0)


def _attn_body(q_ref, ka_ref, kb_ref, va_ref, vb_ref, brow_ref, o_ref, kw, vw, tab_ref,
               *, tq, tqc, hist_is_padding):
    nwin = BAND_ROWS + tqc

    @pl.when((pl.program_id(0) == 0) & (pl.program_id(1) == 0))
    def _build_table():
        wrow = brow_ref.shape[1]
        sh = CHUNK.bit_length() - 1
        rowi = lax.broadcasted_iota(jnp.int32, (tqc, wrow), 0)
        q_chunk = lax.broadcasted_iota(jnp.int32, (tqc, nwin), 0) >> sh
        k_chunk = (lax.broadcasted_iota(jnp.int32, (tqc, nwin), 1) >> sh) - BAND_ROWS // CHUNK
        allowed = (k_chunk <= q_chunk) & (k_chunk >= q_chunk - BAND_ROWS // CHUNK)

        def one_head(h, carry):
            y = jnp.broadcast_to(brow_ref[pl.ds(h, 1), :], (tqc, wrow))
            for bit in range(tqc.bit_length() - 1):
                y = jnp.where(((rowi >> bit) & 1) == 1, pltpu.roll(y, 1 << bit, axis=1), y)
            t = jnp.where(allowed, y[:, LANES:LANES + nwin], NEG_INF)
            tab_ref[h >> 1, pl.ds(pl.multiple_of((h & 1) * tqc, tqc), tqc), :] = t
            return carry

        lax.fori_loop(0, 2 * tab_ref.shape[0], one_head, 0)

    kw[0:BAND_ROWS, :] = ka_ref[...]
    kw[BAND_ROWS:, :] = kb_ref[...]
    vw[0:BAND_ROWS, :] = va_ref[...]
    vw[BAND_ROWS:, :] = vb_ref[...]
    n_pairs = q_ref.shape[1] // LANES
    lo = lax.broadcasted_iota(jnp.int32, (tqc, LANES), 1) < ATT_HEAD_DIM
    keep_q = ((lax.broadcasted_iota(jnp.int32, (2 * tqc, LANES), 1) < ATT_HEAD_DIM)
              == (lax.broadcasted_iota(jnp.int32, (2 * tqc, LANES), 0) < tqc))

    def run(mask_padding):
        def sub_block(j, carry):
            r0 = pl.multiple_of(j * tqc, tqc)
            if mask_padding:
                col = lax.broadcasted_iota(jnp.int32, (2 * tqc, nwin), 1)
                valid = col >= (BAND_ROWS - r0)
            def scores(p):
                lanes = slice(p * LANES, (p + 1) * LANES)
                qp = q_ref[pl.ds(r0, tqc), lanes]
                kp = kw[pl.ds(r0, nwin), lanes]
                q2 = jnp.where(keep_q, jnp.concatenate([qp, qp], axis=0), 0.0)
                s = _dot_nt(q2, kp) + tab_ref[p]
                if mask_padding:
                    s = jnp.where(valid, s, NEG_INF)
                return s

            s_next = scores(0)
            for p in range(n_pairs):
                lanes = slice(p * LANES, (p + 1) * LANES)
                vp = vw[pl.ds(r0, nwin), lanes]
                s = s_next
                if p + 1 < n_pairs:
                    s_next = scores(p + 1)
                e = jnp.exp(s - jnp.max(s, axis=-1, keepdims=True))
                den = jnp.sum(e, axis=-1, keepdims=True)
                pv = _dot(e.astype(BF16), vp) / den
                o_ref[pl.ds(r0, tqc), lanes] = jnp.where(lo, pv[:tqc], pv[tqc:]).astype(o_ref.dtype)
            return carry

        lax.fori_loop(0, tq // tqc, sub_block, 0)

    if hist_is_padding:
        first = pl.program_id(1) == 0
        pl.when(first)(lambda: run(True))
        pl.when(jnp.logical_not(first))(lambda: run(False))
    else:
        run(False)


def _band_attention(q, k_hist, k_cur, v_hist, v_cur, brow, tq, tqc, hist_is_padding):
    bsz, t, width = q.shape
    assert t % tq == 0 and tq % tqc == 0
    if hist_is_padding:
        assert tq == BAND_ROWS
        k_hist, v_hist = k_cur, v_cur
        hist_map = lambda b, i: (b, jnp.maximum(i - 1, 0), 0)
    else:
        assert t == tq and k_hist.shape[1] == BAND_ROWS
        hist_map = lambda b, i: (b, 0, 0)
    cur_map = lambda b, i: (b, i, 0)
    return pl.pallas_call(
        functools.partial(_attn_body, tq=tq, tqc=tqc, hist_is_padding=hist_is_padding),
        grid=(bsz, t // tq),
        in_specs=[
            pl.BlockSpec((None, tq, width), cur_map),
            pl.BlockSpec((None, BAND_ROWS, width), hist_map),
            pl.BlockSpec((None, tq, width), cur_map),
            pl.BlockSpec((None, BAND_ROWS, width), hist_map),
            pl.BlockSpec((None, tq, width), cur_map),
            _resident(brow.shape),
        ],
        out_specs=pl.BlockSpec((None, tq, width), cur_map),
        out_shape=jax.ShapeDtypeStruct((bsz, t, width), BF16),
        scratch_shapes=[
            pltpu.VMEM((BAND_ROWS + tq, width), BF16),
            pltpu.VMEM((BAND_ROWS + tq, width), BF16),
            pltpu.VMEM((width // LANES, 2 * tqc, BAND_ROWS + tqc), F32),
        ],
        compiler_params=_params("arbitrary", "arbitrary"),
    )(q, k_hist, k_cur, v_hist, v_cur, brow)


def _bias_row(rel_bias, tqc):
    width = -(-(LANES + BAND_ROWS + tqc) // LANES) * LANES
    rel = np.clip(BAND_ROWS + LANES - np.arange(width), -(CHUNK - 1), MAX_REL) + (CHUNK - 1)
    return rel_bias[:, rel].astype(F32)


def _trunk(x, hist_is_padding, conv_prev, gla_prev, k_hist, v_hist, w):
    bsz, t, d = x.shape
    n = bsz * t
    depth = w["norm_mix"].shape[0]
    tm = min(512, n)
    tb = min(512, t)
    tq = min(BAND_ROWS, t)
    tqc = min(2 * CHUNK, t)
    if hist_is_padding:
        keep = BAND_ROWS
        assert t % tm == 0 and tm == keep
        tail_div = t // tm
    else:
        keep = k_hist.shape[2]
        assert n == tm and keep == BAND_ROWS and t <= keep
        tail_div = 1
    heads = w["rel_bias"].shape[1]
    xf = x.reshape(n, d)
    conv_new, gla_new, k_new, v_new = [], [], [], []
    for layer in range(depth):
        g_mix = w["norm_mix"][layer][None, :]
        g_ffn = w["norm_ffn"][layer][None, :]
        if layer % 2 == 0:
            e = layer // 2
            proj = _norm_proj(xf, g_mix, w["w_in_ab"][e], tm)
            st0 = gla_prev[e].reshape(bsz, -1, gla_prev.shape[-1]).swapaxes(1, 2)
            mix, cs, st = _mixer_ab(
                proj.reshape(bsz, t, -1), w["conv_w"][e], w["gk_w2"][e], w["gk_b"][e][None, :],
                w["gla_onorm"][e][None, :], conv_prev[e], st0, tb)
            conv_new.append(cs)
            gla_new.append(st.swapaxes(1, 2).reshape(gla_prev.shape[1:]))
            mix = mix.reshape(n, -1)
            w_o = w["w_out_ab"][e]
        else:
            o = layer // 2
            q, k, v, kt, vt = _norm_qkv(xf, g_mix, w["w_qkv"][o], w["pool"], w["poolt"],
                                        w["q_norm"][o], w["k_norm"][o], tm, tail_div)
            att_dim = k.shape[-1]
            shp = (bsz, t, att_dim)
            if hist_is_padding:
                kh = vh = None
            else:
                kh = k_hist[o].reshape(bsz, -1, att_dim).astype(BF16)
                vh = v_hist[o].reshape(bsz, -1, att_dim).astype(BF16)
            brow = _bias_row(w["rel_bias"][o], tqc)
            att = _band_attention(q.reshape(shp), kh, k.reshape(shp), vh, v.reshape(shp), brow,
                                  tq, tqc, hist_is_padding)
            if hist_is_padding:
                k_new.append(kt.reshape(bsz, keep, heads, -1))
                v_new.append(vt.reshape(bsz, keep, heads, -1))
            else:
                k_new.append(jnp.concatenate([k_hist[o][:, t:], kt.reshape(bsz, t, heads, -1)], axis=1))
                v_new.append(jnp.concatenate([v_hist[o][:, t:], vt.reshape(bsz, t, heads, -1)], axis=1))
            mix = att.reshape(n, -1)
            w_o = w["w_o_att"][o]
        xf = _post_mixer(xf, mix, w_o, g_ffn, w["w_ffn_in"][layer], w["w_ffn_out"][layer], tm, 2)
    return (xf.reshape(bsz, t, d), jnp.stack(conv_new), jnp.stack(gla_new),
            jnp.stack(k_new), jnp.stack(v_new))


def kernel(x_prompt, x_sample, state_conv, state_gla, cache_k, cache_v, norm_mix, norm_ffn,
           w_in_ab, conv_w, gla_gk_w2, gla_gk_b, gla_onorm, w_out_ab, w_qkv, q_norm, k_norm,
           rel_bias, w_o_att, w_ffn_in, w_ffn_out):
    bsz = x_prompt.shape[0]
    n_even = state_conv.shape[0]
    n_odd = cache_k.shape[0]
    heads, head_dim = cache_k.shape[-2:]
    att_dim = heads * head_dim
    in_ab = w_in_ab.shape[-1]
    in_pad = -in_ab % LANES
    rank = gla_gk_w2.shape[1]
    pool = (np.arange(att_dim)[:, None] // head_dim == np.arange(LANES)[None, :])
    w = {
        "norm_mix": norm_mix, "norm_ffn": norm_ffn,
        "w_in_ab": jnp.pad(w_in_ab, ((0, 0), (0, 0), (0, in_pad))).astype(BF16),
        "conv_w": conv_w,
        "gk_w2": jnp.pad(gla_gk_w2, ((0, 0), (0, LANES - rank), (0, 0))).astype(BF16),
        "gk_b": gla_gk_b, "gla_onorm": gla_onorm,
        "w_out_ab": w_out_ab.astype(BF16),
        "w_qkv": w_qkv.astype(BF16),
        "q_norm": jnp.tile(q_norm, (1, heads))[:, None, :],
        "k_norm": jnp.tile(k_norm, (1, heads))[:, None, :],
        "rel_bias": rel_bias,
        "w_o_att": w_o_att.astype(BF16),
        "w_ffn_in": w_ffn_in.astype(BF16), "w_ffn_out": w_ffn_out.astype(BF16),
        "pool": jnp.asarray(pool, BF16), "poolt": jnp.asarray(pool.T, BF16),
    }
    conv0 = jnp.zeros((n_even, bsz) + state_conv.shape[2:], x_prompt.dtype)
    gla0 = jnp.zeros((n_even, bsz) + state_gla.shape[2:], F32)
    kv0 = jnp.zeros((n_odd, bsz, BAND_ROWS, heads, head_dim), x_prompt.dtype)
    y_p, conv_p, gla_p, k_p, v_p = _trunk(x_prompt, True, conv0, gla0, kv0, kv0, w)
    y_s, conv_s, gla_s, k_s, v_s = _trunk(x_sample, False, state_conv, state_gla, cache_k, cache_v, w)
    return (y_p, y_s, conv_p, gla_p, k_p, v_p, conv_s, gla_s, k_s, v_s)
```

```python
import functools

import jax
import jax.numpy as jnp
import numpy as np
from jax import lax
from jax.experimental import pallas as pl
from jax.experimental.pallas import tpu as pltpu

F32 = jnp.float32
BF16 = jnp.bfloat16

EPS = 1e-6
NEG_INF = -1e30
CHUNK = 64
CONV_WIDTH = 3
GLA_HEADS = 4
GATE_NORM = 16.0
ATT_HEAD_DIM = 64
BAND_ROWS = 8 * CHUNK
MAX_REL = 256
LANES = 128
MXU_DIM = 256
VMEM_LIMIT = 56 * 1024 * 1024


def _params(*sem):
    return pltpu.CompilerParams(dimension_semantics=sem, vmem_limit_bytes=VMEM_LIMIT)


def _resident(shape):
    nd = len(shape)
    return pl.BlockSpec(shape, lambda *_: (0,) * nd, pipeline_mode=pl.Buffered(1))


def _dot(a, b):
    return jnp.dot(a, b, preferred_element_type=F32)


def _dot_nt(a, b):
    return lax.dot_general(a, b, (((1,), (1,)), ((), ())), preferred_element_type=F32)


def _dot_tn(a, b):
    return lax.dot_general(a, b, (((0,), (0,)), ((), ())), preferred_element_type=F32)


def _split2(a):
    hi = a.astype(BF16)
    lo = (a - hi.astype(F32)).astype(BF16)
    return hi, lo


def _split3(a):
    hi = a.astype(BF16)
    r = a - hi.astype(F32)
    mid = r.astype(BF16)
    lo = (r - mid.astype(F32)).astype(BF16)
    return hi, mid, lo


def _rms_rows(x, g):
    ms = jnp.mean(x * x, axis=-1, keepdims=True)
    return x * lax.rsqrt(ms + EPS) * g


def _silu(x):
    return x * jax.nn.sigmoid(x)


def _proj_body(x_ref, g_ref, w_ref, o_ref):
    h = _rms_rows(x_ref[...], g_ref[...]).astype(BF16)
    o_ref[...] = _dot(h, w_ref[...])


def _norm_proj(x, g, w, tm):
    n, d = x.shape
    nout = w.shape[1]
    return pl.pallas_call(
        _proj_body,
        grid=(n // tm,),
        in_specs=[
            pl.BlockSpec((tm, d), lambda i: (i, 0)),
            _resident((1, d)),
            _resident((d, nout)),
        ],
        out_specs=pl.BlockSpec((tm, nout), lambda i: (i, 0)),
        out_shape=jax.ShapeDtypeStruct((n, nout), F32),
        compiler_params=_params("parallel"),
    )(x, g, w)


def _qkv_body(x_ref, g_ref, w_ref, pool_ref, poolt_ref, gq_ref, gk_ref,
              q_ref, k_ref, v_ref, kt_ref, vt_ref, *, att_dim, q_scale):
    h = _rms_rows(x_ref[...], g_ref[...]).astype(BF16)
    qkv = _dot(h, w_ref[...])

    def head_norm(a, gt):
        ms = _dot((a * a).astype(BF16), pool_ref[...]) * (1.0 / ATT_HEAD_DIM)
        rhi, rlo = _split2(lax.rsqrt(ms + EPS))
        rb = _dot(rhi, poolt_ref[...]) + _dot(rlo, poolt_ref[...])
        return a * rb * gt

    q_ref[...] = (head_norm(qkv[:, :att_dim], gq_ref[...]) * q_scale).astype(BF16)
    kn = head_norm(qkv[:, att_dim:2 * att_dim], gk_ref[...])
    v = qkv[:, 2 * att_dim:]
    k_ref[...] = kn.astype(BF16)
    v_ref[...] = v.astype(BF16)
    kt_ref[...] = kn
    vt_ref[...] = v


def _norm_qkv(x, g, w, pool, poolt, gq, gk, tm, tail_div):
    n, d = x.shape
    att_dim = w.shape[1] // 3
    row = lambda i: (i, 0)
    tail = lambda i: (i // tail_div, 0)
    return pl.pallas_call(
        functools.partial(_qkv_body, att_dim=att_dim, q_scale=ATT_HEAD_DIM ** -0.5),
        grid=(n // tm,),
        in_specs=[
            pl.BlockSpec((tm, d), row),
            _resident((1, d)),
            _resident(w.shape),
            _resident(pool.shape),
            _resident(poolt.shape),
            _resident((1, att_dim)),
            _resident((1, att_dim)),
        ],
        out_specs=[pl.BlockSpec((tm, att_dim), row)] * 3 + [pl.BlockSpec((tm, att_dim), tail)] * 2,
        out_shape=[jax.ShapeDtypeStruct((n, att_dim), BF16)] * 3
        + [jax.ShapeDtypeStruct((n // tail_div, att_dim), F32)] * 2,
        compiler_params=_params("arbitrary"),
    )(x, g, w, pool, poolt, gq, gk)


def _post_body(x_ref, a_ref, wo_ref, g_ref, win_ref, wout_ref, o_ref, *, d_ff, edges):
    x1 = x_ref[...] + _dot(a_ref[...], wo_ref[...])
    h = _rms_rows(x1, g_ref[...]).astype(BF16)
    acc = x1
    for lo, hi in zip(edges[:-1], edges[1:]):
        gate = _dot(h, win_ref[:, lo:hi])
        up = _dot(h, win_ref[:, d_ff + lo:d_ff + hi])
        act = (_silu(gate) * up).astype(BF16)
        acc = acc + _dot(act, wout_ref[lo:hi, :])
    o_ref[...] = acc


def _ffn_edges(d_ff, n_chunks):
    assert d_ff % MXU_DIM == 0
    tiles = d_ff // MXU_DIM
    return tuple(MXU_DIM * ((tiles * c + n_chunks - 1) // n_chunks) for c in range(n_chunks + 1))


def _post_mixer(x, a, wo, g, win, wout, tm, n_chunks):
    n, d = x.shape
    d_ff = wout.shape[0]
    row = lambda i: (i, 0)
    return pl.pallas_call(
        functools.partial(_post_body, d_ff=d_ff, edges=_ffn_edges(d_ff, n_chunks)),
        grid=(n // tm,),
        in_specs=[
            pl.BlockSpec((tm, d), row),
            pl.BlockSpec((tm, a.shape[1]), row),
            _resident(wo.shape),
            _resident((1, d)),
            _resident(win.shape),
            _resident(wout.shape),
        ],
        out_specs=pl.BlockSpec((tm, d), row),
        out_shape=jax.ShapeDtypeStruct((n, d), F32),
        compiler_params=_params("parallel"),
    )(x, a, wo, g, win, wout)


def _mixer_body(p_ref, cw_ref, w2_ref, gb_ref, on_ref, conv0_ref, st0_ref,
                y_ref, convo_ref, sto_ref, ubuf, st_sc, *, tb, cdim, dk, dv):
    hk = dk // GLA_HEADS
    hv = dv // GLA_HEADS
    o_q = 3 * cdim
    o_k = o_q + dk
    o_v = o_k + dk
    o_g = o_v + dv
    o_l = o_g + dv
    t = pl.program_id(1)

    @pl.when(t == 0)
    def _():
        ubuf[0:8, :] = jnp.zeros((8, cdim), F32)
        ubuf[6:8, :] = conv0_ref[...]
        st_sc[...] = st0_ref[...]

    u = p_ref[:, 0:cdim] * p_ref[:, 2 * cdim:3 * cdim]
    ubuf[8:8 + tb, :] = u
    y = (cw_ref[0:1, :] * ubuf[6:6 + tb, :] + cw_ref[1:2, :] * ubuf[7:7 + tb, :]
         + cw_ref[2:3, :] * u)
    y_ref[:, 0:cdim] = (p_ref[:, cdim:2 * cdim] * y).astype(y_ref.dtype)
    tail = ubuf[tb + 6:tb + 8, :]
    ubuf[6:8, :] = tail
    convo_ref[...] = tail

    L = CHUNK
    assert hk == L and L & (L - 1) == 0
    sh = L.bit_length() - 1
    nst = GLA_HEADS * L
    r_i = lax.broadcasted_iota(jnp.int32, (nst, dk), 0)
    c_i = lax.broadcasted_iota(jnp.int32, (nst, dk), 1)
    same_head = (r_i >> sh) == (c_i >> sh)
    a_r = lax.broadcasted_iota(jnp.int32, (nst, nst), 0)
    a_c = lax.broadcasted_iota(jnp.int32, (nst, nst), 1)
    att_keep = ((a_r >> sh) == (a_c >> sh)) & ((a_c & (L - 1)) <= (a_r & (L - 1)))
    tri = (lax.broadcasted_iota(jnp.int32, (L, L), 1)
           <= lax.broadcasted_iota(jnp.int32, (L, L), 0)).astype(BF16)
    w2 = w2_ref[...]
    gb = gb_ref[...]
    onorm = on_ref[...]

    rows = [slice(c * L, (c + 1) * L) for c in range(tb // L)]

    def stack(a):
        return jnp.concatenate([a] * GLA_HEADS, axis=0)

    def heads_to_rows(o0, r):
        return jnp.concatenate([p_ref[r, o0 + h * hv:o0 + (h + 1) * hv] for h in range(GLA_HEADS)], axis=0)

    gk = _dot(p_ref[:, o_l:o_l + LANES].astype(BF16), w2) + gb
    la = (jnp.minimum(gk, 0.0) - jnp.log1p(jnp.exp(-jnp.abs(gk)))) * (1.0 / GATE_NORM)
    l_hi, l_mid, l_lo = _split3(la)
    b = [_dot(tri, l_hi[r]) + _dot(tri, l_mid[r]) + _dot(tri, l_lo[r]) for r in rows]
    b_last = [x[L - 1:L, :] for x in b]
    q_st, k_tl, kd_bd, v_st = [], [], [], []
    for c, r in enumerate(rows):
        q = p_ref[r, o_q:o_k] * (hk ** -0.5)
        k = p_ref[r, o_k:o_v]
        q_st.append(jnp.where(same_head, stack(q * jnp.exp(b[c])), 0.0).astype(BF16))
        k_tl.append(stack((k * jnp.exp(-b[c])).astype(BF16)))
        kd_bd.append(jnp.where(same_head, stack(k * jnp.exp(b_last[c] - b[c])), 0.0).astype(BF16))
        v_st.append(heads_to_rows(o_v, r).astype(BF16))
    att = [jnp.where(att_keep, _dot_nt(q_st[c], k_tl[c]), 0.0).astype(BF16)
           for c in range(len(rows))]
    upd = [_dot_tn(v_st[c], kd_bd[c]) for c in range(len(rows))]
    o = [_dot(att[c], v_st[c]) for c in range(len(rows))]
    st = st_sc[...]
    for c in range(len(rows)):
        o[c] = o[c] + _dot_nt(q_st[c], st.astype(BF16))
        st = st * jnp.exp(b_last[c]) + upd[c]
    for c, r in enumerate(rows):
        oc = _rms_rows(o[c], onorm) * _silu(heads_to_rows(o_g, r))
        for h in range(GLA_HEADS):
            y_ref[r, cdim + h * hv:cdim + (h + 1) * hv] = oc[h * L:(h + 1) * L, :].astype(y_ref.dtype)
    st_sc[...] = st
    sto_ref[...] = st


def _mixer_ab(proj, conv_w, w2pad, gb, onorm, conv0, st0, tb):
    bsz, t, width = proj.shape
    cdim = conv_w.shape[1]
    dk = w2pad.shape[1]
    hv = onorm.shape[1]
    dv = hv * GLA_HEADS
    return pl.pallas_call(
        functools.partial(_mixer_body, tb=tb, cdim=cdim, dk=dk, dv=dv),
        grid=(bsz, t // tb),
        in_specs=[
            pl.BlockSpec((None, tb, width), lambda b, i: (b, i, 0)),
            _resident(conv_w.shape),
            _resident(w2pad.shape),
            _resident(gb.shape),
            _resident(onorm.shape),
            pl.BlockSpec((None, CONV_WIDTH - 1, cdim), lambda b, i: (b, 0, 0)),
            pl.BlockSpec((None, hv, dk), lambda b, i: (b, 0, 0)),
        ],
        out_specs=[
            pl.BlockSpec((None, tb, cdim + dv), lambda b, i: (b, i, 0)),
            pl.BlockSpec((None, CONV_WIDTH - 1, cdim), lambda b, i: (b, 0, 0)),
            pl.BlockSpec((None, hv, dk), lambda b, i: (b, 0, 0)),
        ],
        out_shape=[
            jax.ShapeDtypeStruct((bsz, t, cdim + dv), BF16),
            jax.ShapeDtypeStruct((bsz, CONV_WIDTH - 1, cdim), F32),
            jax.ShapeDtypeStruct((bsz, hv, dk), F32),
        ],
        scratch_shapes=[
            pltpu.VMEM((tb + 8, cdim), F32),
            pltpu.VMEM((hv, dk), F32),
        ],
        compiler_params=_params("parallel", "arbitrary"),
    )(proj, conv_w, w2pad, gb, onorm, conv0, st0)


def _attn_body(q_ref, ka_ref, kb_ref, va_ref, vb_ref, brow_ref, o_ref, kw, vw, tab_ref,
               *, tq, tqc, hist_is_padding):
    nwin = BAND_ROWS + tqc

    @pl.when((pl.program_id(0) == 0) & (pl.program_id(1) == 0))
    def _build_table():
        wrow = brow_ref.shape[1]
        sh = CHUNK.bit_length() - 1
        rowi = lax.broadcasted_iota(jnp.int32, (tqc, wrow), 0)
        q_chunk = lax.broadcasted_iota(jnp.int32, (tqc, nwin), 0) >> sh
        k_chunk = (lax.broadcasted_iota(jnp.int32, (tqc, nwin), 1) >> sh) - BAND_ROWS // CHUNK
        allowed = (k_chunk <= q_chunk) & (k_chunk >= q_chunk - BAND_ROWS // CHUNK)

        def one_head(h, carry):
            y = jnp.broadcast_to(brow_ref[pl.ds(h, 1), :], (tqc, wrow))
            for bit in range(tqc.bit_length() - 1):
                y = jnp.where(((rowi >> bit) & 1) == 1, pltpu.roll(y, 1 << bit, axis=1), y)
            t = jnp.where(allowed, y[:, LANES:LANES + nwin], NEG_INF)
            tab_ref[h >> 1, pl.ds(pl.multiple_of((h & 1) * tqc, tqc), tqc), :] = t
            return carry

        lax.fori_loop(0, 2 * tab_ref.shape[0], one_head, 0)

    kw[0:BAND_ROWS, :] = ka_ref[...]
    kw[BAND_ROWS:, :] = kb_ref[...]
    vw[0:BAND_ROWS, :] = va_ref[...]
    vw[BAND_ROWS:, :] = vb_ref[...]
    n_pairs = q_ref.shape[1] // LANES
    lo = lax.broadcasted_iota(jnp.int32, (tqc, LANES), 1) < ATT_HEAD_DIM
    keep_q = ((lax.broadcasted_iota(jnp.int32, (2 * tqc, LANES), 1) < ATT_HEAD_DIM)
              == (lax.broadcasted_iota(jnp.int32, (2 * tqc, LANES), 0) < tqc))

    def run(mask_padding):
        def sub_block(j, carry):
            r0 = pl.multiple_of(j * tqc, tqc)
            if mask_padding:
                col = lax.broadcasted_iota(jnp.int32, (2 * tqc, nwin), 1)
                valid = col >= (BAND_ROWS - r0)
            def scores(p):
                lanes = slice(p * LANES, (p + 1) * LANES)
                qp = q_ref[pl.ds(r0, tqc), lanes]
                kp = kw[pl.ds(r0, nwin), lanes]
                q2 = jnp.where(keep_q, jnp.concatenate([qp, qp], axis=0), 0.0)
                s = _dot_nt(q2, kp) + tab_ref[p]
                if mask_padding:
                    s = jnp.where(valid, s, NEG_INF)
                return s

            def weighted_values(p, e, den):
                lanes = slice(p * LANES, (p + 1) * LANES)
                pv = _dot(e, vw[pl.ds(r0, nwin), lanes]) / den
                o_ref[pl.ds(r0, tqc), lanes] = jnp.where(lo, pv[:tqc], pv[tqc:]).astype(o_ref.dtype)

            s_next = scores(0)
            pending = None
            for p in range(n_pairs):
                s = s_next
                if p + 1 < n_pairs:
                    s_next = scores(p + 1)
                e = jnp.exp(s - jnp.max(s, axis=-1, keepdims=True))
                den = jnp.sum(e, axis=-1, keepdims=True)
                if pending is not None:
                    weighted_values(*pending)
                pending = (p, e.astype(BF16), den)
            weighted_values(*pending)
            return carry

        lax.fori_loop(0, tq // tqc, sub_block, 0)

    if hist_is_padding:
        first = pl.program_id(1) == 0
        pl.when(first)(lambda: run(True))
        pl.when(jnp.logical_not(first))(lambda: run(False))
    else:
        run(False)


def _band_attention(q, k_hist, k_cur, v_hist, v_cur, brow, tq, tqc, hist_is_padding):
    bsz, t, width = q.shape
    assert t % tq == 0 and tq % tqc == 0
    if hist_is_padding:
        assert tq == BAND_ROWS
        k_hist, v_hist = k_cur, v_cur
        hist_map = lambda b, i: (b, jnp.maximum(i - 1, 0), 0)
    else:
        assert t == tq and k_hist.shape[1] == BAND_ROWS
        hist_map = lambda b, i: (b, 0, 0)
    cur_map = lambda b, i: (b, i, 0)
    return pl.pallas_call(
        functools.partial(_attn_body, tq=tq, tqc=tqc, hist_is_padding=hist_is_padding),
        grid=(bsz, t // tq),
        in_specs=[
            pl.BlockSpec((None, tq, width), cur_map),
            pl.BlockSpec((None, BAND_ROWS, width), hist_map),
            pl.BlockSpec((None, tq, width), cur_map),
            pl.BlockSpec((None, BAND_ROWS, width), hist_map),
            pl.BlockSpec((None, tq, width), cur_map),
            _resident(brow.shape),
        ],
        out_specs=pl.BlockSpec((None, tq, width), cur_map),
        out_shape=jax.ShapeDtypeStruct((bsz, t, width), BF16),
        scratch_shapes=[
            pltpu.VMEM((BAND_ROWS + tq, width), BF16),
            pltpu.VMEM((BAND_ROWS + tq, width), BF16),
            pltpu.VMEM((width // LANES, 2 * tqc, BAND_ROWS + tqc), F32),
        ],
        compiler_params=_params("arbitrary", "arbitrary"),
    )(q, k_hist, k_cur, v_hist, v_cur, brow)


def _bias_row(rel_bias, tqc):
    width = -(-(LANES + BAND_ROWS + tqc) // LANES) * LANES
    rel = np.clip(BAND_ROWS + LANES - np.arange(width), -(CHUNK - 1), MAX_REL) + (CHUNK - 1)
    return rel_bias[:, rel].astype(F32)


def _trunk(x, hist_is_padding, conv_prev, gla_prev, k_hist, v_hist, w):
    bsz, t, d = x.shape
    n = bsz * t
    depth = w["norm_mix"].shape[0]
    tm = min(512, n)
    tb = min(512, t)
    tq = min(BAND_ROWS, t)
    tqc = min(2 * CHUNK, t)
    if hist_is_padding:
        keep = BAND_ROWS
        assert t % tm == 0 and tm == keep
        tail_div = t // tm
    else:
        keep = k_hist.shape[2]
        assert n == tm and keep == BAND_ROWS and t <= keep
        tail_div = 1
    heads = w["rel_bias"].shape[1]
    xf = x.reshape(n, d)
    conv_new, gla_new, k_new, v_new = [], [], [], []
    for layer in range(depth):
        g_mix = w["norm_mix"][layer][None, :]
        g_ffn = w["norm_ffn"][layer][None, :]
        if layer % 2 == 0:
            e = layer // 2
            proj = _norm_proj(xf, g_mix, w["w_in_ab"][e], tm)
            st0 = gla_prev[e].reshape(bsz, -1, gla_prev.shape[-1]).swapaxes(1, 2)
            mix, cs, st = _mixer_ab(
                proj.reshape(bsz, t, -1), w["conv_w"][e], w["gk_w2"][e], w["gk_b"][e][None, :],
                w["gla_onorm"][e][None, :], conv_prev[e], st0, tb)
            conv_new.append(cs)
            gla_new.append(st.swapaxes(1, 2).reshape(gla_prev.shape[1:]))
            mix = mix.reshape(n, -1)
            w_o = w["w_out_ab"][e]
        else:
            o = layer // 2
            q, k, v, kt, vt = _norm_qkv(xf, g_mix, w["w_qkv"][o], w["pool"], w["poolt"],
                                        w["q_norm"][o], w["k_norm"][o], tm, tail_div)
            att_dim = k.shape[-1]
            shp = (bsz, t, att_dim)
            if hist_is_padding:
                kh = vh = None
            else:
                kh = k_hist[o].reshape(bsz, -1, att_dim).astype(BF16)
                vh = v_hist[o].reshape(bsz, -1, att_dim).astype(BF16)
            brow = _bias_row(w["rel_bias"][o], tqc)
            att = _band_attention(q.reshape(shp), kh, k.reshape(shp), vh, v.reshape(shp), brow,
                                  tq, tqc, hist_is_padding)
            k_new.append(kt)
            v_new.append(vt)
            mix = att.reshape(n, -1)
            w_o = w["w_o_att"][o]
        xf = _post_mixer(xf, mix, w_o, g_ffn, w["w_ffn_in"][layer], w["w_ffn_out"][layer], tm, 2)

    def cache(tails, hist):
        new = jnp.stack(tails).reshape(len(tails), bsz, -1, heads, ATT_HEAD_DIM)
        if hist_is_padding:
            return new
        return jnp.concatenate([hist[:, :, t:], new], axis=2)

    return (xf.reshape(bsz, t, d), jnp.stack(conv_new), jnp.stack(gla_new),
            cache(k_new, k_hist), cache(v_new, v_hist))


def kernel(x_prompt, x_sample, state_conv, state_gla, cache_k, cache_v, norm_mix, norm_ffn,
           w_in_ab, conv_w, gla_gk_w2, gla_gk_b, gla_onorm, w_out_ab, w_qkv, q_norm, k_norm,
           rel_bias, w_o_att, w_ffn_in, w_ffn_out):
    bsz = x_prompt.shape[0]
    n_even = state_conv.shape[0]
    n_odd = cache_k.shape[0]
    heads, head_dim = cache_k.shape[-2:]
    att_dim = heads * head_dim
    in_ab = w_in_ab.shape[-1]
    in_pad = -in_ab % LANES
    rank = gla_gk_w2.shape[1]
    pool = (np.arange(att_dim)[:, None] // head_dim == np.arange(LANES)[None, :])
    def per_layer(a, fn=lambda m: m):
        return [fn(a[i]).astype(BF16) for i in range(a.shape[0])]

    w = {
        "norm_mix": norm_mix, "norm_ffn": norm_ffn,
        "w_in_ab": per_layer(w_in_ab, lambda m: jnp.pad(m, ((0, 0), (0, in_pad)))),
        "conv_w": conv_w,
        "gk_w2": per_layer(gla_gk_w2, lambda m: jnp.pad(m, ((0, LANES - rank), (0, 0)))),
        "gk_b": gla_gk_b, "gla_onorm": gla_onorm,
        "w_out_ab": per_layer(w_out_ab),
        "w_qkv": per_layer(w_qkv),
        "q_norm": jnp.tile(q_norm, (1, heads))[:, None, :],
        "k_norm": jnp.tile(k_norm, (1, heads))[:, None, :],
        "rel_bias": rel_bias,
        "w_o_att": per_layer(w_o_att),
        "w_ffn_in": per_layer(w_ffn_in), "w_ffn_out": per_layer(w_ffn_out),
        "pool": jnp.asarray(pool, BF16), "poolt": jnp.asarray(pool.T, BF16),
    }
    conv0 = jnp.zeros((n_even, bsz) + state_conv.shape[2:], x_prompt.dtype)
    gla0 = jnp.zeros((n_even, bsz) + state_gla.shape[2:], F32)
    y_p, conv_p, gla_p, k_p, v_p = _trunk(x_prompt, True, conv0, gla0, None, None, w)
    y_s, conv_s, gla_s, k_s, v_s = _trunk(x_sample, False, state_conv, state_gla, cache_k, cache_v, w)
    return (y_p, y_s, conv_p, gla_p, k_p, v_p, conv_s, gla_s, k_s, v_s)
```

```python
import functools

import jax
import jax.numpy as jnp
import numpy as np
from jax import lax
from jax.experimental import pallas as pl
from jax.experimental.pallas import tpu as pltpu

F32 = jnp.float32
BF16 = jnp.bfloat16

EPS = 1e-6
NEG_INF = -1e30
CHUNK = 64
CONV_WIDTH = 3
GLA_HEADS = 4
GATE_NORM = 16.0
ATT_HEAD_DIM = 64
BAND_ROWS = 8 * CHUNK
MAX_REL = 256
LANES = 128
MXU_DIM = 256
VMEM_LIMIT = 56 * 1024 * 1024


def _params(*sem):
    return pltpu.CompilerParams(dimension_semantics=sem, vmem_limit_bytes=VMEM_LIMIT)


def _resident(shape):
    nd = len(shape)
    return pl.BlockSpec(shape, lambda *_: (0,) * nd, pipeline_mode=pl.Buffered(1))


def _resident_layer(stacked, layer):
    nd = stacked.ndim - 1
    return pl.BlockSpec((None,) + stacked.shape[1:], lambda *_: (layer,) + (0,) * nd,
                        pipeline_mode=pl.Buffered(1))


def _dot(a, b):
    return jnp.dot(a, b, preferred_element_type=F32)


def _dot_nt(a, b):
    return lax.dot_general(a, b, (((1,), (1,)), ((), ())), preferred_element_type=F32)


def _dot_tn(a, b):
    return lax.dot_general(a, b, (((0,), (0,)), ((), ())), preferred_element_type=F32)


def _split2(a):
    hi = a.astype(BF16)
    lo = (a - hi.astype(F32)).astype(BF16)
    return hi, lo


def _split3(a):
    hi = a.astype(BF16)
    r = a - hi.astype(F32)
    mid = r.astype(BF16)
    lo = (r - mid.astype(F32)).astype(BF16)
    return hi, mid, lo


def _rms_rows(x, g):
    ms = jnp.mean(x * x, axis=-1, keepdims=True)
    return x * lax.rsqrt(ms + EPS) * g


def _silu(x):
    return x * jax.nn.sigmoid(x)


def _proj_body(x_ref, g_ref, w_ref, o_ref):
    h = _rms_rows(x_ref[...], g_ref[...]).astype(BF16)
    o_ref[...] = _dot(h, w_ref[...])


def _norm_proj(x, g, w, layer, tm):
    n, d = x.shape
    nout = w.shape[2]
    return pl.pallas_call(
        _proj_body,
        grid=(n // tm,),
        in_specs=[
            pl.BlockSpec((tm, d), lambda i: (i, 0)),
            _resident((1, d)),
            _resident_layer(w, layer),
        ],
        out_specs=pl.BlockSpec((tm, nout), lambda i: (i, 0)),
        out_shape=jax.ShapeDtypeStruct((n, nout), F32),
        compiler_params=_params("parallel"),
    )(x, g, w)


def _qkv_body(x_ref, g_ref, w_ref, pool_ref, poolt_ref, gq_ref, gk_ref,
              q_ref, k_ref, v_ref, kt_ref, vt_ref, *, att_dim, q_scale):
    h = _rms_rows(x_ref[...], g_ref[...]).astype(BF16)
    qkv = _dot(h, w_ref[...])

    def head_norm(a, gt):
        ms = _dot((a * a).astype(BF16), pool_ref[...]) * (1.0 / ATT_HEAD_DIM)
        rhi, rlo = _split2(lax.rsqrt(ms + EPS))
        rb = _dot(rhi, poolt_ref[...]) + _dot(rlo, poolt_ref[...])
        return a * rb * gt

    q_ref[...] = (head_norm(qkv[:, :att_dim], gq_ref[...]) * q_scale).astype(BF16)
    kn = head_norm(qkv[:, att_dim:2 * att_dim], gk_ref[...])
    v = qkv[:, 2 * att_dim:]
    k_ref[...] = kn.astype(BF16)
    v_ref[...] = v.astype(BF16)
    kt_ref[...] = kn
    vt_ref[...] = v


def _norm_qkv(x, g, w, layer, pool, poolt, gq, gk, tm, tail_div):
    n, d = x.shape
    att_dim = w.shape[2] // 3
    row = lambda i: (i, 0)
    tail = lambda i: (i // tail_div, 0)
    return pl.pallas_call(
        functools.partial(_qkv_body, att_dim=att_dim, q_scale=ATT_HEAD_DIM ** -0.5),
        grid=(n // tm,),
        in_specs=[
            pl.BlockSpec((tm, d), row),
            _resident((1, d)),
            _resident_layer(w, layer),
            _resident(pool.shape),
            _resident(poolt.shape),
            _resident((1, att_dim)),
            _resident((1, att_dim)),
        ],
        out_specs=[pl.BlockSpec((tm, att_dim), row)] * 3 + [pl.BlockSpec((tm, att_dim), tail)] * 2,
        out_shape=[jax.ShapeDtypeStruct((n, att_dim), BF16)] * 3
        + [jax.ShapeDtypeStruct((n // tail_div, att_dim), F32)] * 2,
        compiler_params=_params("arbitrary"),
    )(x, g, w, pool, poolt, gq, gk)


def _post_body(x_ref, a_ref, wo_ref, g_ref, win_ref, wout_ref, o_ref, *, d_ff, edges):
    x1 = x_ref[...] + _dot(a_ref[...], wo_ref[...])
    h = _rms_rows(x1, g_ref[...]).astype(BF16)
    acc = x1
    for lo, hi in zip(edges[:-1], edges[1:]):
        gate = _dot(h, win_ref[:, lo:hi])
        up = _dot(h, win_ref[:, d_ff + lo:d_ff + hi])
        act = (_silu(gate) * up).astype(BF16)
        acc = acc + _dot(act, wout_ref[lo:hi, :])
    o_ref[...] = acc


def _ffn_edges(d_ff, n_chunks):
    assert d_ff % MXU_DIM == 0
    tiles = d_ff // MXU_DIM
    return tuple(MXU_DIM * ((tiles * c + n_chunks - 1) // n_chunks) for c in range(n_chunks + 1))


def _post_mixer(x, a, wo, wo_layer, g, win, wout, layer, tm, n_chunks):
    n, d = x.shape
    d_ff = wout.shape[1]
    row = lambda i: (i, 0)
    return pl.pallas_call(
        functools.partial(_post_body, d_ff=d_ff, edges=_ffn_edges(d_ff, n_chunks)),
        grid=(n // tm,),
        in_specs=[
            pl.BlockSpec((tm, d), row),
            pl.BlockSpec((tm, a.shape[1]), row),
            _resident_layer(wo, wo_layer),
            _resident((1, d)),
            _resident_layer(win, layer),
            _resident_layer(wout, layer),
        ],
        out_specs=pl.BlockSpec((tm, d), row),
        out_shape=jax.ShapeDtypeStruct((n, d), F32),
        compiler_params=_params("parallel"),
    )(x, a, wo, g, win, wout)


def _mixer_body(p_ref, cw_ref, w2_ref, gb_ref, on_ref, conv0_ref, st0_ref,
                y_ref, convo_ref, sto_ref, ubuf, st_sc, *, tb, cdim, dk, dv):
    hk = dk // GLA_HEADS
    hv = dv // GLA_HEADS
    o_q = 3 * cdim
    o_k = o_q + dk
    o_v = o_k + dk
    o_g = o_v + dv
    o_l = o_g + dv
    t = pl.program_id(1)

    @pl.when(t == 0)
    def _():
        ubuf[0:8, :] = jnp.zeros((8, cdim), F32)
        ubuf[6:8, :] = conv0_ref[...]
        st_sc[...] = st0_ref[...]

    u = p_ref[:, 0:cdim] * p_ref[:, 2 * cdim:3 * cdim]
    ubuf[8:8 + tb, :] = u
    y = (cw_ref[0:1, :] * ubuf[6:6 + tb, :] + cw_ref[1:2, :] * ubuf[7:7 + tb, :]
         + cw_ref[2:3, :] * u)
    y_ref[:, 0:cdim] = (p_ref[:, cdim:2 * cdim] * y).astype(y_ref.dtype)
    tail = ubuf[tb + 6:tb + 8, :]
    ubuf[6:8, :] = tail
    convo_ref[...] = tail

    L = CHUNK
    assert hk == L and L & (L - 1) == 0
    sh = L.bit_length() - 1
    nst = GLA_HEADS * L
    r_i = lax.broadcasted_iota(jnp.int32, (nst, dk), 0)
    c_i = lax.broadcasted_iota(jnp.int32, (nst, dk), 1)
    same_head = (r_i >> sh) == (c_i >> sh)
    a_r = lax.broadcasted_iota(jnp.int32, (nst, nst), 0)
    a_c = lax.broadcasted_iota(jnp.int32, (nst, nst), 1)
    att_keep = ((a_r >> sh) == (a_c >> sh)) & ((a_c & (L - 1)) <= (a_r & (L - 1)))
    tri = (lax.broadcasted_iota(jnp.int32, (L, L), 1)
           <= lax.broadcasted_iota(jnp.int32, (L, L), 0)).astype(BF16)
    w2 = w2_ref[...]
    gb = gb_ref[...]
    onorm = on_ref[...]

    rows = [slice(c * L, (c + 1) * L) for c in range(tb // L)]

    def stack(a):
        return jnp.concatenate([a] * GLA_HEADS, axis=0)

    def heads_to_rows(o0, r):
        return jnp.concatenate([p_ref[r, o0 + h * hv:o0 + (h + 1) * hv] for h in range(GLA_HEADS)], axis=0)

    gk = _dot(p_ref[:, o_l:o_l + LANES].astype(BF16), w2) + gb
    la = (jnp.minimum(gk, 0.0) - jnp.log1p(jnp.exp(-jnp.abs(gk)))) * (1.0 / GATE_NORM)
    l_hi, l_mid, l_lo = _split3(la)
    b = [_dot(tri, l_hi[r]) + _dot(tri, l_mid[r]) + _dot(tri, l_lo[r]) for r in rows]
    b_last = [x[L - 1:L, :] for x in b]
    q_st, k_tl, kd_bd, v_st = [], [], [], []
    for c, r in enumerate(rows):
        q = p_ref[r, o_q:o_k] * (hk ** -0.5)
        k = p_ref[r, o_k:o_v]
        q_st.append(jnp.where(same_head, stack(q * jnp.exp(b[c])), 0.0).astype(BF16))
        k_tl.append(stack((k * jnp.exp(-b[c])).astype(BF16)))
        kd_bd.append(jnp.where(same_head, stack(k * jnp.exp(b_last[c] - b[c])), 0.0).astype(BF16))
        v_st.append(heads_to_rows(o_v, r).astype(BF16))
    att = [jnp.where(att_keep, _dot_nt(q_st[c], k_tl[c]), 0.0).astype(BF16)
           for c in range(len(rows))]
    upd = [_dot_tn(v_st[c], kd_bd[c]) for c in range(len(rows))]
    o = [_dot(att[c], v_st[c]) for c in range(len(rows))]
    st = st_sc[...]
    for c in range(len(rows)):
        o[c] = o[c] + _dot_nt(q_st[c], st.astype(BF16))
        st = st * jnp.exp(b_last[c]) + upd[c]
    for c, r in enumerate(rows):
        oc = _rms_rows(o[c], onorm) * _silu(heads_to_rows(o_g, r))
        for h in range(GLA_HEADS):
            y_ref[r, cdim + h * hv:cdim + (h + 1) * hv] = oc[h * L:(h + 1) * L, :].astype(y_ref.dtype)
    st_sc[...] = st
    sto_ref[...] = st


def _mixer_ab(proj, conv_w, w2pad, gb, onorm, conv0, st0, tb):
    bsz, t, width = proj.shape
    cdim = conv_w.shape[1]
    dk = w2pad.shape[1]
    hv = onorm.shape[1]
    dv = hv * GLA_HEADS
    return pl.pallas_call(
        functools.partial(_mixer_body, tb=tb, cdim=cdim, dk=dk, dv=dv),
        grid=(bsz, t // tb),
        in_specs=[
            pl.BlockSpec((None, tb, width), lambda b, i: (b, i, 0)),
            _resident(conv_w.shape),
            _resident(w2pad.shape),
            _resident(gb.shape),
            _resident(onorm.shape),
            pl.BlockSpec((None, CONV_WIDTH - 1, cdim), lambda b, i: (b, 0, 0)),
            pl.BlockSpec((None, hv, dk), lambda b, i: (b, 0, 0)),
        ],
        out_specs=[
            pl.BlockSpec((None, tb, cdim + dv), lambda b, i: (b, i, 0)),
            pl.BlockSpec((None, CONV_WIDTH - 1, cdim), lambda b, i: (b, 0, 0)),
            pl.BlockSpec((None, hv, dk), lambda b, i: (b, 0, 0)),
        ],
        out_shape=[
            jax.ShapeDtypeStruct((bsz, t, cdim + dv), BF16),
            jax.ShapeDtypeStruct((bsz, CONV_WIDTH - 1, cdim), F32),
            jax.ShapeDtypeStruct((bsz, hv, dk), F32),
        ],
        scratch_shapes=[
            pltpu.VMEM((tb + 8, cdim), F32),
            pltpu.VMEM((hv, dk), F32),
        ],
        compiler_params=_params("parallel", "arbitrary"),
    )(proj, conv_w, w2pad, gb, onorm, conv0, st0)


def _attn_body(q_ref, ka_ref, kb_ref, va_ref, vb_ref, brow_ref, o_ref, kw, vw, tab_ref,
               *, tq, tqc, hist_is_padding):
    nwin = BAND_ROWS + tqc

    @pl.when((pl.program_id(0) == 0) & (pl.program_id(1) == 0))
    def _build_table():
        wrow = brow_ref.shape[1]
        sh = CHUNK.bit_length() - 1
        rowi = lax.broadcasted_iota(jnp.int32, (tqc, wrow), 0)
        q_chunk = lax.broadcasted_iota(jnp.int32, (tqc, nwin), 0) >> sh
        k_chunk = (lax.broadcasted_iota(jnp.int32, (tqc, nwin), 1) >> sh) - BAND_ROWS // CHUNK
        allowed = (k_chunk <= q_chunk) & (k_chunk >= q_chunk - BAND_ROWS // CHUNK)

        def one_head(h, carry):
            y = jnp.broadcast_to(brow_ref[pl.ds(h, 1), :], (tqc, wrow))
            for bit in range(tqc.bit_length() - 1):
                y = jnp.where(((rowi >> bit) & 1) == 1, pltpu.roll(y, 1 << bit, axis=1), y)
            t = jnp.where(allowed, y[:, LANES:LANES + nwin], NEG_INF)
            tab_ref[h >> 1, pl.ds(pl.multiple_of((h & 1) * tqc, tqc), tqc), :] = t
            return carry

        lax.fori_loop(0, 2 * tab_ref.shape[0], one_head, 0)

    kw[0:BAND_ROWS, :] = ka_ref[...]
    kw[BAND_ROWS:, :] = kb_ref[...]
    vw[0:BAND_ROWS, :] = va_ref[...]
    vw[BAND_ROWS:, :] = vb_ref[...]
    n_pairs = q_ref.shape[1] // LANES
    lo = lax.broadcasted_iota(jnp.int32, (tqc, LANES), 1) < ATT_HEAD_DIM
    keep_q = ((lax.broadcasted_iota(jnp.int32, (2 * tqc, LANES), 1) < ATT_HEAD_DIM)
              == (lax.broadcasted_iota(jnp.int32, (2 * tqc, LANES), 0) < tqc))
    ones = jnp.ones((nwin, LANES), BF16)

    def run(mask_padding):
        def sub_block(j, carry):
            r0 = pl.multiple_of(j * tqc, tqc)
            if mask_padding:
                col = lax.broadcasted_iota(jnp.int32, (2 * tqc, nwin), 1)
                valid = col >= (BAND_ROWS - r0)
            def scores(p):
                lanes = slice(p * LANES, (p + 1) * LANES)
                qp = q_ref[pl.ds(r0, tqc), lanes]
                kp = kw[pl.ds(r0, nwin), lanes]
                q2 = jnp.where(keep_q, jnp.concatenate([qp, qp], axis=0), 0.0)
                s = _dot_nt(q2, kp) + tab_ref[p]
                if mask_padding:
                    s = jnp.where(valid, s, NEG_INF)
                return s

            def weighted_values(p, e):
                lanes = slice(p * LANES, (p + 1) * LANES)
                pv = _dot(e, jnp.concatenate([vw[pl.ds(r0, nwin), lanes], ones], axis=1))
                pv = pv[:, :LANES] / pv[:, LANES:]
                o_ref[pl.ds(r0, tqc), lanes] = jnp.where(lo, pv[:tqc], pv[tqc:]).astype(o_ref.dtype)

            s_next = scores(0)
            pending = None
            for p in range(n_pairs):
                s = s_next
                if p + 1 < n_pairs:
                    s_next = scores(p + 1)
                e = jnp.exp(s - jnp.max(s, axis=-1, keepdims=True))
                if pending is not None:
                    weighted_values(*pending)
                pending = (p, e.astype(BF16))
            weighted_values(*pending)
            return carry

        lax.fori_loop(0, tq // tqc, sub_block, 0, unroll=True)

    if hist_is_padding:
        first = pl.program_id(1) == 0
        pl.when(first)(lambda: run(True))
        pl.when(jnp.logical_not(first))(lambda: run(False))
    else:
        run(False)


def _band_attention(q, k_hist, k_cur, v_hist, v_cur, brow, tq, tqc, hist_is_padding):
    bsz, t, width = q.shape
    assert t % tq == 0 and tq % tqc == 0
    if hist_is_padding:
        assert tq == BAND_ROWS
        k_hist, v_hist = k_cur, v_cur
        hist_map = lambda b, i: (b, jnp.maximum(i - 1, 0), 0)
    else:
        assert t == tq and k_hist.shape[1] == BAND_ROWS
        hist_map = lambda b, i: (b, 0, 0)
    cur_map = lambda b, i: (b, i, 0)
    return pl.pallas_call(
        functools.partial(_attn_body, tq=tq, tqc=tqc, hist_is_padding=hist_is_padding),
        grid=(bsz, t // tq),
        in_specs=[
            pl.BlockSpec((None, tq, width), cur_map),
            pl.BlockSpec((None, BAND_ROWS, width), hist_map),
            pl.BlockSpec((None, tq, width), cur_map),
            pl.BlockSpec((None, BAND_ROWS, width), hist_map),
            pl.BlockSpec((None, tq, width), cur_map),
            _resident(brow.shape),
        ],
        out_specs=pl.BlockSpec((None, tq, width), cur_map),
        out_shape=jax.ShapeDtypeStruct((bsz, t, width), BF16),
        scratch_shapes=[
            pltpu.VMEM((BAND_ROWS + tq, width), BF16),
            pltpu.VMEM((BAND_ROWS + tq, width), BF16),
            pltpu.VMEM((width // LANES, 2 * tqc, BAND_ROWS + tqc), F32),
        ],
        compiler_params=_params("arbitrary", "arbitrary"),
    )(q, k_hist, k_cur, v_hist, v_cur, brow)


def _bias_row(rel_bias, tqc):
    width = -(-(LANES + BAND_ROWS + tqc) // LANES) * LANES
    rel = np.clip(BAND_ROWS + LANES - np.arange(width), -(CHUNK - 1), MAX_REL) + (CHUNK - 1)
    return rel_bias[:, rel].astype(F32)


def _trunk(x, hist_is_padding, conv_prev, gla_prev, k_hist, v_hist, w):
    bsz, t, d = x.shape
    n = bsz * t
    depth = w["norm_mix"].shape[0]
    tm = min(512, n)
    tb = min(512, t)
    tq = min(BAND_ROWS, t)
    tqc = min(2 * CHUNK, t)
    if hist_is_padding:
        keep = BAND_ROWS
        assert t % tm == 0 and tm == keep
        tail_div = t // tm
    else:
        keep = k_hist.shape[2]
        assert n == tm and keep == BAND_ROWS and t <= keep
        tail_div = 1
    heads = w["rel_bias"].shape[1]
    xf = x.reshape(n, d)
    conv_new, gla_new, k_new, v_new = [], [], [], []
    for layer in range(depth):
        g_mix = w["norm_mix"][layer][None, :]
        g_ffn = w["norm_ffn"][layer][None, :]
        if layer % 2 == 0:
            e = layer // 2
            proj = _norm_proj(xf, g_mix, w["w_in_ab"], e, tm)
            st0 = gla_prev[e].reshape(bsz, -1, gla_prev.shape[-1]).swapaxes(1, 2)
            mix, cs, st = _mixer_ab(
                proj.reshape(bsz, t, -1), w["conv_w"][e], w["gk_w2"][e], w["gk_b"][e][None, :],
                w["gla_onorm"][e][None, :], conv_prev[e], st0, tb)
            conv_new.append(cs)
            gla_new.append(st.swapaxes(1, 2).reshape(gla_prev.shape[1:]))
            mix = mix.reshape(n, -1)
            w_o, w_o_layer = w["w_out_ab"], e
        else:
            o = layer // 2
            q, k, v, kt, vt = _norm_qkv(xf, g_mix, w["w_qkv"], o, w["pool"], w["poolt"],
                                        w["q_norm"][o], w["k_norm"][o], tm, tail_div)
            att_dim = k.shape[-1]
            shp = (bsz, t, att_dim)
            if hist_is_padding:
                kh = vh = None
            else:
                kh = k_hist[o].reshape(bsz, -1, att_dim).astype(BF16)
                vh = v_hist[o].reshape(bsz, -1, att_dim).astype(BF16)
            brow = _bias_row(w["rel_bias"][o], tqc)
            att = _band_attention(q.reshape(shp), kh, k.reshape(shp), vh, v.reshape(shp), brow,
                                  tq, tqc, hist_is_padding)
            k_new.append(kt)
            v_new.append(vt)
            mix = att.reshape(n, -1)
            w_o, w_o_layer = w["w_o_att"], o
        xf = _post_mixer(xf, mix, w_o, w_o_layer, g_ffn, w["w_ffn_in"], w["w_ffn_out"], layer, tm, 2)

    def cache(tails, hist):
        new = jnp.stack(tails).reshape(len(tails), bsz, -1, heads, ATT_HEAD_DIM)
        if hist_is_padding:
            return new
        return jnp.concatenate([hist[:, :, t:], new], axis=2)

    return (xf.reshape(bsz, t, d), jnp.stack(conv_new), jnp.stack(gla_new),
            cache(k_new, k_hist), cache(v_new, v_hist))


def kernel(x_prompt, x_sample, state_conv, state_gla, cache_k, cache_v, norm_mix, norm_ffn,
           w_in_ab, conv_w, gla_gk_w2, gla_gk_b, gla_onorm, w_out_ab, w_qkv, q_norm, k_norm,
           rel_bias, w_o_att, w_ffn_in, w_ffn_out):
    bsz = x_prompt.shape[0]
    n_even = state_conv.shape[0]
    n_odd = cache_k.shape[0]
    heads, head_dim = cache_k.shape[-2:]
    att_dim = heads * head_dim
    in_ab = w_in_ab.shape[-1]
    in_pad = -in_ab % LANES
    rank = gla_gk_w2.shape[1]
    pool = (np.arange(att_dim)[:, None] // head_dim == np.arange(LANES)[None, :])
    w = {
        "norm_mix": norm_mix, "norm_ffn": norm_ffn,
        "w_in_ab": jnp.pad(w_in_ab, ((0, 0), (0, 0), (0, in_pad))).astype(BF16),
        "conv_w": conv_w,
        "gk_w2": jnp.pad(gla_gk_w2, ((0, 0), (0, LANES - rank), (0, 0))).astype(BF16),
        "gk_b": gla_gk_b, "gla_onorm": gla_onorm,
        "w_out_ab": w_out_ab.astype(BF16),
        "w_qkv": w_qkv.astype(BF16),
        "q_norm": jnp.tile(q_norm, (1, heads))[:, None, :],
        "k_norm": jnp.tile(k_norm, (1, heads))[:, None, :],
        "rel_bias": rel_bias,
        "w_o_att": w_o_att.astype(BF16),
        "w_ffn_in": w_ffn_in.astype(BF16), "w_ffn_out": w_ffn_out.astype(BF16),
        "pool": jnp.asarray(pool, BF16), "poolt": jnp.asarray(pool.T, BF16),
    }
    conv0 = jnp.zeros((n_even, bsz) + state_conv.shape[2:], x_prompt.dtype)
    gla0 = jnp.zeros((n_even, bsz) + state_gla.shape[2:], F32)
    y_p, conv_p, gla_p, k_p, v_p = _trunk(x_prompt, True, conv0, gla0, None, None, w)
    y_s, conv_s, gla_s, k_s, v_s = _trunk(x_sample, False, state_conv, state_gla, cache_k, cache_v, w)
    return (y_p, y_s, conv_p, gla_p, k_p, v_p, conv_s, gla_s, k_s, v_s)
```

```python
import functools

import jax
import jax.numpy as jnp
import numpy as np
from jax import lax
from jax.experimental import pallas as pl
from jax.experimental.pallas import tpu as pltpu

F32 = jnp.float32
BF16 = jnp.bfloat16

EPS = 1e-6
NEG_INF = -1e30
CHUNK = 64
CONV_WIDTH = 3
GLA_HEADS = 4
GATE_NORM = 16.0
ATT_HEAD_DIM = 64
BAND_ROWS = 8 * CHUNK
MAX_REL = 256
LANES = 128
MXU_DIM = 256
VMEM_LIMIT = 56 * 1024 * 1024


def _params(*sem):
    return pltpu.CompilerParams(dimension_semantics=sem, vmem_limit_bytes=VMEM_LIMIT)


def _resident(shape):
    nd = len(shape)
    return pl.BlockSpec(shape, lambda *_: (0,) * nd, pipeline_mode=pl.Buffered(1))


def _resident_layer(stacked, layer):
    nd = stacked.ndim - 1
    return pl.BlockSpec((None,) + stacked.shape[1:], lambda *_: (layer,) + (0,) * nd,
                        pipeline_mode=pl.Buffered(1))


def _dot(a, b):
    return jnp.dot(a, b, preferred_element_type=F32)


def _dot_nt(a, b):
    return lax.dot_general(a, b, (((1,), (1,)), ((), ())), preferred_element_type=F32)


def _dot_tn(a, b):
    return lax.dot_general(a, b, (((0,), (0,)), ((), ())), preferred_element_type=F32)


def _split2(a):
    hi = a.astype(BF16)
    lo = (a - hi.astype(F32)).astype(BF16)
    return hi, lo


def _split3(a):
    hi = a.astype(BF16)
    r = a - hi.astype(F32)
    mid = r.astype(BF16)
    lo = (r - mid.astype(F32)).astype(BF16)
    return hi, mid, lo


def _rms_rows(x, g):
    ms = jnp.mean(x * x, axis=-1, keepdims=True)
    return x * lax.rsqrt(ms + EPS) * g


def _silu(x):
    return x * jax.nn.sigmoid(x)


def _proj_body(x_ref, g_ref, w_ref, o_ref):
    h = _rms_rows(x_ref[...], g_ref[...]).astype(BF16)
    o_ref[...] = _dot(h, w_ref[...])


def _norm_proj(x, g, w, layer, tm):
    n, d = x.shape
    nout = w.shape[2]
    return pl.pallas_call(
        _proj_body,
        grid=(n // tm,),
        in_specs=[
            pl.BlockSpec((tm, d), lambda i: (i, 0)),
            _resident((1, d)),
            _resident_layer(w, layer),
        ],
        out_specs=pl.BlockSpec((tm, nout), lambda i: (i, 0)),
        out_shape=jax.ShapeDtypeStruct((n, nout), F32),
        compiler_params=_params("parallel"),
    )(x, g, w)


def _qkv_body(x_ref, g_ref, w_ref, pool_ref, poolt_ref, gq_ref, gk_ref,
              q_ref, k_ref, v_ref, kt_ref, vt_ref, *, att_dim, q_scale):
    h = _rms_rows(x_ref[...], g_ref[...]).astype(BF16)
    qkv = _dot(h, w_ref[...])

    def head_norm(a, gt):
        ms = _dot((a * a).astype(BF16), pool_ref[...]) * (1.0 / ATT_HEAD_DIM)
        rb = _dot(lax.rsqrt(ms + EPS).astype(BF16), poolt_ref[...])
        return a * rb * gt

    q_ref[...] = (head_norm(qkv[:, :att_dim], gq_ref[...]) * q_scale).astype(BF16)
    kn = head_norm(qkv[:, att_dim:2 * att_dim], gk_ref[...])
    v = qkv[:, 2 * att_dim:]
    k_ref[...] = kn.astype(BF16)
    v_ref[...] = v.astype(BF16)
    kt_ref[...] = kn
    vt_ref[...] = v


def _norm_qkv(x, g, w, layer, pool, poolt, gq, gk, tm, tail_div):
    n, d = x.shape
    att_dim = w.shape[2] // 3
    row = lambda i: (i, 0)
    tail = lambda i: (i // tail_div, 0)
    return pl.pallas_call(
        functools.partial(_qkv_body, att_dim=att_dim, q_scale=ATT_HEAD_DIM ** -0.5),
        grid=(n // tm,),
        in_specs=[
            pl.BlockSpec((tm, d), row),
            _resident((1, d)),
            _resident_layer(w, layer),
            _resident(pool.shape),
            _resident(poolt.shape),
            _resident((1, att_dim)),
            _resident((1, att_dim)),
        ],
        out_specs=[pl.BlockSpec((tm, att_dim), row)] * 3 + [pl.BlockSpec((tm, att_dim), tail)] * 2,
        out_shape=[jax.ShapeDtypeStruct((n, att_dim), BF16)] * 3
        + [jax.ShapeDtypeStruct((n // tail_div, att_dim), F32)] * 2,
        compiler_params=_params("arbitrary"),
    )(x, g, w, pool, poolt, gq, gk)


def _post_body(x_ref, a_ref, wo_ref, g_ref, win_ref, wout_ref, o_ref, *, d_ff, edges):
    x1 = x_ref[...] + _dot(a_ref[...], wo_ref[...])
    h = _rms_rows(x1, g_ref[...]).astype(BF16)
    acc = x1
    for lo, hi in zip(edges[:-1], edges[1:]):
        gate = _dot(h, win_ref[:, lo:hi])
        up = _dot(h, win_ref[:, d_ff + lo:d_ff + hi])
        act = (_silu(gate) * up).astype(BF16)
        acc = acc + _dot(act, wout_ref[lo:hi, :])
    o_ref[...] = acc


def _ffn_edges(d_ff, n_chunks):
    assert d_ff % MXU_DIM == 0
    tiles = d_ff // MXU_DIM
    return tuple(MXU_DIM * ((tiles * c + n_chunks - 1) // n_chunks) for c in range(n_chunks + 1))


def _post_mixer(x, a, wo, wo_layer, g, win, wout, layer, tm, n_chunks):
    n, d = x.shape
    d_ff = wout.shape[1]
    row = lambda i: (i, 0)
    return pl.pallas_call(
        functools.partial(_post_body, d_ff=d_ff, edges=_ffn_edges(d_ff, n_chunks)),
        grid=(n // tm,),
        in_specs=[
            pl.BlockSpec((tm, d), row),
            pl.BlockSpec((tm, a.shape[1]), row),
            _resident_layer(wo, wo_layer),
            _resident((1, d)),
            _resident_layer(win, layer),
            _resident_layer(wout, layer),
        ],
        out_specs=pl.BlockSpec((tm, d), row),
        out_shape=jax.ShapeDtypeStruct((n, d), F32),
        compiler_params=_params("parallel"),
    )(x, a, wo, g, win, wout)


def _mixer_body(p_ref, cw_ref, w2_ref, gb_ref, on_ref, conv0_ref, st0_ref,
                y_ref, convo_ref, sto_ref, ubuf, st_sc, *, tb, cdim, dk, dv):
    hk = dk // GLA_HEADS
    hv = dv // GLA_HEADS
    o_q = 3 * cdim
    o_k = o_q + dk
    o_v = o_k + dk
    o_g = o_v + dv
    o_l = o_g + dv
    t = pl.program_id(1)

    @pl.when(t == 0)
    def _():
        ubuf[0:8, :] = jnp.zeros((8, cdim), F32)
        ubuf[6:8, :] = conv0_ref[...]
        st_sc[...] = st0_ref[...]

    u = p_ref[:, 0:cdim] * p_ref[:, 2 * cdim:3 * cdim]
    ubuf[8:8 + tb, :] = u
    y = (cw_ref[0:1, :] * ubuf[6:6 + tb, :] + cw_ref[1:2, :] * ubuf[7:7 + tb, :]
         + cw_ref[2:3, :] * u)
    y_ref[:, 0:cdim] = (p_ref[:, cdim:2 * cdim] * y).astype(y_ref.dtype)
    tail = ubuf[tb + 6:tb + 8, :]
    ubuf[6:8, :] = tail
    convo_ref[...] = tail

    L = CHUNK
    assert hk == L and L & (L - 1) == 0
    sh = L.bit_length() - 1
    nst = GLA_HEADS * L
    r_i = lax.broadcasted_iota(jnp.int32, (nst, dk), 0)
    c_i = lax.broadcasted_iota(jnp.int32, (nst, dk), 1)
    same_head = (r_i >> sh) == (c_i >> sh)
    a_r = lax.broadcasted_iota(jnp.int32, (nst, nst), 0)
    a_c = lax.broadcasted_iota(jnp.int32, (nst, nst), 1)
    att_keep = ((a_r >> sh) == (a_c >> sh)) & ((a_c & (L - 1)) <= (a_r & (L - 1)))
    tri = (lax.broadcasted_iota(jnp.int32, (L, L), 1)
           <= lax.broadcasted_iota(jnp.int32, (L, L), 0)).astype(BF16)
    w2 = w2_ref[...]
    gb = gb_ref[...]
    onorm = on_ref[...]

    rows = [slice(c * L, (c + 1) * L) for c in range(tb // L)]

    def stack(a):
        return jnp.concatenate([a] * GLA_HEADS, axis=0)

    def heads_to_rows(o0, r):
        return jnp.concatenate([p_ref[r, o0 + h * hv:o0 + (h + 1) * hv] for h in range(GLA_HEADS)], axis=0)

    gk = _dot(p_ref[:, o_l:o_l + LANES].astype(BF16), w2) + gb
    la = (jnp.minimum(gk, 0.0) - jnp.log1p(jnp.exp(-jnp.abs(gk)))) * (1.0 / GATE_NORM)
    l_hi, l_mid, l_lo = _split3(la)
    b = [_dot(tri, l_hi[r]) + _dot(tri, l_mid[r]) + _dot(tri, l_lo[r]) for r in rows]
    b_last = [x[L - 1:L, :] for x in b]
    q_st, k_tl, kd_bd, v_st = [], [], [], []
    for c, r in enumerate(rows):
        q = p_ref[r, o_q:o_k] * (hk ** -0.5)
        k = p_ref[r, o_k:o_v]
        q_st.append(jnp.where(same_head, stack(q * jnp.exp(b[c])), 0.0).astype(BF16))
        k_tl.append(stack((k * jnp.exp(-b[c])).astype(BF16)))
        kd_bd.append(jnp.where(same_head, stack(k * jnp.exp(b_last[c] - b[c])), 0.0).astype(BF16))
        v_st.append(heads_to_rows(o_v, r).astype(BF16))
    att = [jnp.where(att_keep, _dot_nt(q_st[c], k_tl[c]), 0.0).astype(BF16)
           for c in range(len(rows))]
    upd = [_dot_tn(v_st[c], kd_bd[c]) for c in range(len(rows))]
    o = [_dot(att[c], v_st[c]) for c in range(len(rows))]
    st = st_sc[...]
    for c in range(len(rows)):
        o[c] = o[c] + _dot_nt(q_st[c], st.astype(BF16))
        st = st * jnp.exp(b_last[c]) + upd[c]
    for c, r in enumerate(rows):
        oc = _rms_rows(o[c], onorm) * _silu(heads_to_rows(o_g, r))
        for h in range(GLA_HEADS):
            y_ref[r, cdim + h * hv:cdim + (h + 1) * hv] = oc[h * L:(h + 1) * L, :].astype(y_ref.dtype)
    st_sc[...] = st
    sto_ref[...] = st


def _mixer_ab(proj, conv_w, w2pad, gb, onorm, conv0, st0, tb):
    bsz, t, width = proj.shape
    cdim = conv_w.shape[1]
    dk = w2pad.shape[1]
    hv = onorm.shape[1]
    dv = hv * GLA_HEADS
    return pl.pallas_call(
        functools.partial(_mixer_body, tb=tb, cdim=cdim, dk=dk, dv=dv),
        grid=(bsz, t // tb),
        in_specs=[
            pl.BlockSpec((None, tb, width), lambda b, i: (b, i, 0)),
            _resident(conv_w.shape),
            _resident(w2pad.shape),
            _resident(gb.shape),
            _resident(onorm.shape),
            pl.BlockSpec((None, CONV_WIDTH - 1, cdim), lambda b, i: (b, 0, 0)),
            pl.BlockSpec((None, hv, dk), lambda b, i: (b, 0, 0)),
        ],
        out_specs=[
            pl.BlockSpec((None, tb, cdim + dv), lambda b, i: (b, i, 0)),
            pl.BlockSpec((None, CONV_WIDTH - 1, cdim), lambda b, i: (b, 0, 0)),
            pl.BlockSpec((None, hv, dk), lambda b, i: (b, 0, 0)),
        ],
        out_shape=[
            jax.ShapeDtypeStruct((bsz, t, cdim + dv), BF16),
            jax.ShapeDtypeStruct((bsz, CONV_WIDTH - 1, cdim), F32),
            jax.ShapeDtypeStruct((bsz, hv, dk), F32),
        ],
        scratch_shapes=[
            pltpu.VMEM((tb + 8, cdim), F32),
            pltpu.VMEM((hv, dk), F32),
        ],
        compiler_params=_params("parallel", "arbitrary"),
    )(proj, conv_w, w2pad, gb, onorm, conv0, st0)


def _attn_body(q_ref, ka_ref, kb_ref, va_ref, vb_ref, brow_ref, o_ref, kw, vw, tab_ref,
               *, tq, tqc, hist_is_padding):
    nwin = BAND_ROWS + tqc

    @pl.when((pl.program_id(0) == 0) & (pl.program_id(1) == 0))
    def _build_table():
        wrow = brow_ref.shape[1]
        sh = CHUNK.bit_length() - 1
        rowi = lax.broadcasted_iota(jnp.int32, (tqc, wrow), 0)
        q_chunk = lax.broadcasted_iota(jnp.int32, (tqc, nwin), 0) >> sh
        k_chunk = (lax.broadcasted_iota(jnp.int32, (tqc, nwin), 1) >> sh) - BAND_ROWS // CHUNK
        allowed = (k_chunk <= q_chunk) & (k_chunk >= q_chunk - BAND_ROWS // CHUNK)

        def one_head(h, carry):
            y = jnp.broadcast_to(brow_ref[pl.ds(h, 1), :], (tqc, wrow))
            for bit in range(tqc.bit_length() - 1):
                y = jnp.where(((rowi >> bit) & 1) == 1, pltpu.roll(y, 1 << bit, axis=1), y)
            t = jnp.where(allowed, y[:, LANES:LANES + nwin], NEG_INF)
            tab_ref[h >> 1, pl.ds(pl.multiple_of((h & 1) * tqc, tqc), tqc), :] = t
            return carry

        lax.fori_loop(0, 2 * tab_ref.shape[0], one_head, 0)

    kw[0:BAND_ROWS, :] = ka_ref[...]
    kw[BAND_ROWS:, :] = kb_ref[...]
    vw[0:BAND_ROWS, :] = va_ref[...]
    vw[BAND_ROWS:, :] = vb_ref[...]
    n_pairs = q_ref.shape[1] // LANES
    lo = lax.broadcasted_iota(jnp.int32, (tqc, LANES), 1) < ATT_HEAD_DIM
    keep_q = ((lax.broadcasted_iota(jnp.int32, (2 * tqc, LANES), 1) < ATT_HEAD_DIM)
              == (lax.broadcasted_iota(jnp.int32, (2 * tqc, LANES), 0) < tqc))
    ones = jnp.ones((nwin, LANES), BF16)

    def run(first_block):
        def window(j):
            r0 = j * tqc
            c0 = max(BAND_ROWS - r0, 0) if first_block else 0
            return r0, c0, slice(r0 + c0, r0 + nwin)

        def scores(j, p):
            r0, c0, keys = window(j)
            lanes = slice(p * LANES, (p + 1) * LANES)
            qp = q_ref[r0:r0 + tqc, lanes]
            q2 = jnp.where(keep_q, jnp.concatenate([qp, qp], axis=0), 0.0)
            return _dot_nt(q2, kw[keys, lanes]) + tab_ref[p, :, c0:]

        def weighted_values(j, p, e):
            r0, c0, keys = window(j)
            lanes = slice(p * LANES, (p + 1) * LANES)
            pv = _dot(e, jnp.concatenate([vw[keys, lanes], ones[c0:]], axis=1))
            pv = pv[:, :LANES] / pv[:, LANES:]
            o_ref[r0:r0 + tqc, lanes] = jnp.where(lo, pv[:tqc], pv[tqc:]).astype(o_ref.dtype)

        items = [(j, p) for j in range(tq // tqc) for p in range(n_pairs)]
        s_next = scores(*items[0])
        pending = None
        for n, item in enumerate(items):
            s = s_next
            if n + 1 < len(items):
                s_next = scores(*items[n + 1])
            e = jnp.exp(s - jnp.max(s, axis=-1, keepdims=True))
            if pending is not None:
                weighted_values(*pending)
            pending = (*item, e.astype(BF16))
        weighted_values(*pending)

    if hist_is_padding:
        first = pl.program_id(1) == 0
        pl.when(first)(lambda: run(True))
        pl.when(jnp.logical_not(first))(lambda: run(False))
    else:
        run(False)


def _band_attention(q, k_hist, k_cur, v_hist, v_cur, brow, tq, tqc, hist_is_padding):
    bsz, t, width = q.shape
    assert t % tq == 0 and tq % tqc == 0
    if hist_is_padding:
        assert tq == BAND_ROWS
        k_hist, v_hist = k_cur, v_cur
        hist_map = lambda b, i: (b, jnp.maximum(i - 1, 0), 0)
    else:
        assert t == tq and k_hist.shape[1] == BAND_ROWS
        hist_map = lambda b, i: (b, 0, 0)
    cur_map = lambda b, i: (b, i, 0)
    return pl.pallas_call(
        functools.partial(_attn_body, tq=tq, tqc=tqc, hist_is_padding=hist_is_padding),
        grid=(bsz, t // tq),
        in_specs=[
            pl.BlockSpec((None, tq, width), cur_map),
            pl.BlockSpec((None, BAND_ROWS, width), hist_map),
            pl.BlockSpec((None, tq, width), cur_map),
            pl.BlockSpec((None, BAND_ROWS, width), hist_map),
            pl.BlockSpec((None, tq, width), cur_map),
            _resident(brow.shape),
        ],
        out_specs=pl.BlockSpec((None, tq, width), cur_map),
        out_shape=jax.ShapeDtypeStruct((bsz, t, width), BF16),
        scratch_shapes=[
            pltpu.VMEM((BAND_ROWS + tq, width), BF16),
            pltpu.VMEM((BAND_ROWS + tq, width), BF16),
            pltpu.VMEM((width // LANES, 2 * tqc, BAND_ROWS + tqc), F32),
        ],
        compiler_params=_params("arbitrary", "arbitrary"),
    )(q, k_hist, k_cur, v_hist, v_cur, brow)


def _bias_row(rel_bias, tqc):
    width = -(-(LANES + BAND_ROWS + tqc) // LANES) * LANES
    rel = np.clip(BAND_ROWS + LANES - np.arange(width), -(CHUNK - 1), MAX_REL) + (CHUNK - 1)
    return rel_bias[:, rel].astype(F32)


def _trunk(x, hist_is_padding, conv_prev, gla_prev, k_hist, v_hist, w):
    bsz, t, d = x.shape
    n = bsz * t
    depth = w["norm_mix"].shape[0]
    tm = min(512, n)
    tb = min(512, t)
    tq = min(BAND_ROWS, t)
    tqc = min(2 * CHUNK, t)
    if hist_is_padding:
        keep = BAND_ROWS
        assert t % tm == 0 and tm == keep
        tail_div = t // tm
    else:
        keep = k_hist.shape[2]
        assert n == tm and keep == BAND_ROWS and t <= keep
        tail_div = 1
    heads = w["rel_bias"].shape[1]
    xf = x.reshape(n, d)
    conv_new, gla_new, k_new, v_new = [], [], [], []
    for layer in range(depth):
        g_mix = w["norm_mix"][layer][None, :]
        g_ffn = w["norm_ffn"][layer][None, :]
        if layer % 2 == 0:
            e = layer // 2
            proj = _norm_proj(xf, g_mix, w["w_in_ab"], e, tm)
            st0 = gla_prev[e].reshape(bsz, -1, gla_prev.shape[-1]).swapaxes(1, 2)
            mix, cs, st = _mixer_ab(
                proj.reshape(bsz, t, -1), w["conv_w"][e], w["gk_w2"][e], w["gk_b"][e][None, :],
                w["gla_onorm"][e][None, :], conv_prev[e], st0, tb)
            conv_new.append(cs)
            gla_new.append(st.swapaxes(1, 2).reshape(gla_prev.shape[1:]))
            mix = mix.reshape(n, -1)
            w_o, w_o_layer = w["w_out_ab"], e
        else:
            o = layer // 2
            q, k, v, kt, vt = _norm_qkv(xf, g_mix, w["w_qkv"], o, w["pool"], w["poolt"],
                                        w["q_norm"][o], w["k_norm"][o], tm, tail_div)
            att_dim = k.shape[-1]
            shp = (bsz, t, att_dim)
            if hist_is_padding:
                kh = vh = None
            else:
                kh = k_hist[o].reshape(bsz, -1, att_dim).astype(BF16)
                vh = v_hist[o].reshape(bsz, -1, att_dim).astype(BF16)
            brow = _bias_row(w["rel_bias"][o], tqc)
            att = _band_attention(q.reshape(shp), kh, k.reshape(shp), vh, v.reshape(shp), brow,
                                  tq, tqc, hist_is_padding)
            k_new.append(kt)
            v_new.append(vt)
            mix = att.reshape(n, -1)
            w_o, w_o_layer = w["w_o_att"], o
        xf = _post_mixer(xf, mix, w_o, w_o_layer, g_ffn, w["w_ffn_in"], w["w_ffn_out"], layer, tm, 2)

    def cache(tails, hist):
        new = jnp.stack(tails).reshape(len(tails), bsz, -1, heads, ATT_HEAD_DIM)
        if hist_is_padding:
            return new
        return jnp.concatenate([hist[:, :, t:], new], axis=2)

    return (xf.reshape(bsz, t, d), jnp.stack(conv_new), jnp.stack(gla_new),
            cache(k_new, k_hist), cache(v_new, v_hist))


def kernel(x_prompt, x_sample, state_conv, state_gla, cache_k, cache_v, norm_mix, norm_ffn,
           w_in_ab, conv_w, gla_gk_w2, gla_gk_b, gla_onorm, w_out_ab, w_qkv, q_norm, k_norm,
           rel_bias, w_o_att, w_ffn_in, w_ffn_out):
    bsz = x_prompt.shape[0]
    n_even = state_conv.shape[0]
    n_odd = cache_k.shape[0]
    heads, head_dim = cache_k.shape[-2:]
    att_dim = heads * head_dim
    in_ab = w_in_ab.shape[-1]
    in_pad = -in_ab % LANES
    rank = gla_gk_w2.shape[1]
    pool = (np.arange(att_dim)[:, None] // head_dim == np.arange(LANES)[None, :])
    w = {
        "norm_mix": norm_mix, "norm_ffn": norm_ffn,
        "w_in_ab": jnp.pad(w_in_ab, ((0, 0), (0, 0), (0, in_pad))).astype(BF16),
        "conv_w": conv_w,
        "gk_w2": jnp.pad(gla_gk_w2, ((0, 0), (0, LANES - rank), (0, 0))).astype(BF16),
        "gk_b": gla_gk_b, "gla_onorm": gla_onorm,
        "w_out_ab": w_out_ab.astype(BF16),
        "w_qkv": w_qkv.astype(BF16),
        "q_norm": jnp.tile(q_norm, (1, heads))[:, None, :],
        "k_norm": jnp.tile(k_norm, (1, heads))[:, None, :],
        "rel_bias": rel_bias,
        "w_o_att": w_o_att.astype(BF16),
        "w_ffn_in": w_ffn_in.astype(BF16), "w_ffn_out": w_ffn_out.astype(BF16),
        "pool": jnp.asarray(pool, BF16), "poolt": jnp.asarray(pool.T, BF16),
    }
    conv0 = jnp.zeros((n_even, bsz) + state_conv.shape[2:], x_prompt.dtype)
    gla0 = jnp.zeros((n_even, bsz) + state_gla.shape[2:], F32)
    y_p, conv_p, gla_p, k_p, v_p = _trunk(x_prompt, True, conv0, gla0, None, None, w)
    y_s, conv_s, gla_s, k_s, v_s = _trunk(x_sample, False, state_conv, state_gla, cache_k, cache_v, w)
    return (y_p, y_s, conv_p, gla_p, k_p, v_p, conv_s, gla_s, k_s, v_s)
```

```python
import functools

import jax
import jax.numpy as jnp
import numpy as np
from jax import lax
from jax.experimental import pallas as pl
from jax.experimental.pallas import tpu as pltpu

F32 = jnp.float32
BF16 = jnp.bfloat16

EPS = 1e-6
NEG_INF = -1e30
CHUNK = 64
CONV_WIDTH = 3
GLA_HEADS = 4
GATE_NORM = 16.0
ATT_HEAD_DIM = 64
BAND_ROWS = 8 * CHUNK
MAX_REL = 256
LANES = 128
MXU_DIM = 256
VMEM_LIMIT = 56 * 1024 * 1024


def _params(*sem):
    return pltpu.CompilerParams(dimension_semantics=sem, vmem_limit_bytes=VMEM_LIMIT)


def _resident(shape):
    nd = len(shape)
    return pl.BlockSpec(shape, lambda *_: (0,) * nd, pipeline_mode=pl.Buffered(1))


def _resident_layer(stacked, layer):
    nd = stacked.ndim - 1
    return pl.BlockSpec((None,) + stacked.shape[1:], lambda *_: (layer,) + (0,) * nd,
                        pipeline_mode=pl.Buffered(1))


def _dot(a, b):
    return jnp.dot(a, b, preferred_element_type=F32)


def _dot_nt(a, b):
    return lax.dot_general(a, b, (((1,), (1,)), ((), ())), preferred_element_type=F32)


def _dot_tn(a, b):
    return lax.dot_general(a, b, (((0,), (0,)), ((), ())), preferred_element_type=F32)


def _split2(a):
    hi = a.astype(BF16)
    lo = (a - hi.astype(F32)).astype(BF16)
    return hi, lo


def _split3(a):
    hi = a.astype(BF16)
    r = a - hi.astype(F32)
    mid = r.astype(BF16)
    lo = (r - mid.astype(F32)).astype(BF16)
    return hi, mid, lo


def _rms_rows(x, g):
    ms = jnp.mean(x * x, axis=-1, keepdims=True)
    return x * lax.rsqrt(ms + EPS) * g


def _silu(x):
    return x * jax.nn.sigmoid(x)


def _proj_body(x_ref, g_ref, w_ref, o_ref):
    h = _rms_rows(x_ref[...], g_ref[...]).astype(BF16)
    o_ref[...] = _dot(h, w_ref[...])


def _norm_proj(x, g, w, layer, tm):
    n, d = x.shape
    nout = w.shape[2]
    return pl.pallas_call(
        _proj_body,
        grid=(n // tm,),
        in_specs=[
            pl.BlockSpec((tm, d), lambda i: (i, 0)),
            _resident((1, d)),
            _resident_layer(w, layer),
        ],
        out_specs=pl.BlockSpec((tm, nout), lambda i: (i, 0)),
        out_shape=jax.ShapeDtypeStruct((n, nout), F32),
        compiler_params=_params("parallel"),
    )(x, g, w)


def _qkv_body(x_ref, g_ref, w_ref, pool_ref, poolt_ref, gq_ref, gk_ref,
              q_ref, k_ref, v_ref, kt_ref, vt_ref, *, att_dim, q_scale):
    h = _rms_rows(x_ref[...], g_ref[...]).astype(BF16)
    qkv = _dot(h, w_ref[...])

    def head_norm(a, gt):
        ms = _dot((a * a).astype(BF16), pool_ref[...]) * (1.0 / ATT_HEAD_DIM)
        rb = _dot(lax.rsqrt(ms + EPS).astype(BF16), poolt_ref[...])
        return a * rb * gt

    q_ref[...] = (head_norm(qkv[:, :att_dim], gq_ref[...]) * q_scale).astype(BF16)
    kn = head_norm(qkv[:, att_dim:2 * att_dim], gk_ref[...])
    v = qkv[:, 2 * att_dim:]
    k_ref[...] = kn.astype(BF16)
    v_ref[...] = v.astype(BF16)
    kt_ref[...] = kn
    vt_ref[...] = v


def _norm_qkv(x, g, w, layer, pool, poolt, gq, gk, tm, tail_div):
    n, d = x.shape
    att_dim = w.shape[2] // 3
    row = lambda i: (i, 0)
    tail = lambda i: (i // tail_div, 0)
    return pl.pallas_call(
        functools.partial(_qkv_body, att_dim=att_dim, q_scale=ATT_HEAD_DIM ** -0.5),
        grid=(n // tm,),
        in_specs=[
            pl.BlockSpec((tm, d), row),
            _resident((1, d)),
            _resident_layer(w, layer),
            _resident(pool.shape),
            _resident(poolt.shape),
            _resident((1, att_dim)),
            _resident((1, att_dim)),
        ],
        out_specs=[pl.BlockSpec((tm, att_dim), row)] * 3 + [pl.BlockSpec((tm, att_dim), tail)] * 2,
        out_shape=[jax.ShapeDtypeStruct((n, att_dim), BF16)] * 3
        + [jax.ShapeDtypeStruct((n // tail_div, att_dim), F32)] * 2,
        compiler_params=_params("arbitrary"),
    )(x, g, w, pool, poolt, gq, gk)


def _post_body(x_ref, a_ref, wo_ref, g_ref, win_ref, wout_ref, o_ref, *, d_ff, edges):
    x1 = x_ref[...] + _dot(a_ref[...], wo_ref[...])
    h = _rms_rows(x1, g_ref[...]).astype(BF16)
    acc = x1
    for lo, hi in zip(edges[:-1], edges[1:]):
        gate = _dot(h, win_ref[:, lo:hi])
        up = _dot(h, win_ref[:, d_ff + lo:d_ff + hi])
        act = (_silu(gate) * up).astype(BF16)
        acc = acc + _dot(act, wout_ref[lo:hi, :])
    o_ref[...] = acc


def _ffn_edges(d_ff, n_chunks):
    assert d_ff % MXU_DIM == 0
    tiles = d_ff // MXU_DIM
    return tuple(MXU_DIM * ((tiles * c + n_chunks - 1) // n_chunks) for c in range(n_chunks + 1))


def _post_mixer(x, a, wo, wo_layer, g, win, wout, layer, tm, n_chunks):
    n, d = x.shape
    d_ff = wout.shape[1]
    row = lambda i: (i, 0)
    return pl.pallas_call(
        functools.partial(_post_body, d_ff=d_ff, edges=_ffn_edges(d_ff, n_chunks)),
        grid=(n // tm,),
        in_specs=[
            pl.BlockSpec((tm, d), row),
            pl.BlockSpec((tm, a.shape[1]), row),
            _resident_layer(wo, wo_layer),
            _resident((1, d)),
            _resident_layer(win, layer),
            _resident_layer(wout, layer),
        ],
        out_specs=pl.BlockSpec((tm, d), row),
        out_shape=jax.ShapeDtypeStruct((n, d), F32),
        compiler_params=_params("parallel"),
    )(x, a, wo, g, win, wout)


def _mixer_body(p_ref, cw_ref, w2_ref, gb_ref, on_ref, conv0_ref, st0_ref,
                y_ref, convo_ref, sto_ref, ubuf, st_sc, *, tb, cdim, dk, dv):
    hk = dk // GLA_HEADS
    hv = dv // GLA_HEADS
    o_q = 3 * cdim
    o_k = o_q + dk
    o_v = o_k + dk
    o_g = o_v + dv
    o_l = o_g + dv
    t = pl.program_id(1)

    @pl.when(t == 0)
    def _():
        ubuf[0:8, :] = jnp.zeros((8, cdim), F32)
        ubuf[6:8, :] = conv0_ref[...]
        st_sc[...] = st0_ref[...]

    u = p_ref[:, 0:cdim] * p_ref[:, 2 * cdim:3 * cdim]
    ubuf[8:8 + tb, :] = u
    y = (cw_ref[0:1, :] * ubuf[6:6 + tb, :] + cw_ref[1:2, :] * ubuf[7:7 + tb, :]
         + cw_ref[2:3, :] * u)
    y_ref[:, 0:cdim] = (p_ref[:, cdim:2 * cdim] * y).astype(y_ref.dtype)
    tail = ubuf[tb + 6:tb + 8, :]
    ubuf[6:8, :] = tail
    convo_ref[...] = tail

    L = CHUNK
    assert hk == L and L & (L - 1) == 0
    sh = L.bit_length() - 1
    nst = GLA_HEADS * L
    r_i = lax.broadcasted_iota(jnp.int32, (nst, dk), 0)
    c_i = lax.broadcasted_iota(jnp.int32, (nst, dk), 1)
    same_head = (r_i >> sh) == (c_i >> sh)
    a_r = lax.broadcasted_iota(jnp.int32, (nst, nst), 0)
    a_c = lax.broadcasted_iota(jnp.int32, (nst, nst), 1)
    att_keep = ((a_r >> sh) == (a_c >> sh)) & ((a_c & (L - 1)) <= (a_r & (L - 1)))
    tri = (lax.broadcasted_iota(jnp.int32, (L, L), 1)
           <= lax.broadcasted_iota(jnp.int32, (L, L), 0)).astype(BF16)
    w2 = w2_ref[...]
    gb = gb_ref[...]
    onorm = on_ref[...]

    rows = [slice(c * L, (c + 1) * L) for c in range(tb // L)]

    def stack(a):
        return jnp.concatenate([a] * GLA_HEADS, axis=0)

    def heads_to_rows(o0, r):
        return jnp.concatenate([p_ref[r, o0 + h * hv:o0 + (h + 1) * hv] for h in range(GLA_HEADS)], axis=0)

    gk = _dot(p_ref[:, o_l:o_l + LANES].astype(BF16), w2) + gb
    la = (jnp.minimum(gk, 0.0) - jnp.log1p(jnp.exp(-jnp.abs(gk)))) * (1.0 / GATE_NORM)
    l_hi, l_mid, l_lo = _split3(la)
    b = [_dot(tri, l_hi[r]) + _dot(tri, l_mid[r]) + _dot(tri, l_lo[r]) for r in rows]
    b_last = [x[L - 1:L, :] for x in b]
    q_st, k_tl, kd_bd, v_st = [], [], [], []
    for c, r in enumerate(rows):
        q = p_ref[r, o_q:o_k] * (hk ** -0.5)
        k = p_ref[r, o_k:o_v]
        q_st.append(jnp.where(same_head, stack(q * jnp.exp(b[c])), 0.0).astype(BF16))
        k_tl.append(stack((k * jnp.exp(-b[c])).astype(BF16)))
        kd_bd.append(jnp.where(same_head, stack(k * jnp.exp(b_last[c] - b[c])), 0.0).astype(BF16))
        v_st.append(heads_to_rows(o_v, r).astype(BF16))
    att = [jnp.where(att_keep, _dot_nt(q_st[c], k_tl[c]), 0.0).astype(BF16)
           for c in range(len(rows))]
    upd = [_dot_tn(v_st[c], kd_bd[c]) for c in range(len(rows))]
    o = [_dot(att[c], v_st[c]) for c in range(len(rows))]
    st = st_sc[...]
    for c in range(len(rows)):
        o[c] = o[c] + _dot_nt(q_st[c], st.astype(BF16))
        st = st * jnp.exp(b_last[c]) + upd[c]
    for c, r in enumerate(rows):
        oc = _rms_rows(o[c], onorm) * _silu(heads_to_rows(o_g, r))
        for h in range(GLA_HEADS):
            y_ref[r, cdim + h * hv:cdim + (h + 1) * hv] = oc[h * L:(h + 1) * L, :].astype(y_ref.dtype)
    st_sc[...] = st
    sto_ref[...] = st


def _mixer_ab(proj, conv_w, w2pad, gb, onorm, conv0, st0, tb):
    bsz, t, width = proj.shape
    cdim = conv_w.shape[1]
    dk = w2pad.shape[1]
    hv = onorm.shape[1]
    dv = hv * GLA_HEADS
    return pl.pallas_call(
        functools.partial(_mixer_body, tb=tb, cdim=cdim, dk=dk, dv=dv),
        grid=(bsz, t // tb),
        in_specs=[
            pl.BlockSpec((None, tb, width), lambda b, i: (b, i, 0)),
            _resident(conv_w.shape),
            _resident(w2pad.shape),
            _resident(gb.shape),
            _resident(onorm.shape),
            pl.BlockSpec((None, CONV_WIDTH - 1, cdim), lambda b, i: (b, 0, 0)),
            pl.BlockSpec((None, hv, dk), lambda b, i: (b, 0, 0)),
        ],
        out_specs=[
            pl.BlockSpec((None, tb, cdim + dv), lambda b, i: (b, i, 0)),
            pl.BlockSpec((None, CONV_WIDTH - 1, cdim), lambda b, i: (b, 0, 0)),
            pl.BlockSpec((None, hv, dk), lambda b, i: (b, 0, 0)),
        ],
        out_shape=[
            jax.ShapeDtypeStruct((bsz, t, cdim + dv), BF16),
            jax.ShapeDtypeStruct((bsz, CONV_WIDTH - 1, cdim), F32),
            jax.ShapeDtypeStruct((bsz, hv, dk), F32),
        ],
        scratch_shapes=[
            pltpu.VMEM((tb + 8, cdim), F32),
            pltpu.VMEM((hv, dk), F32),
        ],
        compiler_params=_params("parallel", "arbitrary"),
    )(proj, conv_w, w2pad, gb, onorm, conv0, st0)


def _attn_body(q_ref, ka_ref, kb_ref, va_ref, vb_ref, brow_ref, o_ref, kw, vw, tab_ref,
               *, tq, tqc, hist_is_padding):
    nwin = BAND_ROWS + tqc

    @pl.when((pl.program_id(0) == 0) & (pl.program_id(1) == 0))
    def _build_table():
        wrow = brow_ref.shape[1]
        sh = CHUNK.bit_length() - 1
        rowi = lax.broadcasted_iota(jnp.int32, (tqc, wrow), 0)
        q_chunk = lax.broadcasted_iota(jnp.int32, (tqc, nwin), 0) >> sh
        k_chunk = (lax.broadcasted_iota(jnp.int32, (tqc, nwin), 1) >> sh) - BAND_ROWS // CHUNK
        allowed = (k_chunk <= q_chunk) & (k_chunk >= q_chunk - BAND_ROWS // CHUNK)

        def one_head(h, carry):
            y = jnp.broadcast_to(brow_ref[pl.ds(h, 1), :], (tqc, wrow))
            for bit in range(tqc.bit_length() - 1):
                y = jnp.where(((rowi >> bit) & 1) == 1, pltpu.roll(y, 1 << bit, axis=1), y)
            t = jnp.where(allowed, y[:, LANES:LANES + nwin], NEG_INF)
            tab_ref[h >> 1, pl.ds(pl.multiple_of((h & 1) * tqc, tqc), tqc), :] = t
            return carry

        lax.fori_loop(0, 2 * tab_ref.shape[0], one_head, 0)

    kw[0:BAND_ROWS, :] = ka_ref[...].astype(BF16)
    kw[BAND_ROWS:, :] = kb_ref[...]
    vw[0:BAND_ROWS, :] = va_ref[...].astype(BF16)
    vw[BAND_ROWS:, :] = vb_ref[...]
    n_pairs = q_ref.shape[1] // LANES
    lo = lax.broadcasted_iota(jnp.int32, (tqc, LANES), 1) < ATT_HEAD_DIM
    keep_q = ((lax.broadcasted_iota(jnp.int32, (2 * tqc, LANES), 1) < ATT_HEAD_DIM)
              == (lax.broadcasted_iota(jnp.int32, (2 * tqc, LANES), 0) < tqc))
    ones = jnp.ones((nwin, LANES), BF16)

    def run(first_block):
        def window(j):
            r0 = j * tqc
            c0 = max(BAND_ROWS - r0, 0) if first_block else 0
            return r0, c0, slice(r0 + c0, r0 + nwin)

        def scores(j, p):
            r0, c0, keys = window(j)
            lanes = slice(p * LANES, (p + 1) * LANES)
            qp = q_ref[r0:r0 + tqc, lanes]
            q2 = jnp.where(keep_q, jnp.concatenate([qp, qp], axis=0), 0.0)
            return _dot_nt(q2, kw[keys, lanes]) + tab_ref[p, :, c0:]

        def weighted_values(j, p, e):
            r0, c0, keys = window(j)
            lanes = slice(p * LANES, (p + 1) * LANES)
            pv = _dot(e, jnp.concatenate([vw[keys, lanes], ones[c0:]], axis=1))
            pv = pv[:, :LANES] / pv[:, LANES:]
            o_ref[r0:r0 + tqc, lanes] = jnp.where(lo, pv[:tqc], pv[tqc:]).astype(o_ref.dtype)

        for j in range(tq // tqc):
            s_next = scores(j, 0)
            pending = None
            for p in range(n_pairs):
                s = s_next
                if p + 1 < n_pairs:
                    s_next = scores(j, p + 1)
                e = jnp.exp(s - jnp.max(s, axis=-1, keepdims=True))
                if pending is not None:
                    weighted_values(*pending)
                pending = (j, p, e.astype(BF16))
            weighted_values(*pending)

    if hist_is_padding:
        first = pl.program_id(1) == 0
        pl.when(first)(lambda: run(True))
        pl.when(jnp.logical_not(first))(lambda: run(False))
    else:
        run(False)


def _band_attention(q, k_hist, k_cur, v_hist, v_cur, brow, tq, tqc, hist_is_padding):
    bsz, t, width = q.shape
    assert t % tq == 0 and tq % tqc == 0
    if hist_is_padding:
        assert tq == BAND_ROWS
        k_hist, v_hist = k_cur, v_cur
        hist_map = lambda b, i: (b, jnp.maximum(i - 1, 0), 0)
    else:
        assert t == tq and k_hist.shape[1] == BAND_ROWS
        hist_map = lambda b, i: (b, 0, 0)
    cur_map = lambda b, i: (b, i, 0)
    return pl.pallas_call(
        functools.partial(_attn_body, tq=tq, tqc=tqc, hist_is_padding=hist_is_padding),
        grid=(bsz, t // tq),
        in_specs=[
            pl.BlockSpec((None, tq, width), cur_map),
            pl.BlockSpec((None, BAND_ROWS, width), hist_map),
            pl.BlockSpec((None, tq, width), cur_map),
            pl.BlockSpec((None, BAND_ROWS, width), hist_map),
            pl.BlockSpec((None, tq, width), cur_map),
            _resident(brow.shape),
        ],
        out_specs=pl.BlockSpec((None, tq, width), cur_map),
        out_shape=jax.ShapeDtypeStruct((bsz, t, width), BF16),
        scratch_shapes=[
            pltpu.VMEM((BAND_ROWS + tq, width), BF16),
            pltpu.VMEM((BAND_ROWS + tq, width), BF16),
            pltpu.VMEM((width // LANES, 2 * tqc, BAND_ROWS + tqc), F32),
        ],
        compiler_params=_params("arbitrary", "arbitrary"),
    )(q, k_hist, k_cur, v_hist, v_cur, brow)


def _bias_row(rel_bias, tqc):
    width = -(-(LANES + BAND_ROWS + tqc) // LANES) * LANES
    rel = np.clip(BAND_ROWS + LANES - np.arange(width), -(CHUNK - 1), MAX_REL) + (CHUNK - 1)
    return rel_bias[:, rel].astype(F32)


def _trunk(x, hist_is_padding, conv_prev, gla_prev, k_hist, v_hist, w):
    bsz, t, d = x.shape
    n = bsz * t
    depth = w["norm_mix"].shape[0]
    tm = min(512, n)
    tb = min(512, t)
    tq = min(BAND_ROWS, t)
    tqc = min(2 * CHUNK, t)
    if hist_is_padding:
        keep = BAND_ROWS
        assert t % tm == 0 and tm == keep
        tail_div = t // tm
    else:
        keep = k_hist.shape[2]
        assert n == tm and keep == BAND_ROWS and t <= keep
        tail_div = 1
    heads = w["rel_bias"].shape[1]
    xf = x.reshape(n, d)
    conv_new, gla_new, k_new, v_new = [], [], [], []
    for layer in range(depth):
        g_mix = w["norm_mix"][layer][None, :]
        g_ffn = w["norm_ffn"][layer][None, :]
        if layer % 2 == 0:
            e = layer // 2
            proj = _norm_proj(xf, g_mix, w["w_in_ab"], e, tm)
            st0 = gla_prev[e].reshape(bsz, -1, gla_prev.shape[-1]).swapaxes(1, 2)
            mix, cs, st = _mixer_ab(
                proj.reshape(bsz, t, -1), w["conv_w"][e], w["gk_w2"][e], w["gk_b"][e][None, :],
                w["gla_onorm"][e][None, :], conv_prev[e], st0, tb)
            conv_new.append(cs)
            gla_new.append(st.swapaxes(1, 2).reshape(gla_prev.shape[1:]))
            mix = mix.reshape(n, -1)
            w_o, w_o_layer = w["w_out_ab"], e
        else:
            o = layer // 2
            q, k, v, kt, vt = _norm_qkv(xf, g_mix, w["w_qkv"], o, w["pool"], w["poolt"],
                                        w["q_norm"][o], w["k_norm"][o], tm, tail_div)
            att_dim = k.shape[-1]
            shp = (bsz, t, att_dim)
            if hist_is_padding:
                kh = vh = None
            else:
                kh = k_hist[o].reshape(bsz, -1, att_dim)
                vh = v_hist[o].reshape(bsz, -1, att_dim)
            brow = _bias_row(w["rel_bias"][o], tqc)
            att = _band_attention(q.reshape(shp), kh, k.reshape(shp), vh, v.reshape(shp), brow,
                                  tq, tqc, hist_is_padding)
            k_new.append(kt)
            v_new.append(vt)
            mix = att.reshape(n, -1)
            w_o, w_o_layer = w["w_o_att"], o
        xf = _post_mixer(xf, mix, w_o, w_o_layer, g_ffn, w["w_ffn_in"], w["w_ffn_out"], layer, tm, 2)

    def cache(tails, hist):
        new = jnp.stack(tails).reshape(len(tails), bsz, -1, heads, ATT_HEAD_DIM)
        if hist_is_padding:
            return new
        return jnp.concatenate([hist[:, :, t:], new], axis=2)

    return (xf.reshape(bsz, t, d), jnp.stack(conv_new), jnp.stack(gla_new),
            cache(k_new, k_hist), cache(v_new, v_hist))


def kernel(x_prompt, x_sample, state_conv, state_gla, cache_k, cache_v, norm_mix, norm_ffn,
           w_in_ab, conv_w, gla_gk_w2, gla_gk_b, gla_onorm, w_out_ab, w_qkv, q_norm, k_norm,
           rel_bias, w_o_att, w_ffn_in, w_ffn_out):
    bsz = x_prompt.shape[0]
    n_even = state_conv.shape[0]
    n_odd = cache_k.shape[0]
    heads, head_dim = cache_k.shape[-2:]
    att_dim = heads * head_dim
    in_ab = w_in_ab.shape[-1]
    in_pad = -in_ab % LANES
    rank = gla_gk_w2.shape[1]
    pool = (np.arange(att_dim)[:, None] // head_dim == np.arange(LANES)[None, :])
    w = {
        "norm_mix": norm_mix, "norm_ffn": norm_ffn,
        "w_in_ab": jnp.pad(w_in_ab, ((0, 0), (0, 0), (0, in_pad))).astype(BF16),
        "conv_w": conv_w,
        "gk_w2": jnp.pad(gla_gk_w2, ((0, 0), (0, LANES - rank), (0, 0))).astype(BF16),
        "gk_b": gla_gk_b, "gla_onorm": gla_onorm,
        "w_out_ab": w_out_ab.astype(BF16),
        "w_qkv": w_qkv.astype(BF16),
        "q_norm": jnp.tile(q_norm, (1, heads))[:, None, :],
        "k_norm": jnp.tile(k_norm, (1, heads))[:, None, :],
        "rel_bias": rel_bias,
        "w_o_att": w_o_att.astype(BF16),
        "w_ffn_in": w_ffn_in.astype(BF16), "w_ffn_out": w_ffn_out.astype(BF16),
        "pool": jnp.asarray(pool, BF16), "poolt": jnp.asarray(pool.T, BF16),
    }
    conv0 = jnp.zeros((n_even, bsz) + state_conv.shape[2:], x_prompt.dtype)
    gla0 = jnp.zeros((n_even, bsz) + state_gla.shape[2:], F32)
    y_p, conv_p, gla_p, k_p, v_p = _trunk(x_prompt, True, conv0, gla0, None, None, w)
    y_s, conv_s, gla_s, k_s, v_s = _trunk(x_sample, False, state_conv, state_gla, cache_k, cache_v, w)
    return (y_p, y_s, conv_p, gla_p, k_p, v_p, conv_s, gla_s, k_s, v_s)
```

```python
import functools

import jax
import jax.numpy as jnp
import numpy as np
from jax import lax
from jax.experimental import pallas as pl
from jax.experimental.pallas import tpu as pltpu

F32 = jnp.float32
BF16 = jnp.bfloat16

EPS = 1e-6
NEG_INF = -1e30
CHUNK = 64
CONV_WIDTH = 3
GLA_HEADS = 4
GATE_NORM = 16.0
ATT_HEAD_DIM = 64
BAND_ROWS = 8 * CHUNK
MAX_REL = 256
LANES = 128
MXU_DIM = 256
VMEM_LIMIT = 56 * 1024 * 1024


def _params(*sem):
    return pltpu.CompilerParams(dimension_semantics=sem, vmem_limit_bytes=VMEM_LIMIT)


def _resident(shape):
    nd = len(shape)
    return pl.BlockSpec(shape, lambda *_: (0,) * nd, pipeline_mode=pl.Buffered(1))


def _resident_layer(stacked, layer):
    nd = stacked.ndim - 1
    return pl.BlockSpec((None,) + stacked.shape[1:], lambda *_: (layer,) + (0,) * nd,
                        pipeline_mode=pl.Buffered(1))


def _dot(a, b):
    return jnp.dot(a, b, preferred_element_type=F32)


def _dot_nt(a, b):
    return lax.dot_general(a, b, (((1,), (1,)), ((), ())), preferred_element_type=F32)


def _dot_tn(a, b):
    return lax.dot_general(a, b, (((0,), (0,)), ((), ())), preferred_element_type=F32)


def _split2(a):
    hi = a.astype(BF16)
    lo = (a - hi.astype(F32)).astype(BF16)
    return hi, lo


def _split3(a):
    hi = a.astype(BF16)
    r = a - hi.astype(F32)
    mid = r.astype(BF16)
    lo = (r - mid.astype(F32)).astype(BF16)
    return hi, mid, lo


def _rms_rows(x, g):
    ms = jnp.mean(x * x, axis=-1, keepdims=True)
    return x * lax.rsqrt(ms + EPS) * g


def _silu(x):
    return x * jax.nn.sigmoid(x)


def _proj_body(x_ref, g_ref, w_ref, o_ref):
    h = _rms_rows(x_ref[...], g_ref[...]).astype(BF16)
    o_ref[...] = _dot(h, w_ref[...])


def _rowwise_call(body, row_groups, shared, outs, tm):
    n_groups, n_in, n_out = len(row_groups), len(row_groups[0]), len(outs)
    nblk = [grp[0].shape[0] // tm for grp in row_groups]
    start = [sum(nblk[:g]) for g in range(n_groups)]

    def local(g):
        return lambda i: jnp.clip(i - start[g], 0, nblk[g] - 1)

    in_specs, operands, out_specs, out_shape = [], [], [], []
    for g, grp in enumerate(row_groups):
        for a in grp:
            assert a.shape[0] == nblk[g] * tm
            in_specs.append(pl.BlockSpec((tm, a.shape[1]), lambda i, f=local(g): (f(i), 0)))
            operands.append(a)
    for a, spec in shared:
        in_specs.append(spec)
        operands.append(a)
    for g in range(n_groups):
        for cols, dtype, keep in outs:
            assert nblk[g] % keep[g] == 0
            out_specs.append(pl.BlockSpec((tm, cols), lambda i, f=local(g), k=keep[g]: (f(i) // k, 0)))
            out_shape.append(jax.ShapeDtypeStruct((nblk[g] // keep[g] * tm, cols), dtype))

    def per_group(*refs):
        ins, sh = refs[:n_groups * n_in], refs[n_groups * n_in:n_groups * n_in + len(shared)]
        out_refs = refs[n_groups * n_in + len(shared):]
        i = pl.program_id(0)
        for g in range(n_groups):
            pl.when((i >= start[g]) & (i < start[g] + nblk[g]))(functools.partial(
                body, *ins[g * n_in:(g + 1) * n_in], *sh, *out_refs[g * n_out:(g + 1) * n_out]))

    res = pl.pallas_call(
        per_group,
        grid=(sum(nblk),),
        in_specs=in_specs,
        out_specs=out_specs,
        out_shape=out_shape,
        compiler_params=_params("arbitrary"),
    )(*operands)
    return [res[g * n_out:(g + 1) * n_out] for g in range(n_groups)]


def _norm_proj(xs, g, w, layer, tm):
    d = xs[0].shape[1]
    res = _rowwise_call(
        _proj_body, [[x] for x in xs],
        [(g, _resident((1, d))), (w, _resident_layer(w, layer))],
        [(w.shape[2], F32, [1] * len(xs))], tm)
    return [r[0] for r in res]


def _qkv_body(x_ref, g_ref, w_ref, pool_ref, poolt_ref, gq_ref, gk_ref,
              q_ref, k_ref, v_ref, kt_ref, vt_ref, *, att_dim, q_scale):
    h = _rms_rows(x_ref[...], g_ref[...]).astype(BF16)
    qkv = _dot(h, w_ref[...])

    def head_norm(a, gt):
        ms = _dot((a * a).astype(BF16), pool_ref[...]) * (1.0 / ATT_HEAD_DIM)
        rb = _dot(lax.rsqrt(ms + EPS).astype(BF16), poolt_ref[...])
        return a * rb * gt

    q_ref[...] = (head_norm(qkv[:, :att_dim], gq_ref[...]) * q_scale).astype(BF16)
    kn = head_norm(qkv[:, att_dim:2 * att_dim], gk_ref[...])
    v = qkv[:, 2 * att_dim:]
    k_ref[...] = kn.astype(BF16)
    v_ref[...] = v.astype(BF16)
    kt_ref[...] = kn
    vt_ref[...] = v


def _norm_qkv(xs, g, w, layer, pool, poolt, gq, gk, tm, tail_divs):
    d = xs[0].shape[1]
    att_dim = w.shape[2] // 3
    every = [1] * len(xs)
    return _rowwise_call(
        functools.partial(_qkv_body, att_dim=att_dim, q_scale=ATT_HEAD_DIM ** -0.5),
        [[x] for x in xs],
        [(g, _resident((1, d))), (w, _resident_layer(w, layer)), (pool, _resident(pool.shape)),
         (poolt, _resident(poolt.shape)), (gq, _resident((1, att_dim))), (gk, _resident((1, att_dim)))],
        [(att_dim, BF16, every)] * 3 + [(att_dim, F32, tail_divs)] * 2, tm)


def _post_body(x_ref, a_ref, wo_ref, g_ref, win_ref, wout_ref, o_ref, *, d_ff, edges):
    x1 = x_ref[...] + _dot(a_ref[...], wo_ref[...])
    h = _rms_rows(x1, g_ref[...]).astype(BF16)
    acc = x1
    for lo, hi in zip(edges[:-1], edges[1:]):
        gate = _dot(h, win_ref[:, lo:hi])
        up = _dot(h, win_ref[:, d_ff + lo:d_ff + hi])
        act = (_silu(gate) * up).astype(BF16)
        acc = acc + _dot(act, wout_ref[lo:hi, :])
    o_ref[...] = acc


def _ffn_edges(d_ff, n_chunks):
    assert d_ff % MXU_DIM == 0
    tiles = d_ff // MXU_DIM
    return tuple(MXU_DIM * ((tiles * c + n_chunks - 1) // n_chunks) for c in range(n_chunks + 1))


def _post_mixer(xs, mixes, wo, wo_layer, g, win, wout, layer, tm, n_chunks):
    d = xs[0].shape[1]
    d_ff = wout.shape[1]
    res = _rowwise_call(
        functools.partial(_post_body, d_ff=d_ff, edges=_ffn_edges(d_ff, n_chunks)),
        [[x, a] for x, a in zip(xs, mixes)],
        [(wo, _resident_layer(wo, wo_layer)), (g, _resident((1, d))),
         (win, _resident_layer(win, layer)), (wout, _resident_layer(wout, layer))],
        [(d, F32, [1] * len(xs))], tm)
    return [r[0] for r in res]


def _mixer_body(p_ref, cw_ref, w2_ref, gb_ref, on_ref, conv0_ref, st0_ref,
                y_ref, convo_ref, sto_ref, ubuf, st_sc, *, tb, cdim, dk, dv):
    hk = dk // GLA_HEADS
    hv = dv // GLA_HEADS
    o_q = 3 * cdim
    o_k = o_q + dk
    o_v = o_k + dk
    o_g = o_v + dv
    o_l = o_g + dv
    t = pl.program_id(1)

    @pl.when(t == 0)
    def _():
        ubuf[0:8, :] = jnp.zeros((8, cdim), F32)
        ubuf[6:8, :] = conv0_ref[...]
        st_sc[...] = st0_ref[...]

    u = p_ref[:, 0:cdim] * p_ref[:, 2 * cdim:3 * cdim]
    ubuf[8:8 + tb, :] = u
    y = (cw_ref[0:1, :] * ubuf[6:6 + tb, :] + cw_ref[1:2, :] * ubuf[7:7 + tb, :]
         + cw_ref[2:3, :] * u)
    y_ref[:, 0:cdim] = (p_ref[:, cdim:2 * cdim] * y).astype(y_ref.dtype)
    tail = ubuf[tb + 6:tb + 8, :]
    ubuf[6:8, :] = tail
    convo_ref[...] = tail

    L = CHUNK
    assert hk == L and L & (L - 1) == 0
    sh = L.bit_length() - 1
    nst = GLA_HEADS * L
    r_i = lax.broadcasted_iota(jnp.int32, (nst, dk), 0)
    c_i = lax.broadcasted_iota(jnp.int32, (nst, dk), 1)
    same_head = (r_i >> sh) == (c_i >> sh)
    a_r = lax.broadcasted_iota(jnp.int32, (nst, nst), 0)
    a_c = lax.broadcasted_iota(jnp.int32, (nst, nst), 1)
    att_keep = ((a_r >> sh) == (a_c >> sh)) & ((a_c & (L - 1)) <= (a_r & (L - 1)))
    tri = (lax.broadcasted_iota(jnp.int32, (L, L), 1)
           <= lax.broadcasted_iota(jnp.int32, (L, L), 0)).astype(BF16)
    w2 = w2_ref[...]
    gb = gb_ref[...]
    onorm = on_ref[...]

    rows = [slice(c * L, (c + 1) * L) for c in range(tb // L)]

    def stack(a):
        return jnp.concatenate([a] * GLA_HEADS, axis=0)

    def heads_to_rows(o0, r):
        return jnp.concatenate([p_ref[r, o0 + h * hv:o0 + (h + 1) * hv] for h in range(GLA_HEADS)], axis=0)

    gk = _dot(p_ref[:, o_l:o_l + LANES].astype(BF16), w2) + gb
    la = (jnp.minimum(gk, 0.0) - jnp.log1p(jnp.exp(-jnp.abs(gk)))) * (1.0 / GATE_NORM)
    l_hi, l_mid, l_lo = _split3(la)
    b = [_dot(tri, l_hi[r]) + _dot(tri, l_mid[r]) + _dot(tri, l_lo[r]) for r in rows]
    b_last = [x[L - 1:L, :] for x in b]
    q_st, k_tl, kd_bd, v_st = [], [], [], []
    for c, r in enumerate(rows):
        q = p_ref[r, o_q:o_k] * (hk ** -0.5)
        k = p_ref[r, o_k:o_v]
        q_st.append(jnp.where(same_head, stack(q * jnp.exp(b[c])), 0.0).astype(BF16))
        k_tl.append(stack((k * jnp.exp(-b[c])).astype(BF16)))
        kd_bd.append(jnp.where(same_head, stack(k * jnp.exp(b_last[c] - b[c])), 0.0).astype(BF16))
        v_st.append(heads_to_rows(o_v, r).astype(BF16))
    att = [jnp.where(att_keep, _dot_nt(q_st[c], k_tl[c]), 0.0).astype(BF16)
           for c in range(len(rows))]
    upd = [_dot_tn(v_st[c], kd_bd[c]) for c in range(len(rows))]
    o = [_dot(att[c], v_st[c]) for c in range(len(rows))]
    st = st_sc[...]
    for c in range(len(rows)):
        o[c] = o[c] + _dot_nt(q_st[c], st.astype(BF16))
        st = st * jnp.exp(b_last[c]) + upd[c]
    for c, r in enumerate(rows):
        oc = _rms_rows(o[c], onorm) * _silu(heads_to_rows(o_g, r))
        for h in range(GLA_HEADS):
            y_ref[r, cdim + h * hv:cdim + (h + 1) * hv] = oc[h * L:(h + 1) * L, :].astype(y_ref.dtype)
    st_sc[...] = st
    sto_ref[...] = st


def _mixer_ab(proj, conv_w, w2pad, gb, onorm, conv0, st0, tb):
    bsz, t, width = proj.shape
    cdim = conv_w.shape[1]
    dk = w2pad.shape[1]
    hv = onorm.shape[1]
    dv = hv * GLA_HEADS
    return pl.pallas_call(
        functools.partial(_mixer_body, tb=tb, cdim=cdim, dk=dk, dv=dv),
        grid=(bsz, t // tb),
        in_specs=[
            pl.BlockSpec((None, tb, width), lambda b, i: (b, i, 0)),
            _resident(conv_w.shape),
            _resident(w2pad.shape),
            _resident(gb.shape),
            _resident(onorm.shape),
            pl.BlockSpec((None, CONV_WIDTH - 1, cdim), lambda b, i: (b, 0, 0)),
            pl.BlockSpec((None, hv, dk), lambda b, i: (b, 0, 0)),
        ],
        out_specs=[
            pl.BlockSpec((None, tb, cdim + dv), lambda b, i: (b, i, 0)),
            pl.BlockSpec((None, CONV_WIDTH - 1, cdim), lambda b, i: (b, 0, 0)),
            pl.BlockSpec((None, hv, dk), lambda b, i: (b, 0, 0)),
        ],
        out_shape=[
            jax.ShapeDtypeStruct((bsz, t, cdim + dv), BF16),
            jax.ShapeDtypeStruct((bsz, CONV_WIDTH - 1, cdim), F32),
            jax.ShapeDtypeStruct((bsz, hv, dk), F32),
        ],
        scratch_shapes=[
            pltpu.VMEM((tb + 8, cdim), F32),
            pltpu.VMEM((hv, dk), F32),
        ],
        compiler_params=_params("parallel", "arbitrary"),
    )(proj, conv_w, w2pad, gb, onorm, conv0, st0)


def _attn_body(q_ref, ka_ref, kb_ref, va_ref, vb_ref, brow_ref, o_ref, kw, vw, tab_ref,
               *, tq, tqc, hist_is_padding):
    nwin = BAND_ROWS + tqc

    @pl.when((pl.program_id(0) == 0) & (pl.program_id(1) == 0))
    def _build_table():
        wrow = brow_ref.shape[1]
        sh = CHUNK.bit_length() - 1
        rowi = lax.broadcasted_iota(jnp.int32, (tqc, wrow), 0)
        q_chunk = lax.broadcasted_iota(jnp.int32, (tqc, nwin), 0) >> sh
        k_chunk = (lax.broadcasted_iota(jnp.int32, (tqc, nwin), 1) >> sh) - BAND_ROWS // CHUNK
        allowed = (k_chunk <= q_chunk) & (k_chunk >= q_chunk - BAND_ROWS // CHUNK)

        def one_head(h, carry):
            y = jnp.broadcast_to(brow_ref[pl.ds(h, 1), :], (tqc, wrow))
            for bit in range(tqc.bit_length() - 1):
                y = jnp.where(((rowi >> bit) & 1) == 1, pltpu.roll(y, 1 << bit, axis=1), y)
            t = jnp.where(allowed, y[:, LANES:LANES + nwin], NEG_INF)
            tab_ref[h >> 1, pl.ds(pl.multiple_of((h & 1) * tqc, tqc), tqc), :] = t
            return carry

        lax.fori_loop(0, 2 * tab_ref.shape[0], one_head, 0)

    kw[0:BAND_ROWS, :] = ka_ref[...].astype(BF16)
    kw[BAND_ROWS:, :] = kb_ref[...]
    vw[0:BAND_ROWS, :] = va_ref[...].astype(BF16)
    vw[BAND_ROWS:, :] = vb_ref[...]
    n_pairs = q_ref.shape[1] // LANES
    lo = lax.broadcasted_iota(jnp.int32, (tqc, LANES), 1) < ATT_HEAD_DIM
    keep_q = ((lax.broadcasted_iota(jnp.int32, (2 * tqc, LANES), 1) < ATT_HEAD_DIM)
              == (lax.broadcasted_iota(jnp.int32, (2 * tqc, LANES), 0) < tqc))
    ones = jnp.ones((nwin, LANES), BF16)

    def run(first_block):
        def window(j):
            r0 = j * tqc
            c0 = max(BAND_ROWS - r0, 0) if first_block else 0
            return r0, c0, slice(r0 + c0, r0 + nwin)

        def scores(j, p):
            r0, c0, keys = window(j)
            lanes = slice(p * LANES, (p + 1) * LANES)
            qp = q_ref[r0:r0 + tqc, lanes]
            q2 = jnp.where(keep_q, jnp.concatenate([qp, qp], axis=0), 0.0)
            return _dot_nt(q2, kw[keys, lanes]) + tab_ref[p, :, c0:]

        def weighted_values(j, p, e):
            r0, c0, keys = window(j)
            lanes = slice(p * LANES, (p + 1) * LANES)
            pv = _dot(e, jnp.concatenate([vw[keys, lanes], ones[c0:]], axis=1))
            pv = pv[:, :LANES] / pv[:, LANES:]
            o_ref[r0:r0 + tqc, lanes] = jnp.where(lo, pv[:tqc], pv[tqc:]).astype(o_ref.dtype)

        for j in range(tq // tqc):
            s_next = scores(j, 0)
            pending = None
            for p in range(n_pairs):
                s = s_next
                if p + 1 < n_pairs:
                    s_next = scores(j, p + 1)
                e = jnp.exp(s - jnp.max(s, axis=-1, keepdims=True))
                if pending is not None:
                    weighted_values(*pending)
                pending = (j, p, e.astype(BF16))
            weighted_values(*pending)

    if hist_is_padding:
        first = pl.program_id(1) == 0
        pl.when(first)(lambda: run(True))
        pl.when(jnp.logical_not(first))(lambda: run(False))
    else:
        run(False)


def _band_attention(q, k_hist, k_cur, v_hist, v_cur, layer, brow, tq, tqc):
    bsz, t, width = q.shape
    assert t % tq == 0 and tq % tqc == 0
    hist_is_padding = k_hist is None
    if hist_is_padding:
        assert tq == BAND_ROWS
        k_hist, v_hist = k_cur, v_cur
        hist_spec = pl.BlockSpec((None, BAND_ROWS, width), lambda b, i: (b, jnp.maximum(i - 1, 0), 0))
    else:
        assert t == tq and k_hist.shape[2] == BAND_ROWS
        hist_spec = pl.BlockSpec((None, None, BAND_ROWS, width), lambda b, i: (layer, b, 0, 0))
    cur_map = lambda b, i: (b, i, 0)
    return pl.pallas_call(
        functools.partial(_attn_body, tq=tq, tqc=tqc, hist_is_padding=hist_is_padding),
        grid=(bsz, t // tq),
        in_specs=[
            pl.BlockSpec((None, tq, width), cur_map),
            hist_spec,
            pl.BlockSpec((None, tq, width), cur_map),
            hist_spec,
            pl.BlockSpec((None, tq, width), cur_map),
            _resident(brow.shape),
        ],
        out_specs=pl.BlockSpec((None, tq, width), cur_map),
        out_shape=jax.ShapeDtypeStruct((bsz, t, width), BF16),
        scratch_shapes=[
            pltpu.VMEM((BAND_ROWS + tq, width), BF16),
            pltpu.VMEM((BAND_ROWS + tq, width), BF16),
            pltpu.VMEM((width // LANES, 2 * tqc, BAND_ROWS + tqc), F32),
        ],
        compiler_params=_params("arbitrary", "arbitrary"),
    )(q, k_hist, k_cur, v_hist, v_cur, brow)


def _bias_row(rel_bias, tqc):
    width = -(-(LANES + BAND_ROWS + tqc) // LANES) * LANES
    rel = np.clip(BAND_ROWS + LANES - np.arange(width), -(CHUNK - 1), MAX_REL) + (CHUNK - 1)
    return rel_bias[:, rel].astype(F32)


class _Group:
    def __init__(self, x, conv_prev, gla_prev, k_hist, v_hist):
        self.bsz, self.t, d = x.shape
        self.xf = x.reshape(-1, d)
        self.conv_prev, self.gla_prev = conv_prev, gla_prev
        self.k_hist, self.v_hist = k_hist, v_hist
        self.conv_new, self.gla_new, self.k_new, self.v_new = [], [], [], []

    @property
    def fresh(self):
        return self.k_hist is None


def _trunk(groups, w):
    d = groups[0].xf.shape[1]
    depth = w["norm_mix"].shape[0]
    heads = w["rel_bias"].shape[1]
    att_dim = heads * ATT_HEAD_DIM
    tm = BAND_ROWS
    tail_divs = []
    for grp in groups:
        if grp.fresh:
            assert grp.t % tm == 0
            tail_divs.append(grp.t // tm)
        else:
            assert grp.bsz * grp.t == tm and grp.k_hist.shape[2] == BAND_ROWS and grp.t <= BAND_ROWS
            tail_divs.append(1)
            grp.kh = grp.k_hist.reshape(grp.k_hist.shape[:3] + (att_dim,))
            grp.vh = grp.v_hist.reshape(grp.v_hist.shape[:3] + (att_dim,))
    for layer in range(depth):
        g_mix = w["norm_mix"][layer][None, :]
        g_ffn = w["norm_ffn"][layer][None, :]
        xs = [grp.xf for grp in groups]
        mixes = []
        if layer % 2 == 0:
            e = layer // 2
            projs = _norm_proj(xs, g_mix, w["w_in_ab"], e, tm)
            for grp, proj in zip(groups, projs):
                st0 = grp.gla_prev[e].reshape(grp.bsz, -1, grp.gla_prev.shape[-1]).swapaxes(1, 2)
                mix, cs, st = _mixer_ab(
                    proj.reshape(grp.bsz, grp.t, -1), w["conv_w"][e], w["gk_w2"][e],
                    w["gk_b"][e][None, :], w["gla_onorm"][e][None, :], grp.conv_prev[e], st0,
                    min(BAND_ROWS, grp.t))
                grp.conv_new.append(cs)
                grp.gla_new.append(st.swapaxes(1, 2).reshape(grp.gla_prev.shape[1:]))
                mixes.append(mix.reshape(-1, mix.shape[-1]))
            w_o, w_o_layer = w["w_out_ab"], e
        else:
            o = layer // 2
            qkvs = _norm_qkv(xs, g_mix, w["w_qkv"], o, w["pool"], w["poolt"],
                             w["q_norm"][o], w["k_norm"][o], tm, tail_divs)
            for grp, (q, k, v, kt, vt) in zip(groups, qkvs):
                shp = (grp.bsz, grp.t, att_dim)
                tqc = min(2 * CHUNK, grp.t)
                att = _band_attention(
                    q.reshape(shp), None if grp.fresh else grp.kh, k.reshape(shp),
                    None if grp.fresh else grp.vh, v.reshape(shp), o,
                    _bias_row(w["rel_bias"][o], tqc), min(BAND_ROWS, grp.t), tqc)
                grp.k_new.append(kt)
                grp.v_new.append(vt)
                mixes.append(att.reshape(-1, att_dim))
            w_o, w_o_layer = w["w_o_att"], o
        outs = _post_mixer(xs, mixes, w_o, w_o_layer, g_ffn, w["w_ffn_in"], w["w_ffn_out"], layer, tm, 2)
        for grp, xf in zip(groups, outs):
            grp.xf = xf

    def cache(grp, tails, hist):
        new = jnp.stack(tails).reshape(len(tails), grp.bsz, -1, heads, ATT_HEAD_DIM)
        if grp.fresh:
            return new
        return jnp.concatenate([hist[:, :, grp.t:], new], axis=2)

    return [(grp.xf.reshape(grp.bsz, grp.t, d), jnp.stack(grp.conv_new), jnp.stack(grp.gla_new),
             cache(grp, grp.k_new, grp.k_hist), cache(grp, grp.v_new, grp.v_hist)) for grp in groups]


def kernel(x_prompt, x_sample, state_conv, state_gla, cache_k, cache_v, norm_mix, norm_ffn,
           w_in_ab, conv_w, gla_gk_w2, gla_gk_b, gla_onorm, w_out_ab, w_qkv, q_norm, k_norm,
           rel_bias, w_o_att, w_ffn_in, w_ffn_out):
    bsz = x_prompt.shape[0]
    n_even = state_conv.shape[0]
    n_odd = cache_k.shape[0]
    heads, head_dim = cache_k.shape[-2:]
    att_dim = heads * head_dim
    in_ab = w_in_ab.shape[-1]
    in_pad = -in_ab % LANES
    rank = gla_gk_w2.shape[1]
    pool = (np.arange(att_dim)[:, None] // head_dim == np.arange(LANES)[None, :])
    w = {
        "norm_mix": norm_mix, "norm_ffn": norm_ffn,
        "w_in_ab": jnp.pad(w_in_ab, ((0, 0), (0, 0), (0, in_pad))).astype(BF16),
        "conv_w": conv_w,
        "gk_w2": jnp.pad(gla_gk_w2, ((0, 0), (0, LANES - rank), (0, 0))).astype(BF16),
        "gk_b": gla_gk_b, "gla_onorm": gla_onorm,
        "w_out_ab": w_out_ab.astype(BF16),
        "w_qkv": w_qkv.astype(BF16),
        "q_norm": jnp.tile(q_norm, (1, heads))[:, None, :],
        "k_norm": jnp.tile(k_norm, (1, heads))[:, None, :],
        "rel_bias": rel_bias,
        "w_o_att": w_o_att.astype(BF16),
        "w_ffn_in": w_ffn_in.astype(BF16), "w_ffn_out": w_ffn_out.astype(BF16),
        "pool": jnp.asarray(pool, BF16), "poolt": jnp.asarray(pool.T, BF16),
    }
    conv0 = jnp.zeros((n_even, bsz) + state_conv.shape[2:], x_prompt.dtype)
    gla0 = jnp.zeros((n_even, bsz) + state_gla.shape[2:], F32)
    (y_p, conv_p, gla_p, k_p, v_p), (y_s, conv_s, gla_s, k_s, v_s) = _trunk(
        [_Group(x_prompt, conv0, gla0, None, None),
         _Group(x_sample, state_conv, state_gla, cache_k, cache_v)], w)
    return (y_p, y_s, conv_p, gla_p, k_p, v_p, conv_s, gla_s, k_s, v_s)
```

```python
import functools

import jax
import jax.numpy as jnp
import numpy as np
from jax import lax
from jax.experimental import pallas as pl
from jax.experimental.pallas import tpu as pltpu

F32 = jnp.float32
BF16 = jnp.bfloat16

EPS = 1e-6
NEG_INF = -1e30
CHUNK = 64
CONV_WIDTH = 3
GLA_HEADS = 4
GATE_NORM = 16.0
ATT_HEAD_DIM = 64
BAND_ROWS = 8 * CHUNK
MAX_REL = 256
LANES = 128
MXU_DIM = 256
VMEM_LIMIT = 56 * 1024 * 1024


def _params(*sem):
    return pltpu.CompilerParams(dimension_semantics=sem, vmem_limit_bytes=VMEM_LIMIT)


def _resident(shape):
    nd = len(shape)
    return pl.BlockSpec(shape, lambda *_: (0,) * nd, pipeline_mode=pl.Buffered(1))


def _resident_layer(stacked, layer):
    nd = stacked.ndim - 1
    return pl.BlockSpec((None,) + stacked.shape[1:], lambda *_: (layer,) + (0,) * nd,
                        pipeline_mode=pl.Buffered(1))


def _dot(a, b):
    return jnp.dot(a, b, preferred_element_type=F32)


def _dot_nt(a, b):
    return lax.dot_general(a, b, (((1,), (1,)), ((), ())), preferred_element_type=F32)


def _dot_tn(a, b):
    return lax.dot_general(a, b, (((0,), (0,)), ((), ())), preferred_element_type=F32)


def _split2(a):
    hi = a.astype(BF16)
    lo = (a - hi.astype(F32)).astype(BF16)
    return hi, lo


def _rms_rows(x, g):
    ms = jnp.mean(x * x, axis=-1, keepdims=True)
    return x * lax.rsqrt(ms + EPS) * g


def _silu(x):
    return x * jax.nn.sigmoid(x)


def _row_halves(ref):
    half = ref.shape[0] // 2
    return [slice(0, half), slice(half, 2 * half)]


def _proj_body(x_ref, g_ref, w_ref, o_ref):
    rows = _row_halves(x_ref)
    h = [_rms_rows(x_ref[r, :], g_ref[...]).astype(BF16) for r in rows]
    for r, hr in zip(rows, h):
        o_ref[r, :] = _dot(hr, w_ref[...])


def _rowwise_call(body, row_groups, shared, outs, tm):
    n_groups, n_in, n_out = len(row_groups), len(row_groups[0]), len(outs)
    nblk = [grp[0].shape[0] // tm for grp in row_groups]
    start = [sum(nblk[:g]) for g in range(n_groups)]

    def local(g):
        return lambda i: jnp.clip(i - start[g], 0, nblk[g] - 1)

    in_specs, operands, out_specs, out_shape = [], [], [], []
    for g, grp in enumerate(row_groups):
        for a in grp:
            assert a.shape[0] == nblk[g] * tm
            in_specs.append(pl.BlockSpec((tm, a.shape[1]), lambda i, f=local(g): (f(i), 0)))
            operands.append(a)
    for a, spec in shared:
        in_specs.append(spec)
        operands.append(a)
    for g in range(n_groups):
        for cols, dtype, keep in outs:
            assert nblk[g] % keep[g] == 0
            out_specs.append(pl.BlockSpec((tm, cols), lambda i, f=local(g), k=keep[g]: (f(i) // k, 0)))
            out_shape.append(jax.ShapeDtypeStruct((nblk[g] // keep[g] * tm, cols), dtype))

    def per_group(*refs):
        ins, sh = refs[:n_groups * n_in], refs[n_groups * n_in:n_groups * n_in + len(shared)]
        out_refs = refs[n_groups * n_in + len(shared):]
        i = pl.program_id(0)
        for g in range(n_groups):
            pl.when((i >= start[g]) & (i < start[g] + nblk[g]))(functools.partial(
                body, *ins[g * n_in:(g + 1) * n_in], *sh, *out_refs[g * n_out:(g + 1) * n_out]))

    res = pl.pallas_call(
        per_group,
        grid=(sum(nblk),),
        in_specs=in_specs,
        out_specs=out_specs,
        out_shape=out_shape,
        compiler_params=_params("arbitrary"),
    )(*operands)
    return [res[g * n_out:(g + 1) * n_out] for g in range(n_groups)]


def _norm_proj(xs, g, w, layer, tm):
    d = xs[0].shape[1]
    res = _rowwise_call(
        _proj_body, [[x] for x in xs],
        [(g, _resident((1, d))), (w, _resident_layer(w, layer))],
        [(w.shape[2], F32, [1] * len(xs))], tm)
    return [r[0] for r in res]


def _qkv_body(x_ref, g_ref, w_ref, pool_ref, poolt_ref, gq_ref, gk_ref,
              q_ref, k_ref, v_ref, kt_ref, vt_ref, *, att_dim, q_scale):
    def head_norm(a, gt):
        ms = _dot((a * a).astype(BF16), pool_ref[...]) * (1.0 / ATT_HEAD_DIM)
        rb = _dot(lax.rsqrt(ms + EPS).astype(BF16), poolt_ref[...])
        return a * rb * gt

    rows = _row_halves(x_ref)
    h = [_rms_rows(x_ref[r, :], g_ref[...]).astype(BF16) for r in rows]
    qkv = [_dot(hr, w_ref[...]) for hr in h]
    for r, y in zip(rows, qkv):
        q_ref[r, :] = (head_norm(y[:, :att_dim], gq_ref[...]) * q_scale).astype(BF16)
        kn = head_norm(y[:, att_dim:2 * att_dim], gk_ref[...])
        v = y[:, 2 * att_dim:]
        k_ref[r, :] = kn.astype(BF16)
        v_ref[r, :] = v.astype(BF16)
        kt_ref[r, :] = kn
        vt_ref[r, :] = v


def _norm_qkv(xs, g, w, layer, pool, poolt, gq, gk, tm, tail_divs):
    d = xs[0].shape[1]
    att_dim = w.shape[2] // 3
    every = [1] * len(xs)
    return _rowwise_call(
        functools.partial(_qkv_body, att_dim=att_dim, q_scale=ATT_HEAD_DIM ** -0.5),
        [[x] for x in xs],
        [(g, _resident((1, d))), (w, _resident_layer(w, layer)), (pool, _resident(pool.shape)),
         (poolt, _resident(poolt.shape)), (gq, _resident((1, att_dim))), (gk, _resident((1, att_dim)))],
        [(att_dim, BF16, every)] * 3 + [(att_dim, F32, tail_divs)] * 2, tm)


def _post_body(x_ref, a_ref, wo_ref, g_ref, win_ref, wout_ref, o_ref, *, d_ff, edges):
    rows = _row_halves(x_ref)
    acc = [x_ref[r, :] + _dot(a_ref[r, :], wo_ref[...]) for r in rows]
    h = [_rms_rows(x1, g_ref[...]).astype(BF16) for x1 in acc]
    for lo, hi in zip(edges[:-1], edges[1:]):
        for i in range(len(rows)):
            gate = _dot(h[i], win_ref[:, lo:hi])
            up = _dot(h[i], win_ref[:, d_ff + lo:d_ff + hi])
            act = (_silu(gate) * up).astype(BF16)
            acc[i] = acc[i] + _dot(act, wout_ref[lo:hi, :])
    for i, r in enumerate(rows):
        o_ref[r, :] = acc[i]


def _ffn_edges(d_ff, n_chunks):
    assert d_ff % MXU_DIM == 0
    tiles = d_ff // MXU_DIM
    return tuple(MXU_DIM * ((tiles * c + n_chunks - 1) // n_chunks) for c in range(n_chunks + 1))


def _post_mixer(xs, mixes, wo, wo_layer, g, win, wout, layer, tm, n_chunks):
    d = xs[0].shape[1]
    d_ff = wout.shape[1]
    res = _rowwise_call(
        functools.partial(_post_body, d_ff=d_ff, edges=_ffn_edges(d_ff, n_chunks)),
        [[x, a] for x, a in zip(xs, mixes)],
        [(wo, _resident_layer(wo, wo_layer)), (g, _resident((1, d))),
         (win, _resident_layer(win, layer)), (wout, _resident_layer(wout, layer))],
        [(d, F32, [1] * len(xs))], tm)
    return [r[0] for r in res]


def _mixer_body(p_ref, cw_ref, w2_ref, gb_ref, on_ref, conv0_ref, st0_ref,
                y_ref, convo_ref, sto_ref, ubuf, st_sc, *, tb, cdim, dk, dv):
    hk = dk // GLA_HEADS
    hv = dv // GLA_HEADS
    o_q = 3 * cdim
    o_k = o_q + dk
    o_v = o_k + dk
    o_g = o_v + dv
    o_l = o_g + dv
    t = pl.program_id(1)

    @pl.when(t == 0)
    def _():
        ubuf[0:8, :] = jnp.zeros((8, cdim), F32)
        ubuf[6:8, :] = conv0_ref[...]
        st_sc[...] = st0_ref[...]

    u = p_ref[:, 0:cdim] * p_ref[:, 2 * cdim:3 * cdim]
    ubuf[8:8 + tb, :] = u
    y = (cw_ref[0:1, :] * ubuf[6:6 + tb, :] + cw_ref[1:2, :] * ubuf[7:7 + tb, :]
         + cw_ref[2:3, :] * u)
    y_ref[:, 0:cdim] = (p_ref[:, cdim:2 * cdim] * y).astype(y_ref.dtype)
    tail = ubuf[tb + 6:tb + 8, :]
    ubuf[6:8, :] = tail
    convo_ref[...] = tail

    L = CHUNK
    assert hk == L and L & (L - 1) == 0
    sh = L.bit_length() - 1
    nst = GLA_HEADS * L
    r_i = lax.broadcasted_iota(jnp.int32, (nst, dk), 0)
    c_i = lax.broadcasted_iota(jnp.int32, (nst, dk), 1)
    same_head = (r_i >> sh) == (c_i >> sh)
    a_r = lax.broadcasted_iota(jnp.int32, (nst, nst), 0)
    a_c = lax.broadcasted_iota(jnp.int32, (nst, nst), 1)
    att_keep = ((a_r >> sh) == (a_c >> sh)) & ((a_c & (L - 1)) <= (a_r & (L - 1)))
    tri = (lax.broadcasted_iota(jnp.int32, (L, L), 1)
           <= lax.broadcasted_iota(jnp.int32, (L, L), 0)).astype(BF16)
    w2 = w2_ref[...]
    gb = gb_ref[...]
    onorm = on_ref[...]

    rows = [slice(c * L, (c + 1) * L) for c in range(tb // L)]

    def stack(a):
        return jnp.concatenate([a] * GLA_HEADS, axis=0)

    def heads_to_rows(o0, r):
        return jnp.concatenate([p_ref[r, o0 + h * hv:o0 + (h + 1) * hv] for h in range(GLA_HEADS)], axis=0)

    gk = _dot(p_ref[:, o_l:o_l + LANES].astype(BF16), w2) + gb
    la = (jnp.minimum(gk, 0.0) - jnp.log(1.0 + jnp.exp(-jnp.abs(gk)))) * (1.0 / GATE_NORM)
    l_hi, l_lo = _split2(la)
    b = [_dot(tri, l_hi[r]) + _dot(tri, l_lo[r]) for r in rows]
    b_last = [x[L - 1:L, :] for x in b]
    q_st, k_tl, kd_bd, v_st = [], [], [], []
    for c, r in enumerate(rows):
        q = p_ref[r, o_q:o_k] * (hk ** -0.5)
        k = p_ref[r, o_k:o_v]
        q_st.append(jnp.where(same_head, stack(q * jnp.exp(b[c])), 0.0).astype(BF16))
        k_tl.append(stack((k * jnp.exp(-b[c])).astype(BF16)))
        kd_bd.append(jnp.where(same_head, stack(k * jnp.exp(b_last[c] - b[c])), 0.0).astype(BF16))
        v_st.append(heads_to_rows(o_v, r).astype(BF16))
    att = [jnp.where(att_keep, _dot_nt(q_st[c], k_tl[c]), 0.0).astype(BF16)
           for c in range(len(rows))]
    upd = [_dot_tn(v_st[c], kd_bd[c]) for c in range(len(rows))]
    o = [_dot(att[c], v_st[c]) for c in range(len(rows))]
    st = st_sc[...]
    for c in range(len(rows)):
        o[c] = o[c] + _dot_nt(q_st[c], st.astype(BF16))
        st = st * jnp.exp(b_last[c]) + upd[c]
    for c, r in enumerate(rows):
        oc = _rms_rows(o[c], onorm) * _silu(heads_to_rows(o_g, r))
        for h in range(GLA_HEADS):
            y_ref[r, cdim + h * hv:cdim + (h + 1) * hv] = oc[h * L:(h + 1) * L, :].astype(y_ref.dtype)
    st_sc[...] = st
    sto_ref[...] = st


def _mixer_ab(proj, conv_w, w2pad, gb, onorm, conv0, st0, tb):
    bsz, t, width = proj.shape
    cdim = conv_w.shape[1]
    dk = w2pad.shape[1]
    hv = onorm.shape[1]
    dv = hv * GLA_HEADS
    return pl.pallas_call(
        functools.partial(_mixer_body, tb=tb, cdim=cdim, dk=dk, dv=dv),
        grid=(bsz, t // tb),
        in_specs=[
            pl.BlockSpec((None, tb, width), lambda b, i: (b, i, 0)),
            _resident(conv_w.shape),
            _resident(w2pad.shape),
            _resident(gb.shape),
            _resident(onorm.shape),
            pl.BlockSpec((None, CONV_WIDTH - 1, cdim), lambda b, i: (b, 0, 0)),
            pl.BlockSpec((None, hv, dk), lambda b, i: (b, 0, 0)),
        ],
        out_specs=[
            pl.BlockSpec((None, tb, cdim + dv), lambda b, i: (b, i, 0)),
            pl.BlockSpec((None, CONV_WIDTH - 1, cdim), lambda b, i: (b, 0, 0)),
            pl.BlockSpec((None, hv, dk), lambda b, i: (b, 0, 0)),
        ],
        out_shape=[
            jax.ShapeDtypeStruct((bsz, t, cdim + dv), BF16),
            jax.ShapeDtypeStruct((bsz, CONV_WIDTH - 1, cdim), F32),
            jax.ShapeDtypeStruct((bsz, hv, dk), F32),
        ],
        scratch_shapes=[
            pltpu.VMEM((tb + 8, cdim), F32),
            pltpu.VMEM((hv, dk), F32),
        ],
        compiler_params=_params("parallel", "arbitrary"),
    )(proj, conv_w, w2pad, gb, onorm, conv0, st0)


def _attn_body(q_ref, ka_ref, kb_ref, va_ref, vb_ref, brow_ref, o_ref, kw, vw, tab_ref,
               *, tq, tqc, hist_is_padding):
    nwin = BAND_ROWS + tqc

    @pl.when((pl.program_id(0) == 0) & (pl.program_id(1) == 0))
    def _build_table():
        wrow = brow_ref.shape[1]
        sh = CHUNK.bit_length() - 1
        rowi = lax.broadcasted_iota(jnp.int32, (tqc, wrow), 0)
        q_chunk = lax.broadcasted_iota(jnp.int32, (tqc, nwin), 0) >> sh
        k_chunk = (lax.broadcasted_iota(jnp.int32, (tqc, nwin), 1) >> sh) - BAND_ROWS // CHUNK
        allowed = (k_chunk <= q_chunk) & (k_chunk >= q_chunk - BAND_ROWS // CHUNK)

        def one_head(h, carry):
            y = jnp.broadcast_to(brow_ref[pl.ds(h, 1), :], (tqc, wrow))
            for bit in range(tqc.bit_length() - 1):
                y = jnp.where(((rowi >> bit) & 1) == 1, pltpu.roll(y, 1 << bit, axis=1), y)
            t = jnp.where(allowed, y[:, LANES:LANES + nwin], NEG_INF)
            tab_ref[h >> 1, pl.ds(pl.multiple_of((h & 1) * tqc, tqc), tqc), :] = t
            return carry

        lax.fori_loop(0, 2 * tab_ref.shape[0], one_head, 0)

    kw[0:BAND_ROWS, :] = ka_ref[...].astype(BF16)
    kw[BAND_ROWS:, :] = kb_ref[...]
    vw[0:BAND_ROWS, :] = va_ref[...].astype(BF16)
    vw[BAND_ROWS:, :] = vb_ref[...]
    n_pairs = q_ref.shape[1] // LANES
    lo = lax.broadcasted_iota(jnp.int32, (tqc, LANES), 1) < ATT_HEAD_DIM
    keep_q = ((lax.broadcasted_iota(jnp.int32, (2 * tqc, LANES), 1) < ATT_HEAD_DIM)
              == (lax.broadcasted_iota(jnp.int32, (2 * tqc, LANES), 0) < tqc))
    ones = jnp.ones((nwin, LANES), BF16)

    def run(first_block):
        def window(j):
            r0 = j * tqc
            c0 = max(BAND_ROWS - r0, 0) if first_block else 0
            return r0, c0, slice(r0 + c0, r0 + nwin)

        def scores(j, p):
            r0, c0, keys = window(j)
            lanes = slice(p * LANES, (p + 1) * LANES)
            qp = q_ref[r0:r0 + tqc, lanes]
            q2 = jnp.where(keep_q, jnp.concatenate([qp, qp], axis=0), 0.0)
            return _dot_nt(q2, kw[keys, lanes]) + tab_ref[p, :, c0:]

        def weighted_values(j, p, e):
            r0, c0, keys = window(j)
            lanes = slice(p * LANES, (p + 1) * LANES)
            pv = _dot(e, jnp.concatenate([vw[keys, lanes], ones[c0:]], axis=1))
            pv = pv[:, :LANES] / pv[:, LANES:]
            o_ref[r0:r0 + tqc, lanes] = jnp.where(lo, pv[:tqc], pv[tqc:]).astype(o_ref.dtype)

        for j in range(tq // tqc):
            s_next = scores(j, 0)
            pending = None
            for p in range(n_pairs):
                s = s_next
                if p + 1 < n_pairs:
                    s_next = scores(j, p + 1)
                e = jnp.exp((s - jnp.max(s, axis=-1, keepdims=True)).astype(BF16))
                if pending is not None:
                    weighted_values(*pending)
                pending = (j, p, e)
            weighted_values(*pending)

    if hist_is_padding:
        first = pl.program_id(1) == 0
        pl.when(first)(lambda: run(True))
        pl.when(jnp.logical_not(first))(lambda: run(False))
    else:
        run(False)


def _band_attention(q, k_hist, k_cur, v_hist, v_cur, layer, brow, tq, tqc):
    bsz, t, width = q.shape
    assert t % tq == 0 and tq % tqc == 0
    hist_is_padding = k_hist is None
    if hist_is_padding:
        assert tq == BAND_ROWS
        k_hist, v_hist = k_cur, v_cur
        hist_spec = pl.BlockSpec((None, BAND_ROWS, width), lambda b, i: (b, jnp.maximum(i - 1, 0), 0))
    else:
        assert t == tq and k_hist.shape[2] == BAND_ROWS
        hist_spec = pl.BlockSpec((None, None, BAND_ROWS, width), lambda b, i: (layer, b, 0, 0))
    cur_map = lambda b, i: (b, i, 0)
    return pl.pallas_call(
        functools.partial(_attn_body, tq=tq, tqc=tqc, hist_is_padding=hist_is_padding),
        grid=(bsz, t // tq),
        in_specs=[
            pl.BlockSpec((None, tq, width), cur_map),
            hist_spec,
            pl.BlockSpec((None, tq, width), cur_map),
            hist_spec,
            pl.BlockSpec((None, tq, width), cur_map),
            _resident(brow.shape),
        ],
        out_specs=pl.BlockSpec((None, tq, width), cur_map),
        out_shape=jax.ShapeDtypeStruct((bsz, t, width), BF16),
        scratch_shapes=[
            pltpu.VMEM((BAND_ROWS + tq, width), BF16),
            pltpu.VMEM((BAND_ROWS + tq, width), BF16),
            pltpu.VMEM((width // LANES, 2 * tqc, BAND_ROWS + tqc), F32),
        ],
        compiler_params=_params("arbitrary", "arbitrary"),
    )(q, k_hist, k_cur, v_hist, v_cur, brow)


def _bias_row(rel_bias, tqc):
    width = -(-(LANES + BAND_ROWS + tqc) // LANES) * LANES
    rel = np.clip(BAND_ROWS + LANES - np.arange(width), -(CHUNK - 1), MAX_REL) + (CHUNK - 1)
    return rel_bias[:, rel].astype(F32)


class _Group:
    def __init__(self, x, conv_prev, gla_prev, k_hist, v_hist):
        self.bsz, self.t, d = x.shape
        self.xf = x.reshape(-1, d)
        self.conv_prev, self.gla_prev = conv_prev, gla_prev
        self.k_hist, self.v_hist = k_hist, v_hist
        self.conv_new, self.gla_new, self.k_new, self.v_new = [], [], [], []

    @property
    def fresh(self):
        return self.k_hist is None


def _trunk(groups, w):
    d = groups[0].xf.shape[1]
    depth = w["norm_mix"].shape[0]
    heads = w["rel_bias"].shape[1]
    att_dim = heads * ATT_HEAD_DIM
    tm = BAND_ROWS
    tail_divs = []
    for grp in groups:
        if grp.fresh:
            assert grp.t % tm == 0
            tail_divs.append(grp.t // tm)
        else:
            assert grp.bsz * grp.t == tm and grp.k_hist.shape[2] == BAND_ROWS and grp.t <= BAND_ROWS
            tail_divs.append(1)
            grp.kh = grp.k_hist.reshape(grp.k_hist.shape[:3] + (att_dim,))
            grp.vh = grp.v_hist.reshape(grp.v_hist.shape[:3] + (att_dim,))
    for layer in range(depth):
        g_mix = w["norm_mix"][layer][None, :]
        g_ffn = w["norm_ffn"][layer][None, :]
        xs = [grp.xf for grp in groups]
        mixes = []
        if layer % 2 == 0:
            e = layer // 2
            projs = _norm_proj(xs, g_mix, w["w_in_ab"], e, tm)
            for grp, proj in zip(groups, projs):
                st0 = grp.gla_prev[e].reshape(grp.bsz, -1, grp.gla_prev.shape[-1]).swapaxes(1, 2)
                mix, cs, st = _mixer_ab(
                    proj.reshape(grp.bsz, grp.t, -1), w["conv_w"][e], w["gk_w2"][e],
                    w["gk_b"][e][None, :], w["gla_onorm"][e][None, :], grp.conv_prev[e], st0,
                    min(BAND_ROWS, grp.t))
                grp.conv_new.append(cs)
                grp.gla_new.append(st.swapaxes(1, 2).reshape(grp.gla_prev.shape[1:]))
                mixes.append(mix.reshape(-1, mix.shape[-1]))
            w_o, w_o_layer = w["w_out_ab"], e
        else:
            o = layer // 2
            qkvs = _norm_qkv(xs, g_mix, w["w_qkv"], o, w["pool"], w["poolt"],
                             w["q_norm"][o], w["k_norm"][o], tm, tail_divs)
            for grp, (q, k, v, kt, vt) in zip(groups, qkvs):
                shp = (grp.bsz, grp.t, att_dim)
                tqc = min(2 * CHUNK, grp.t)
                att = _band_attention(
                    q.reshape(shp), None if grp.fresh else grp.kh, k.reshape(shp),
                    None if grp.fresh else grp.vh, v.reshape(shp), o,
                    _bias_row(w["rel_bias"][o], tqc), min(BAND_ROWS, grp.t), tqc)
                grp.k_new.append(kt)
                grp.v_new.append(vt)
                mixes.append(att.reshape(-1, att_dim))
            w_o, w_o_layer = w["w_o_att"], o
        outs = _post_mixer(xs, mixes, w_o, w_o_layer, g_ffn, w["w_ffn_in"], w["w_ffn_out"], layer, tm, 2)
        for grp, xf in zip(groups, outs):
            grp.xf = xf

    def cache(grp, tails, hist):
        new = jnp.stack(tails).reshape(len(tails), grp.bsz, -1, heads, ATT_HEAD_DIM)
        if grp.fresh:
            return new
        return jnp.concatenate([hist[:, :, grp.t:], new], axis=2)

    return [(grp.xf.reshape(grp.bsz, grp.t, d), jnp.stack(grp.conv_new), jnp.stack(grp.gla_new),
             cache(grp, grp.k_new, grp.k_hist), cache(grp, grp.v_new, grp.v_hist)) for grp in groups]


def kernel(x_prompt, x_sample, state_conv, state_gla, cache_k, cache_v, norm_mix, norm_ffn,
           w_in_ab, conv_w, gla_gk_w2, gla_gk_b, gla_onorm, w_out_ab, w_qkv, q_norm, k_norm,
           rel_bias, w_o_att, w_ffn_in, w_ffn_out):
    bsz = x_prompt.shape[0]
    n_even = state_conv.shape[0]
    n_odd = cache_k.shape[0]
    heads, head_dim = cache_k.shape[-2:]
    att_dim = heads * head_dim
    in_ab = w_in_ab.shape[-1]
    in_pad = -in_ab % LANES
    rank = gla_gk_w2.shape[1]
    pool = (np.arange(att_dim)[:, None] // head_dim == np.arange(LANES)[None, :])
    w = {
        "norm_mix": norm_mix, "norm_ffn": norm_ffn,
        "w_in_ab": jnp.pad(w_in_ab, ((0, 0), (0, 0), (0, in_pad))).astype(BF16),
        "conv_w": conv_w,
        "gk_w2": jnp.pad(gla_gk_w2, ((0, 0), (0, LANES - rank), (0, 0))).astype(BF16),
        "gk_b": gla_gk_b, "gla_onorm": gla_onorm,
        "w_out_ab": w_out_ab.astype(BF16),
        "w_qkv": w_qkv.astype(BF16),
        "q_norm": jnp.tile(q_norm, (1, heads))[:, None, :],
        "k_norm": jnp.tile(k_norm, (1, heads))[:, None, :],
        "rel_bias": rel_bias,
        "w_o_att": w_o_att.astype(BF16),
        "w_ffn_in": w_ffn_in.astype(BF16), "w_ffn_out": w_ffn_out.astype(BF16),
        "pool": jnp.asarray(pool, BF16), "poolt": jnp.asarray(pool.T, BF16),
    }
    conv0 = jnp.zeros((n_even, bsz) + state_conv.shape[2:], x_prompt.dtype)
    gla0 = jnp.zeros((n_even, bsz) + state_gla.shape[2:], F32)
    (y_p, conv_p, gla_p, k_p, v_p), (y_s, conv_s, gla_s, k_s, v_s) = _trunk(
        [_Group(x_prompt, conv0, gla0, None, None),
         _Group(x_sample, state_conv, state_gla, cache_k, cache_v)], w)
    return (y_p, y_s, conv_p, gla_p, k_p, v_p, conv_s, gla_s, k_s, v_s)
```

```python
import functools

import jax
import jax.numpy as jnp
import numpy as np
from jax import lax
from jax.experimental import pallas as pl
from jax.experimental.pallas import tpu as pltpu

F32 = jnp.float32
BF16 = jnp.bfloat16

EPS = 1e-6
NEG_INF = -1e30
CHUNK = 64
CONV_WIDTH = 3
GLA_HEADS = 4
GATE_NORM = 16.0
ATT_HEAD_DIM = 64
BAND_ROWS = 8 * CHUNK
MAX_REL = 256
LANES = 128
SUBLANES = 8
MXU_DIM = 256
VMEM_LIMIT = 56 * 1024 * 1024


def _params(*sem):
    return pltpu.CompilerParams(dimension_semantics=sem, vmem_limit_bytes=VMEM_LIMIT)


def _resident(shape):
    nd = len(shape)
    return pl.BlockSpec(shape, lambda *_: (0,) * nd, pipeline_mode=pl.Buffered(1))


def _resident_layer(stacked, layer):
    nd = stacked.ndim - 1
    return pl.BlockSpec((None,) + stacked.shape[1:], lambda *_: (layer,) + (0,) * nd,
                        pipeline_mode=pl.Buffered(1))


def _dot(a, b):
    return jnp.dot(a, b, preferred_element_type=F32)


def _dot_nt(a, b):
    return lax.dot_general(a, b, (((1,), (1,)), ((), ())), preferred_element_type=F32)


def _dot_tn(a, b):
    return lax.dot_general(a, b, (((0,), (0,)), ((), ())), preferred_element_type=F32)


def _split2(a):
    hi = a.astype(BF16)
    lo = (a - hi.astype(F32)).astype(BF16)
    return hi, lo


def _rms_rows(x, g):
    ms = jnp.mean(x * x, axis=-1, keepdims=True)
    return x * lax.rsqrt(ms + EPS) * g


def _silu(x):
    return x * jax.nn.sigmoid(x)


def _row_halves(ref):
    half = ref.shape[0] // 2
    return [slice(0, half), slice(half, 2 * half)]


def _proj_body(x_ref, g_ref, w_ref, oa_ref, ob_ref):
    split = oa_ref.shape[1]
    rows = _row_halves(x_ref)
    h = [_rms_rows(x_ref[r, :], g_ref[...]).astype(BF16) for r in rows]
    for r, hr in zip(rows, h):
        y = _dot(hr, w_ref[...])
        oa_ref[r, :] = y[:, :split]
        ob_ref[r, :] = y[:, split:].astype(ob_ref.dtype)


def _rowwise_call(body, row_groups, shared, outs, tm):
    n_groups, n_in, n_out = len(row_groups), len(row_groups[0]), len(outs)
    nblk = [grp[0].shape[0] // tm for grp in row_groups]
    start = [sum(nblk[:g]) for g in range(n_groups)]

    def local(g):
        return lambda i: jnp.clip(i - start[g], 0, nblk[g] - 1)

    in_specs, operands, out_specs, out_shape = [], [], [], []
    for g, grp in enumerate(row_groups):
        for a in grp:
            assert a.shape[0] == nblk[g] * tm
            in_specs.append(pl.BlockSpec((tm, a.shape[1]), lambda i, f=local(g): (f(i), 0)))
            operands.append(a)
    for a, spec in shared:
        in_specs.append(spec)
        operands.append(a)
    for g in range(n_groups):
        for cols, dtype, keep in outs:
            assert nblk[g] % keep[g] == 0
            out_specs.append(pl.BlockSpec((tm, cols), lambda i, f=local(g), k=keep[g]: (f(i) // k, 0)))
            out_shape.append(jax.ShapeDtypeStruct((nblk[g] // keep[g] * tm, cols), dtype))

    def per_group(*refs):
        ins, sh = refs[:n_groups * n_in], refs[n_groups * n_in:n_groups * n_in + len(shared)]
        out_refs = refs[n_groups * n_in + len(shared):]
        i = pl.program_id(0)
        for g in range(n_groups):
            pl.when((i >= start[g]) & (i < start[g] + nblk[g]))(functools.partial(
                body, *ins[g * n_in:(g + 1) * n_in], *sh, *out_refs[g * n_out:(g + 1) * n_out]))

    res = pl.pallas_call(
        per_group,
        grid=(sum(nblk),),
        in_specs=in_specs,
        out_specs=out_specs,
        out_shape=out_shape,
        compiler_params=_params("arbitrary"),
    )(*operands)
    return [res[g * n_out:(g + 1) * n_out] for g in range(n_groups)]


def _norm_proj(xs, g, w, layer, tm, split):
    d = xs[0].shape[1]
    every = [1] * len(xs)
    return _rowwise_call(
        _proj_body, [[x] for x in xs],
        [(g, _resident((1, d))), (w, _resident_layer(w, layer))],
        [(split, F32, every), (w.shape[2] - split, BF16, every)], tm)


def _qkv_body(x_ref, g_ref, w_ref, pool_ref, poolt_ref, gq_ref, gk_ref,
              q_ref, k_ref, v_ref, kt_ref, vt_ref, *, att_dim, q_scale):
    def head_norm(a, gt):
        ms = _dot((a * a).astype(BF16), pool_ref[...]) * (1.0 / ATT_HEAD_DIM)
        rb = _dot(lax.rsqrt(ms + EPS).astype(BF16), poolt_ref[...])
        return a * rb * gt

    rows = _row_halves(x_ref)
    h = [_rms_rows(x_ref[r, :], g_ref[...]).astype(BF16) for r in rows]
    qkv = [_dot(hr, w_ref[...]) for hr in h]
    for r, y in zip(rows, qkv):
        q_ref[r, :] = (head_norm(y[:, :att_dim], gq_ref[...]) * q_scale).astype(BF16)
        kn = head_norm(y[:, att_dim:2 * att_dim], gk_ref[...])
        v = y[:, 2 * att_dim:]
        k_ref[r, :] = kn.astype(BF16)
        v_ref[r, :] = v.astype(BF16)
        kt_ref[r, :] = kn
        vt_ref[r, :] = v


def _norm_qkv(xs, g, w, layer, pool, poolt, gq, gk, tm, tail_divs):
    d = xs[0].shape[1]
    att_dim = w.shape[2] // 3
    every = [1] * len(xs)
    return _rowwise_call(
        functools.partial(_qkv_body, att_dim=att_dim, q_scale=ATT_HEAD_DIM ** -0.5),
        [[x] for x in xs],
        [(g, _resident((1, d))), (w, _resident_layer(w, layer)), (pool, _resident(pool.shape)),
         (poolt, _resident(poolt.shape)), (gq, _resident((1, att_dim))), (gk, _resident((1, att_dim)))],
        [(att_dim, BF16, every)] * 3 + [(att_dim, F32, tail_divs)] * 2, tm)


def _post_body(x_ref, a_ref, wo_ref, g_ref, win_ref, wout_ref, o_ref, *, d_ff, edges):
    rows = _row_halves(x_ref)
    acc = [x_ref[r, :] + _dot(a_ref[r, :], wo_ref[...]) for r in rows]
    h = [_rms_rows(x1, g_ref[...]).astype(BF16) for x1 in acc]
    for lo, hi in zip(edges[:-1], edges[1:]):
        for i in range(len(rows)):
            gate = _dot(h[i], win_ref[:, lo:hi])
            up = _dot(h[i], win_ref[:, d_ff + lo:d_ff + hi])
            act = (_silu(gate) * up).astype(BF16)
            acc[i] = acc[i] + _dot(act, wout_ref[lo:hi, :])
    for i, r in enumerate(rows):
        o_ref[r, :] = acc[i]


def _ffn_edges(d_ff, n_chunks):
    assert d_ff % MXU_DIM == 0
    tiles = d_ff // MXU_DIM
    return tuple(MXU_DIM * ((tiles * c + n_chunks - 1) // n_chunks) for c in range(n_chunks + 1))


def _post_mixer(xs, mixes, wo, wo_layer, g, win, wout, layer, tm, n_chunks):
    d = xs[0].shape[1]
    d_ff = wout.shape[1]
    res = _rowwise_call(
        functools.partial(_post_body, d_ff=d_ff, edges=_ffn_edges(d_ff, n_chunks)),
        [[x, a] for x, a in zip(xs, mixes)],
        [(wo, _resident_layer(wo, wo_layer)), (g, _resident((1, d))),
         (win, _resident_layer(win, layer)), (wout, _resident_layer(wout, layer))],
        [(d, F32, [1] * len(xs))], tm)
    return [r[0] for r in res]


def _mixer_body(pa_ref, p_ref, cw_ref, w2_ref, gb_ref, on_ref, conv0_ref, st0_ref,
                y_ref, convo_ref, sto_ref, ubuf, st_sc, *, tb, cdim, dk, dv):
    hk = dk // GLA_HEADS
    hv = dv // GLA_HEADS
    o_q = 0
    o_k = o_q + dk
    o_v = o_k + dk
    o_g = o_v + dv
    o_l = o_g + dv
    t = pl.program_id(1)

    @pl.when(t == 0)
    def _():
        ubuf[0:8, :] = jnp.zeros((8, cdim), F32)
        ubuf[6:8, :] = conv0_ref[...]
        st_sc[...] = st0_ref[...]

    u = pa_ref[:, 0:cdim] * pa_ref[:, 2 * cdim:3 * cdim]
    ubuf[8:8 + tb, :] = u
    y = (cw_ref[0:1, :] * ubuf[6:6 + tb, :] + cw_ref[1:2, :] * ubuf[7:7 + tb, :]
         + cw_ref[2:3, :] * u)
    y_ref[:, 0:cdim] = (pa_ref[:, cdim:2 * cdim] * y).astype(y_ref.dtype)
    tail = ubuf[tb + 6:tb + 8, :]
    ubuf[6:8, :] = tail
    convo_ref[...] = tail

    L = CHUNK
    assert hk == L and L & (L - 1) == 0
    sh = L.bit_length() - 1
    nst = GLA_HEADS * L
    r_i = lax.broadcasted_iota(jnp.int32, (nst, dk), 0)
    c_i = lax.broadcasted_iota(jnp.int32, (nst, dk), 1)
    same_head = (r_i >> sh) == (c_i >> sh)
    a_r = lax.broadcasted_iota(jnp.int32, (nst, nst), 0)
    a_c = lax.broadcasted_iota(jnp.int32, (nst, nst), 1)
    att_keep = ((a_r >> sh) == (a_c >> sh)) & ((a_c & (L - 1)) <= (a_r & (L - 1)))
    tri = (lax.broadcasted_iota(jnp.int32, (L, L), 1)
           <= lax.broadcasted_iota(jnp.int32, (L, L), 0)).astype(BF16)
    w2 = w2_ref[...]
    gb = gb_ref[...]
    onorm = on_ref[...]

    rows = [slice(c * L, (c + 1) * L) for c in range(tb // L)]

    def stack(a):
        return jnp.concatenate([a] * GLA_HEADS, axis=0)

    def heads_to_rows(o0, r):
        return jnp.concatenate([p_ref[r, o0 + h * hv:o0 + (h + 1) * hv] for h in range(GLA_HEADS)], axis=0)

    gk = _dot(p_ref[:, o_l:o_l + LANES], w2) + gb
    la = (jnp.minimum(gk, 0.0) - jnp.log(1.0 + jnp.exp(-jnp.abs(gk)))) * (1.0 / GATE_NORM)
    l_hi, l_lo = _split2(la)
    b = [_dot(tri, l_hi[r]) + _dot(tri, l_lo[r]) for r in rows]
    b_last = [x[L - 1:L, :] for x in b]
    q_st, k_tl, kd_bd, v_st = [], [], [], []
    for c, r in enumerate(rows):
        q = p_ref[r, o_q:o_k].astype(F32) * (hk ** -0.5)
        k = p_ref[r, o_k:o_v].astype(F32)
        q_st.append(jnp.where(same_head, stack(q * jnp.exp(b[c])), 0.0).astype(BF16))
        k_tl.append(stack((k * jnp.exp(-b[c])).astype(BF16)))
        kd_bd.append(jnp.where(same_head, stack(k * jnp.exp(b_last[c] - b[c])), 0.0).astype(BF16))
        v_st.append(heads_to_rows(o_v, r))
    att = [jnp.where(att_keep, _dot_nt(q_st[c], k_tl[c]), 0.0).astype(BF16)
           for c in range(len(rows))]
    upd = [_dot_tn(v_st[c], kd_bd[c]) for c in range(len(rows))]
    o = [_dot(att[c], v_st[c]) for c in range(len(rows))]
    st = st_sc[...]
    for c in range(len(rows)):
        o[c] = o[c] + _dot_nt(q_st[c], st.astype(BF16))
        st = st * jnp.exp(b_last[c]) + upd[c]
    for c, r in enumerate(rows):
        oc = _rms_rows(o[c], onorm) * _silu(heads_to_rows(o_g, r).astype(F32))
        for h in range(GLA_HEADS):
            y_ref[r, cdim + h * hv:cdim + (h + 1) * hv] = oc[h * L:(h + 1) * L, :].astype(y_ref.dtype)
    st_sc[...] = st
    sto_ref[...] = st


def _mixer_ab(proj_a, proj_b, conv_w, w2pad, gb, onorm, conv0, st0, tb):
    bsz, t, _ = proj_a.shape
    cdim = conv_w.shape[1]
    dk = w2pad.shape[1]
    hv = onorm.shape[1]
    dv = hv * GLA_HEADS
    return pl.pallas_call(
        functools.partial(_mixer_body, tb=tb, cdim=cdim, dk=dk, dv=dv),
        grid=(bsz, t // tb),
        in_specs=[
            pl.BlockSpec((None, tb, proj_a.shape[2]), lambda b, i: (b, i, 0)),
            pl.BlockSpec((None, tb, proj_b.shape[2]), lambda b, i: (b, i, 0)),
            _resident(conv_w.shape),
            _resident(w2pad.shape),
            _resident(gb.shape),
            _resident(onorm.shape),
            pl.BlockSpec((None, CONV_WIDTH - 1, cdim), lambda b, i: (b, 0, 0)),
            pl.BlockSpec((None, hv, dk), lambda b, i: (b, 0, 0)),
        ],
        out_specs=[
            pl.BlockSpec((None, tb, cdim + dv), lambda b, i: (b, i, 0)),
            pl.BlockSpec((None, CONV_WIDTH - 1, cdim), lambda b, i: (b, 0, 0)),
            pl.BlockSpec((None, hv, dk), lambda b, i: (b, 0, 0)),
        ],
        out_shape=[
            jax.ShapeDtypeStruct((bsz, t, cdim + dv), BF16),
            jax.ShapeDtypeStruct((bsz, CONV_WIDTH - 1, cdim), F32),
            jax.ShapeDtypeStruct((bsz, hv, dk), F32),
        ],
        scratch_shapes=[
            pltpu.VMEM((tb + 8, cdim), F32),
            pltpu.VMEM((hv, dk), F32),
        ],
        compiler_params=_params("parallel", "arbitrary"),
    )(proj_a, proj_b, conv_w, w2pad, gb, onorm, conv0, st0)


def _attn_body(q_ref, ka_ref, kb_ref, va_ref, vb_ref, brow_ref, o_ref, kw, vw, tab_ref,
               *, tq, tqc, hist_is_padding):
    nwin = BAND_ROWS + tqc

    @pl.when((pl.program_id(0) == 0) & (pl.program_id(1) == 0))
    def _build_table():
        wrow = brow_ref.shape[1]
        sh = CHUNK.bit_length() - 1
        sub = lax.broadcasted_iota(jnp.int32, (SUBLANES, wrow), 0)
        q_chunk = lax.broadcasted_iota(jnp.int32, (tqc, nwin), 0) >> sh
        k_chunk = (lax.broadcasted_iota(jnp.int32, (tqc, nwin), 1) >> sh) - BAND_ROWS // CHUNK
        allowed = (k_chunk <= q_chunk) & (k_chunk >= q_chunk - BAND_ROWS // CHUNK)

        def one_head(h, carry):
            y = jnp.broadcast_to(brow_ref[pl.ds(h, 1), :], (SUBLANES, wrow))
            for bit in range(SUBLANES.bit_length() - 1):
                y = jnp.where(((sub >> bit) & 1) == 1, pltpu.roll(y, 1 << bit, axis=1), y)
            y = jnp.concatenate([y] + [pltpu.roll(y, SUBLANES * a, axis=1)
                                       for a in range(1, tqc // SUBLANES)], axis=0)
            t = jnp.where(allowed, y[:, LANES:LANES + nwin], NEG_INF)
            tab_ref[h >> 1, pl.ds(pl.multiple_of((h & 1) * tqc, tqc), tqc), :] = t
            return carry

        lax.fori_loop(0, 2 * tab_ref.shape[0], one_head, 0)

    kw[0:BAND_ROWS, :] = ka_ref[...].astype(BF16)
    kw[BAND_ROWS:, :] = kb_ref[...]
    vw[0:BAND_ROWS, :] = va_ref[...].astype(BF16)
    vw[BAND_ROWS:, :] = vb_ref[...]
    n_pairs = q_ref.shape[1] // LANES
    lo = lax.broadcasted_iota(jnp.int32, (tqc, LANES), 1) < ATT_HEAD_DIM
    keep_q = ((lax.broadcasted_iota(jnp.int32, (2 * tqc, LANES), 1) < ATT_HEAD_DIM)
              == (lax.broadcasted_iota(jnp.int32, (2 * tqc, LANES), 0) < tqc))
    ones = jnp.ones((nwin, LANES), BF16)

    def run(first_block):
        def window(j):
            r0 = j * tqc
            c0 = max(BAND_ROWS - r0, 0) if first_block else 0
            return r0, c0, slice(r0 + c0, r0 + nwin)

        def scores(j, p):
            r0, c0, keys = window(j)
            lanes = slice(p * LANES, (p + 1) * LANES)
            qp = q_ref[r0:r0 + tqc, lanes]
            q2 = jnp.where(keep_q, jnp.concatenate([qp, qp], axis=0), 0.0)
            return _dot_nt(q2, kw[keys, lanes]) + tab_ref[p, :, c0:]

        def weighted_values(j, p, e):
            r0, c0, keys = window(j)
            lanes = slice(p * LANES, (p + 1) * LANES)
            pv = _dot(e, jnp.concatenate([vw[keys, lanes], ones[c0:]], axis=1))
            pv = pv[:, :LANES] / pv[:, LANES:]
            o_ref[r0:r0 + tqc, lanes] = jnp.where(lo, pv[:tqc], pv[tqc:]).astype(o_ref.dtype)

        for j in range(tq // tqc):
            s_next = scores(j, 0)
            pending = None
            for p in range(n_pairs):
                s = s_next
                if p + 1 < n_pairs:
                    s_next = scores(j, p + 1)
                e = jnp.exp((s - jnp.max(s, axis=-1, keepdims=True)).astype(BF16))
                if pending is not None:
                    weighted_values(*pending)
                pending = (j, p, e)
            weighted_values(*pending)

    if hist_is_padding:
        first = pl.program_id(1) == 0
        pl.when(first)(lambda: run(True))
        pl.when(jnp.logical_not(first))(lambda: run(False))
    else:
        run(False)


def _band_attention(q, k_hist, k_cur, v_hist, v_cur, layer, brow, tq, tqc):
    bsz, t, width = q.shape
    assert t % tq == 0 and tq % tqc == 0
    hist_is_padding = k_hist is None
    if hist_is_padding:
        assert tq == BAND_ROWS
        k_hist, v_hist = k_cur, v_cur
        hist_spec = pl.BlockSpec((None, BAND_ROWS, width), lambda b, i: (b, jnp.maximum(i - 1, 0), 0))
    else:
        assert t == tq and k_hist.shape[2] == BAND_ROWS
        hist_spec = pl.BlockSpec((None, None, BAND_ROWS, width), lambda b, i: (layer, b, 0, 0))
    cur_map = lambda b, i: (b, i, 0)
    return pl.pallas_call(
        functools.partial(_attn_body, tq=tq, tqc=tqc, hist_is_padding=hist_is_padding),
        grid=(bsz, t // tq),
        in_specs=[
            pl.BlockSpec((None, tq, width), cur_map),
            hist_spec,
            pl.BlockSpec((None, tq, width), cur_map),
            hist_spec,
            pl.BlockSpec((None, tq, width), cur_map),
            _resident(brow.shape),
        ],
        out_specs=pl.BlockSpec((None, tq, width), cur_map),
        out_shape=jax.ShapeDtypeStruct((bsz, t, width), BF16),
        scratch_shapes=[
            pltpu.VMEM((BAND_ROWS + tq, width), BF16),
            pltpu.VMEM((BAND_ROWS + tq, width), BF16),
            pltpu.VMEM((width // LANES, 2 * tqc, BAND_ROWS + tqc), F32),
        ],
        compiler_params=_params("arbitrary", "arbitrary"),
    )(q, k_hist, k_cur, v_hist, v_cur, brow)


def _bias_row(rel_bias, tqc):
    width = -(-(LANES + BAND_ROWS + tqc) // LANES) * LANES
    rel = np.clip(BAND_ROWS + LANES - np.arange(width), -(CHUNK - 1), MAX_REL) + (CHUNK - 1)
    return rel_bias[:, rel].astype(F32)


class _Group:
    def __init__(self, x, conv_prev, gla_prev, k_hist, v_hist):
        self.bsz, self.t, d = x.shape
        self.xf = x.reshape(-1, d)
        self.conv_prev, self.gla_prev = conv_prev, gla_prev
        self.k_hist, self.v_hist = k_hist, v_hist
        self.conv_new, self.gla_new, self.k_new, self.v_new = [], [], [], []

    @property
    def fresh(self):
        return self.k_hist is None


def _trunk(groups, w):
    d = groups[0].xf.shape[1]
    depth = w["norm_mix"].shape[0]
    heads = w["rel_bias"].shape[1]
    att_dim = heads * ATT_HEAD_DIM
    tm = BAND_ROWS
    tail_divs = []
    for grp in groups:
        if grp.fresh:
            assert grp.t % tm == 0
            tail_divs.append(grp.t // tm)
        else:
            assert grp.bsz * grp.t == tm and grp.k_hist.shape[2] == BAND_ROWS and grp.t <= BAND_ROWS
            tail_divs.append(1)
            grp.kh = grp.k_hist.reshape(grp.k_hist.shape[:3] + (att_dim,))
            grp.vh = grp.v_hist.reshape(grp.v_hist.shape[:3] + (att_dim,))
    for layer in range(depth):
        g_mix = w["norm_mix"][layer][None, :]
        g_ffn = w["norm_ffn"][layer][None, :]
        xs = [grp.xf for grp in groups]
        mixes = []
        if layer % 2 == 0:
            e = layer // 2
            projs = _norm_proj(xs, g_mix, w["w_in_ab"], e, tm, 3 * w["conv_w"].shape[2])
            for grp, (proj_a, proj_b) in zip(groups, projs):
                st0 = grp.gla_prev[e].reshape(grp.bsz, -1, grp.gla_prev.shape[-1]).swapaxes(1, 2)
                mix, cs, st = _mixer_ab(
                    proj_a.reshape(grp.bsz, grp.t, -1), proj_b.reshape(grp.bsz, grp.t, -1),
                    w["conv_w"][e], w["gk_w2"][e],
                    w["gk_b"][e][None, :], w["gla_onorm"][e][None, :], grp.conv_prev[e], st0,
                    min(BAND_ROWS, grp.t))
                grp.conv_new.append(cs)
                grp.gla_new.append(st.swapaxes(1, 2).reshape(grp.gla_prev.shape[1:]))
                mixes.append(mix.reshape(-1, mix.shape[-1]))
            w_o, w_o_layer = w["w_out_ab"], e
        else:
            o = layer // 2
            qkvs = _norm_qkv(xs, g_mix, w["w_qkv"], o, w["pool"], w["poolt"],
                             w["q_norm"][o], w["k_norm"][o], tm, tail_divs)
            for grp, (q, k, v, kt, vt) in zip(groups, qkvs):
                shp = (grp.bsz, grp.t, att_dim)
                tqc = min(2 * CHUNK, grp.t)
                att = _band_attention(
                    q.reshape(shp), None if grp.fresh else grp.kh, k.reshape(shp),
                    None if grp.fresh else grp.vh, v.reshape(shp), o,
                    _bias_row(w["rel_bias"][o], tqc), min(BAND_ROWS, grp.t), tqc)
                grp.k_new.append(kt)
                grp.v_new.append(vt)
                mixes.append(att.reshape(-1, att_dim))
            w_o, w_o_layer = w["w_o_att"], o
        outs = _post_mixer(xs, mixes, w_o, w_o_layer, g_ffn, w["w_ffn_in"], w["w_ffn_out"], layer, tm, 2)
        for grp, xf in zip(groups, outs):
            grp.xf = xf

    def cache(grp, tails, hist):
        new = jnp.stack(tails).reshape(len(tails), grp.bsz, -1, heads, ATT_HEAD_DIM)
        if grp.fresh:
            return new
        return jnp.concatenate([hist[:, :, grp.t:], new], axis=2)

    return [(grp.xf.reshape(grp.bsz, grp.t, d), jnp.stack(grp.conv_new), jnp.stack(grp.gla_new),
             cache(grp, grp.k_new, grp.k_hist), cache(grp, grp.v_new, grp.v_hist)) for grp in groups]


def kernel(x_prompt, x_sample, state_conv, state_gla, cache_k, cache_v, norm_mix, norm_ffn,
           w_in_ab, conv_w, gla_gk_w2, gla_gk_b, gla_onorm, w_out_ab, w_qkv, q_norm, k_norm,
           rel_bias, w_o_att, w_ffn_in, w_ffn_out):
    bsz = x_prompt.shape[0]
    n_even = state_conv.shape[0]
    n_odd = cache_k.shape[0]
    heads, head_dim = cache_k.shape[-2:]
    att_dim = heads * head_dim
    in_ab = w_in_ab.shape[-1]
    in_pad = -in_ab % LANES
    rank = gla_gk_w2.shape[1]
    pool = (np.arange(att_dim)[:, None] // head_dim == np.arange(LANES)[None, :])
    w = {
        "norm_mix": norm_mix, "norm_ffn": norm_ffn,
        "w_in_ab": jnp.pad(w_in_ab, ((0, 0), (0, 0), (0, in_pad))).astype(BF16),
        "conv_w": conv_w,
        "gk_w2": jnp.pad(gla_gk_w2, ((0, 0), (0, LANES - rank), (0, 0))).astype(BF16),
        "gk_b": gla_gk_b, "gla_onorm": gla_onorm,
        "w_out_ab": w_out_ab.astype(BF16),
        "w_qkv": w_qkv.astype(BF16),
        "q_norm": jnp.tile(q_norm, (1, heads))[:, None, :],
        "k_norm": jnp.tile(k_norm, (1, heads))[:, None, :],
        "rel_bias": rel_bias,
        "w_o_att": w_o_att.astype(BF16),
        "w_ffn_in": w_ffn_in.astype(BF16), "w_ffn_out": w_ffn_out.astype(BF16),
        "pool": jnp.asarray(pool, BF16), "poolt": jnp.asarray(pool.T, BF16),
    }
    conv0 = jnp.zeros((n_even, bsz) + state_conv.shape[2:], x_prompt.dtype)
    gla0 = jnp.zeros((n_even, bsz) + state_gla.shape[2:], F32)
    (y_p, conv_p, gla_p, k_p, v_p), (y_s, conv_s, gla_s, k_s, v_s) = _trunk(
        [_Group(x_prompt, conv0, gla0, None, None),
         _Group(x_sample, state_conv, state_gla, cache_k, cache_v)], w)
    return (y_p, y_s, conv_p, gla_p, k_p, v_p, conv_s, gla_s, k_s, v_s)
```

```python
import functools

import jax
import jax.numpy as jnp
import numpy as np
from jax import lax
from jax.experimental import pallas as pl
from jax.experimental.pallas import tpu as pltpu

F32 = jnp.float32
BF16 = jnp.bfloat16

EPS = 1e-6
NEG_INF = -1e30
CHUNK = 64
CONV_WIDTH = 3
GLA_HEADS = 4
GATE_NORM = 16.0
ATT_HEAD_DIM = 64
BAND_ROWS = 8 * CHUNK
MAX_REL = 256
LANES = 128
SUBLANES = 8
MXU_DIM = 256
VMEM_LIMIT = 56 * 1024 * 1024


def _params(*sem):
    return pltpu.CompilerParams(dimension_semantics=sem, vmem_limit_bytes=VMEM_LIMIT)


def _resident(shape):
    nd = len(shape)
    return pl.BlockSpec(shape, lambda *_: (0,) * nd, pipeline_mode=pl.Buffered(1))


def _resident_layer(stacked, layer):
    nd = stacked.ndim - 1
    return pl.BlockSpec((None,) + stacked.shape[1:], lambda *_: (layer,) + (0,) * nd,
                        pipeline_mode=pl.Buffered(1))


def _dot(a, b):
    return jnp.dot(a, b, preferred_element_type=F32)


def _dot_nt(a, b):
    return lax.dot_general(a, b, (((1,), (1,)), ((), ())), preferred_element_type=F32)


def _dot_tn(a, b):
    return lax.dot_general(a, b, (((0,), (0,)), ((), ())), preferred_element_type=F32)


def _split2(a):
    hi = a.astype(BF16)
    lo = (a - hi.astype(F32)).astype(BF16)
    return hi, lo


def _rms_rows(x, g):
    ms = jnp.mean(x * x, axis=-1, keepdims=True)
    return x * lax.rsqrt(ms + EPS) * g


def _silu(x):
    return x * jax.nn.sigmoid(x)


def _row_halves(ref):
    half = ref.shape[0] // 2
    return [slice(0, half), slice(half, 2 * half)]


def _proj_body(x_ref, g_ref, w_ref, oa_ref, ob_ref):
    split = oa_ref.shape[1]
    rows = _row_halves(x_ref)
    h = [_rms_rows(x_ref[r, :], g_ref[...]).astype(BF16) for r in rows]
    for r, hr in zip(rows, h):
        y = _dot(hr, w_ref[...])
        oa_ref[r, :] = y[:, :split]
        ob_ref[r, :] = y[:, split:].astype(ob_ref.dtype)


def _rowwise_call(body, row_groups, shared, outs, tm):
    n_groups, n_in, n_out = len(row_groups), len(row_groups[0]), len(outs)
    tms = [min(tm, grp[0].shape[0]) for grp in row_groups]
    nblk = [grp[0].shape[0] // t for grp, t in zip(row_groups, tms)]
    start = [sum(nblk[:g]) for g in range(n_groups)]

    def local(g):
        return lambda i: jnp.clip(i - start[g], 0, nblk[g] - 1)

    in_specs, operands, out_specs, out_shape = [], [], [], []
    for g, grp in enumerate(row_groups):
        for a in grp:
            assert a.shape[0] == nblk[g] * tms[g]
            in_specs.append(pl.BlockSpec((tms[g], a.shape[1]), lambda i, f=local(g): (f(i), 0)))
            operands.append(a)
    for a, spec in shared:
        in_specs.append(spec)
        operands.append(a)
    for g in range(n_groups):
        for cols, dtype, keep in outs:
            assert nblk[g] % keep[g] == 0
            out_specs.append(pl.BlockSpec((tms[g], cols), lambda i, f=local(g), k=keep[g]: (f(i) // k, 0)))
            out_shape.append(jax.ShapeDtypeStruct((nblk[g] // keep[g] * tms[g], cols), dtype))

    def per_group(*refs):
        ins, sh = refs[:n_groups * n_in], refs[n_groups * n_in:n_groups * n_in + len(shared)]
        out_refs = refs[n_groups * n_in + len(shared):]
        i = pl.program_id(0)
        for g in range(n_groups):
            pl.when((i >= start[g]) & (i < start[g] + nblk[g]))(functools.partial(
                body, *ins[g * n_in:(g + 1) * n_in], *sh, *out_refs[g * n_out:(g + 1) * n_out]))

    res = pl.pallas_call(
        per_group,
        grid=(sum(nblk),),
        in_specs=in_specs,
        out_specs=out_specs,
        out_shape=out_shape,
        compiler_params=_params("arbitrary"),
    )(*operands)
    return [res[g * n_out:(g + 1) * n_out] for g in range(n_groups)]


def _norm_proj(xs, g, w, layer, tm, split):
    d = xs[0].shape[1]
    every = [1] * len(xs)
    return _rowwise_call(
        _proj_body, [[x] for x in xs],
        [(g, _resident((1, d))), (w, _resident_layer(w, layer))],
        [(split, F32, every), (w.shape[2] - split, BF16, every)], tm)


def _qkv_body(x_ref, g_ref, w_ref, pool_ref, poolt_ref, gq_ref, gk_ref,
              q_ref, k_ref, v_ref, kt_ref, vt_ref, *, att_dim, q_scale):
    def head_norm(a, gt):
        ms = _dot((a * a).astype(BF16), pool_ref[...]) * (1.0 / ATT_HEAD_DIM)
        rb = _dot(lax.rsqrt(ms + EPS).astype(BF16), poolt_ref[...])
        return a * rb * gt

    rows = _row_halves(x_ref)
    h = [_rms_rows(x_ref[r, :], g_ref[...]).astype(BF16) for r in rows]
    qkv = [_dot(hr, w_ref[...]) for hr in h]
    for r, y in zip(rows, qkv):
        q_ref[r, :] = (head_norm(y[:, :att_dim], gq_ref[...]) * q_scale).astype(BF16)
        kn = head_norm(y[:, att_dim:2 * att_dim], gk_ref[...])
        v = y[:, 2 * att_dim:]
        k_ref[r, :] = kn.astype(BF16)
        v_ref[r, :] = v.astype(BF16)
        kt_ref[r, :] = kn
        vt_ref[r, :] = v


def _norm_qkv(xs, g, w, layer, pool, poolt, gq, gk, tm, tail_divs):
    d = xs[0].shape[1]
    att_dim = w.shape[2] // 3
    every = [1] * len(xs)
    return _rowwise_call(
        functools.partial(_qkv_body, att_dim=att_dim, q_scale=ATT_HEAD_DIM ** -0.5),
        [[x] for x in xs],
        [(g, _resident((1, d))), (w, _resident_layer(w, layer)), (pool, _resident(pool.shape)),
         (poolt, _resident(poolt.shape)), (gq, _resident((1, att_dim))), (gk, _resident((1, att_dim)))],
        [(att_dim, BF16, every)] * 3 + [(att_dim, F32, tail_divs)] * 2, tm)


def _post_body(x_ref, a_ref, wo_ref, g_ref, win_ref, wout_ref, o_ref, *, d_ff, edges):
    rows = _row_halves(x_ref)
    acc = [x_ref[r, :] + _dot(a_ref[r, :], wo_ref[...]) for r in rows]
    h = [_rms_rows(x1, g_ref[...]).astype(BF16) for x1 in acc]
    for lo, hi in zip(edges[:-1], edges[1:]):
        for i in range(len(rows)):
            gate = _dot(h[i], win_ref[:, lo:hi])
            up = _dot(h[i], win_ref[:, d_ff + lo:d_ff + hi])
            act = (_silu(gate) * up).astype(BF16)
            acc[i] = acc[i] + _dot(act, wout_ref[lo:hi, :])
    for i, r in enumerate(rows):
        o_ref[r, :] = acc[i]


def _ffn_edges(d_ff, n_chunks):
    assert d_ff % MXU_DIM == 0
    tiles = d_ff // MXU_DIM
    return tuple(MXU_DIM * ((tiles * c + n_chunks - 1) // n_chunks) for c in range(n_chunks + 1))


def _post_mixer(xs, mixes, wo, wo_layer, g, win, wout, layer, tm, n_chunks):
    d = xs[0].shape[1]
    d_ff = wout.shape[1]
    res = _rowwise_call(
        functools.partial(_post_body, d_ff=d_ff, edges=_ffn_edges(d_ff, n_chunks)),
        [[x, a] for x, a in zip(xs, mixes)],
        [(wo, _resident_layer(wo, wo_layer)), (g, _resident((1, d))),
         (win, _resident_layer(win, layer)), (wout, _resident_layer(wout, layer))],
        [(d, F32, [1] * len(xs))], tm)
    return [r[0] for r in res]


def _mixer_body(pa_ref, p_ref, cw_ref, w2_ref, gb_ref, on_ref, conv0_ref, st0_ref,
                y_ref, convo_ref, sto_ref, ubuf, st_sc, *, tb, cdim, dk, dv):
    hk = dk // GLA_HEADS
    hv = dv // GLA_HEADS
    o_q = 0
    o_k = o_q + dk
    o_v = o_k + dk
    o_g = o_v + dv
    o_l = o_g + dv
    t = pl.program_id(1)

    @pl.when(t == 0)
    def _():
        ubuf[0:8, :] = jnp.zeros((8, cdim), F32)
        ubuf[6:8, :] = conv0_ref[...]
        st_sc[...] = st0_ref[...]

    u = pa_ref[:, 0:cdim] * pa_ref[:, 2 * cdim:3 * cdim]
    ubuf[8:8 + tb, :] = u
    y = (cw_ref[0:1, :] * ubuf[6:6 + tb, :] + cw_ref[1:2, :] * ubuf[7:7 + tb, :]
         + cw_ref[2:3, :] * u)
    y_ref[:, 0:cdim] = (pa_ref[:, cdim:2 * cdim] * y).astype(y_ref.dtype)
    tail = ubuf[tb + 6:tb + 8, :]
    ubuf[6:8, :] = tail
    convo_ref[...] = tail

    L = CHUNK
    assert hk == L and L & (L - 1) == 0
    sh = L.bit_length() - 1
    nst = GLA_HEADS * L
    r_i = lax.broadcasted_iota(jnp.int32, (nst, dk), 0)
    c_i = lax.broadcasted_iota(jnp.int32, (nst, dk), 1)
    same_head = (r_i >> sh) == (c_i >> sh)
    a_r = lax.broadcasted_iota(jnp.int32, (nst, nst), 0)
    a_c = lax.broadcasted_iota(jnp.int32, (nst, nst), 1)
    att_keep = ((a_r >> sh) == (a_c >> sh)) & ((a_c & (L - 1)) <= (a_r & (L - 1)))
    tri = (lax.broadcasted_iota(jnp.int32, (L, L), 1)
           <= lax.broadcasted_iota(jnp.int32, (L, L), 0)).astype(BF16)
    w2 = w2_ref[...]
    gb = gb_ref[...]
    onorm = on_ref[...]

    rows = [slice(c * L, (c + 1) * L) for c in range(tb // L)]

    def stack(a):
        return jnp.concatenate([a] * GLA_HEADS, axis=0)

    def heads_to_rows(o0, r):
        return jnp.concatenate([p_ref[r, o0 + h * hv:o0 + (h + 1) * hv] for h in range(GLA_HEADS)], axis=0)

    gk = _dot(p_ref[:, o_l:o_l + LANES], w2) + gb
    la = (jnp.minimum(gk, 0.0) - jnp.log(1.0 + jnp.exp(-jnp.abs(gk)))) * (1.0 / GATE_NORM)
    l_hi, l_lo = _split2(la)
    b = [_dot(tri, l_hi[r]) + _dot(tri, l_lo[r]) for r in rows]
    b_last = [x[L - 1:L, :] for x in b]
    q_st, k_tl, kd_bd, v_st = [], [], [], []
    for c, r in enumerate(rows):
        q = p_ref[r, o_q:o_k].astype(F32) * (hk ** -0.5)
        k = p_ref[r, o_k:o_v].astype(F32)
        q_st.append(jnp.where(same_head, stack(q * jnp.exp(b[c])), 0.0).astype(BF16))
        k_tl.append(stack((k * jnp.exp(-b[c])).astype(BF16)))
        kd_bd.append(jnp.where(same_head, stack(k * jnp.exp(b_last[c] - b[c])), 0.0).astype(BF16))
        v_st.append(heads_to_rows(o_v, r))
    att = [jnp.where(att_keep, _dot_nt(q_st[c], k_tl[c]), 0.0).astype(BF16)
           for c in range(len(rows))]
    upd = [_dot_tn(v_st[c], kd_bd[c]) for c in range(len(rows))]
    o = [_dot(att[c], v_st[c]) for c in range(len(rows))]
    st = st_sc[...]
    for c in range(len(rows)):
        o[c] = o[c] + _dot_nt(q_st[c], st.astype(BF16))
        st = st * jnp.exp(b_last[c]) + upd[c]
    for c, r in enumerate(rows):
        oc = _rms_rows(o[c], onorm) * _silu(heads_to_rows(o_g, r).astype(F32))
        for h in range(GLA_HEADS):
            y_ref[r, cdim + h * hv:cdim + (h + 1) * hv] = oc[h * L:(h + 1) * L, :].astype(y_ref.dtype)
    st_sc[...] = st
    sto_ref[...] = st


def _mixer_ab(proj_a, proj_b, conv_w, w2pad, gb, onorm, conv0, st0, tb):
    bsz, t, _ = proj_a.shape
    cdim = conv_w.shape[1]
    dk = w2pad.shape[1]
    hv = onorm.shape[1]
    dv = hv * GLA_HEADS
    return pl.pallas_call(
        functools.partial(_mixer_body, tb=tb, cdim=cdim, dk=dk, dv=dv),
        grid=(bsz, t // tb),
        in_specs=[
            pl.BlockSpec((None, tb, proj_a.shape[2]), lambda b, i: (b, i, 0)),
            pl.BlockSpec((None, tb, proj_b.shape[2]), lambda b, i: (b, i, 0)),
            _resident(conv_w.shape),
            _resident(w2pad.shape),
            _resident(gb.shape),
            _resident(onorm.shape),
            pl.BlockSpec((None, CONV_WIDTH - 1, cdim), lambda b, i: (b, 0, 0)),
            pl.BlockSpec((None, hv, dk), lambda b, i: (b, 0, 0)),
        ],
        out_specs=[
            pl.BlockSpec((None, tb, cdim + dv), lambda b, i: (b, i, 0)),
            pl.BlockSpec((None, CONV_WIDTH - 1, cdim), lambda b, i: (b, 0, 0)),
            pl.BlockSpec((None, hv, dk), lambda b, i: (b, 0, 0)),
        ],
        out_shape=[
            jax.ShapeDtypeStruct((bsz, t, cdim + dv), BF16),
            jax.ShapeDtypeStruct((bsz, CONV_WIDTH - 1, cdim), F32),
            jax.ShapeDtypeStruct((bsz, hv, dk), F32),
        ],
        scratch_shapes=[
            pltpu.VMEM((tb + 8, cdim), F32),
            pltpu.VMEM((hv, dk), F32),
        ],
        compiler_params=_params("parallel", "arbitrary"),
    )(proj_a, proj_b, conv_w, w2pad, gb, onorm, conv0, st0)


def _attn_body(q_ref, ka_ref, kb_ref, va_ref, vb_ref, brow_ref, o_ref, kw, vw, tab_ref,
               *, tq, tqc, hist_is_padding, ones_block):
    nwin = BAND_ROWS + tqc

    @pl.when((pl.program_id(0) == 0) & (pl.program_id(1) == 0))
    def _build_table():
        wrow = brow_ref.shape[1]
        sh = CHUNK.bit_length() - 1
        sub = lax.broadcasted_iota(jnp.int32, (SUBLANES, wrow), 0)
        q_chunk = lax.broadcasted_iota(jnp.int32, (tqc, nwin), 0) >> sh
        k_chunk = (lax.broadcasted_iota(jnp.int32, (tqc, nwin), 1) >> sh) - BAND_ROWS // CHUNK
        allowed = (k_chunk <= q_chunk) & (k_chunk >= q_chunk - BAND_ROWS // CHUNK)

        def one_head(h, carry):
            y = jnp.broadcast_to(brow_ref[pl.ds(h, 1), :], (SUBLANES, wrow))
            for bit in range(SUBLANES.bit_length() - 1):
                y = jnp.where(((sub >> bit) & 1) == 1, pltpu.roll(y, 1 << bit, axis=1), y)
            y = jnp.concatenate([y] + [pltpu.roll(y, SUBLANES * a, axis=1)
                                       for a in range(1, tqc // SUBLANES)], axis=0)
            t = jnp.where(allowed, y[:, LANES:LANES + nwin], NEG_INF)
            tab_ref[h >> 1, pl.ds(pl.multiple_of((h & 1) * tqc, tqc), tqc), :] = t
            return carry

        lax.fori_loop(0, 2 * tab_ref.shape[0], one_head, 0)

    kw[0:BAND_ROWS, :] = ka_ref[...].astype(BF16)
    kw[BAND_ROWS:, :] = kb_ref[...]
    vw[0:BAND_ROWS, :] = va_ref[...].astype(BF16)
    vw[BAND_ROWS:, :] = vb_ref[...]
    n_pairs = q_ref.shape[1] // LANES
    lo = lax.broadcasted_iota(jnp.int32, (tqc, LANES), 1) < ATT_HEAD_DIM
    keep_q = ((lax.broadcasted_iota(jnp.int32, (2 * tqc, LANES), 1) < ATT_HEAD_DIM)
              == (lax.broadcasted_iota(jnp.int32, (2 * tqc, LANES), 0) < tqc))
    ones = jnp.ones((nwin, LANES), BF16)

    def run(first_block):
        def window(j):
            r0 = j * tqc
            c0 = max(BAND_ROWS - r0, 0) if first_block else 0
            return r0, c0, slice(r0 + c0, r0 + nwin)

        def scores(j, p):
            r0, c0, keys = window(j)
            lanes = slice(p * LANES, (p + 1) * LANES)
            qp = q_ref[r0:r0 + tqc, lanes]
            q2 = jnp.where(keep_q, jnp.concatenate([qp, qp], axis=0), 0.0)
            return _dot_nt(q2, kw[keys, lanes]) + tab_ref[p, :, c0:]

        def weighted_values(j, p, e, den):
            r0, c0, keys = window(j)
            lanes = slice(p * LANES, (p + 1) * LANES)
            if ones_block:
                pv = _dot(e, jnp.concatenate([vw[keys, lanes], ones[c0:]], axis=1))
                pv = pv[:, :LANES] / pv[:, LANES:]
            else:
                pv = _dot(e, vw[keys, lanes]) / den
            o_ref[r0:r0 + tqc, lanes] = jnp.where(lo, pv[:tqc], pv[tqc:]).astype(o_ref.dtype)

        for j in range(tq // tqc):
            s_next = scores(j, 0)
            pending = None
            for p in range(n_pairs):
                s = s_next
                if p + 1 < n_pairs:
                    s_next = scores(j, p + 1)
                e = jnp.exp((s - jnp.max(s, axis=-1, keepdims=True)).astype(BF16))
                den = None if ones_block else jnp.sum(e.astype(F32), axis=-1, keepdims=True)
                if pending is not None:
                    weighted_values(*pending)
                pending = (j, p, e, den)
            weighted_values(*pending)

    if hist_is_padding:
        first = pl.program_id(1) == 0
        pl.when(first)(lambda: run(True))
        pl.when(jnp.logical_not(first))(lambda: run(False))
    else:
        run(False)


def _band_attention(q, k_hist, k_cur, v_hist, v_cur, layer, brow, tq, tqc, ones_block):
    bsz, t, width = q.shape
    assert t % tq == 0 and tq % tqc == 0
    hist_is_padding = k_hist is None
    if hist_is_padding:
        assert tq == BAND_ROWS
        k_hist, v_hist = k_cur, v_cur
        hist_spec = pl.BlockSpec((None, BAND_ROWS, width), lambda b, i: (b, jnp.maximum(i - 1, 0), 0))
    else:
        assert t == tq and k_hist.shape[2] == BAND_ROWS
        hist_spec = pl.BlockSpec((None, None, BAND_ROWS, width), lambda b, i: (layer, b, 0, 0))
    cur_map = lambda b, i: (b, i, 0)
    return pl.pallas_call(
        functools.partial(_attn_body, tq=tq, tqc=tqc, hist_is_padding=hist_is_padding,
                          ones_block=ones_block),
        grid=(bsz, t // tq),
        in_specs=[
            pl.BlockSpec((None, tq, width), cur_map),
            hist_spec,
            pl.BlockSpec((None, tq, width), cur_map),
            hist_spec,
            pl.BlockSpec((None, tq, width), cur_map),
            _resident(brow.shape),
        ],
        out_specs=pl.BlockSpec((None, tq, width), cur_map),
        out_shape=jax.ShapeDtypeStruct((bsz, t, width), BF16),
        scratch_shapes=[
            pltpu.VMEM((BAND_ROWS + tq, width), BF16),
            pltpu.VMEM((BAND_ROWS + tq, width), BF16),
            pltpu.VMEM((width // LANES, 2 * tqc, BAND_ROWS + tqc), F32),
        ],
        compiler_params=_params("arbitrary", "arbitrary"),
    )(q, k_hist, k_cur, v_hist, v_cur, brow)


def _bias_row(rel_bias, tqc):
    width = -(-(LANES + BAND_ROWS + tqc) // LANES) * LANES
    rel = np.clip(BAND_ROWS + LANES - np.arange(width), -(CHUNK - 1), MAX_REL) + (CHUNK - 1)
    return rel_bias[:, rel].astype(F32)


class _Group:
    def __init__(self, x, conv_prev, gla_prev, k_hist, v_hist):
        self.bsz, self.t, d = x.shape
        self.xf = x.reshape(-1, d)
        self.conv_prev, self.gla_prev = conv_prev, gla_prev
        self.k_hist, self.v_hist = k_hist, v_hist
        self.conv_new, self.gla_new, self.k_new, self.v_new = [], [], [], []

    @property
    def fresh(self):
        return self.k_hist is None


def _trunk(groups, w):
    d = groups[0].xf.shape[1]
    depth = w["norm_mix"].shape[0]
    heads = w["rel_bias"].shape[1]
    att_dim = heads * ATT_HEAD_DIM
    tm = BAND_ROWS
    tail_divs = []
    for grp in groups:
        if grp.fresh:
            assert grp.t % tm == 0
            tail_divs.append(grp.t // tm)
        else:
            assert grp.bsz * grp.t == tm and grp.k_hist.shape[2] == BAND_ROWS and grp.t <= BAND_ROWS
            tail_divs.append(1)
            grp.kh = grp.k_hist.reshape(grp.k_hist.shape[:3] + (att_dim,))
            grp.vh = grp.v_hist.reshape(grp.v_hist.shape[:3] + (att_dim,))
    for layer in range(depth):
        g_mix = w["norm_mix"][layer][None, :]
        g_ffn = w["norm_ffn"][layer][None, :]
        xs = [grp.xf for grp in groups]
        mixes = []
        if layer % 2 == 0:
            e = layer // 2
            projs = _norm_proj(xs, g_mix, w["w_in_ab"], e, 2 * tm, 3 * w["conv_w"].shape[2])
            for grp, (proj_a, proj_b) in zip(groups, projs):
                st0 = grp.gla_prev[e].reshape(grp.bsz, -1, grp.gla_prev.shape[-1]).swapaxes(1, 2)
                mix, cs, st = _mixer_ab(
                    proj_a.reshape(grp.bsz, grp.t, -1), proj_b.reshape(grp.bsz, grp.t, -1),
                    w["conv_w"][e], w["gk_w2"][e],
                    w["gk_b"][e][None, :], w["gla_onorm"][e][None, :], grp.conv_prev[e], st0,
                    min(BAND_ROWS, grp.t))
                grp.conv_new.append(cs)
                grp.gla_new.append(st.swapaxes(1, 2).reshape(grp.gla_prev.shape[1:]))
                mixes.append(mix.reshape(-1, mix.shape[-1]))
            w_o, w_o_layer = w["w_out_ab"], e
        else:
            o = layer // 2
            qkvs = _norm_qkv(xs, g_mix, w["w_qkv"], o, w["pool"], w["poolt"],
                             w["q_norm"][o], w["k_norm"][o], tm, tail_divs)
            for grp, (q, k, v, kt, vt) in zip(groups, qkvs):
                shp = (grp.bsz, grp.t, att_dim)
                tqc = min(2 * CHUNK, grp.t)
                att = _band_attention(
                    q.reshape(shp), None if grp.fresh else grp.kh, k.reshape(shp),
                    None if grp.fresh else grp.vh, v.reshape(shp), o,
                    _bias_row(w["rel_bias"][o], tqc), min(BAND_ROWS, grp.t), tqc, o == 0)
                grp.k_new.append(kt)
                grp.v_new.append(vt)
                mixes.append(att.reshape(-1, att_dim))
            w_o, w_o_layer = w["w_o_att"], o
        outs = _post_mixer(xs, mixes, w_o, w_o_layer, g_ffn, w["w_ffn_in"], w["w_ffn_out"], layer, tm, (2, 1, 3, 11)[layer])
        for grp, xf in zip(groups, outs):
            grp.xf = xf

    def cache(grp, tails, hist):
        new = jnp.stack(tails).reshape(len(tails), grp.bsz, -1, heads, ATT_HEAD_DIM)
        if grp.fresh:
            return new
        return jnp.concatenate([hist[:, :, grp.t:], new], axis=2)

    return [(grp.xf.reshape(grp.bsz, grp.t, d), jnp.stack(grp.conv_new), jnp.stack(grp.gla_new),
             cache(grp, grp.k_new, grp.k_hist), cache(grp, grp.v_new, grp.v_hist)) for grp in groups]


def kernel(x_prompt, x_sample, state_conv, state_gla, cache_k, cache_v, norm_mix, norm_ffn,
           w_in_ab, conv_w, gla_gk_w2, gla_gk_b, gla_onorm, w_out_ab, w_qkv, q_norm, k_norm,
           rel_bias, w_o_att, w_ffn_in, w_ffn_out):
    bsz = x_prompt.shape[0]
    n_even = state_conv.shape[0]
    n_odd = cache_k.shape[0]
    heads, head_dim = cache_k.shape[-2:]
    att_dim = heads * head_dim
    in_ab = w_in_ab.shape[-1]
    in_pad = -in_ab % LANES
    rank = gla_gk_w2.shape[1]
    pool = (np.arange(att_dim)[:, None] // head_dim == np.arange(LANES)[None, :])
    w = {
        "norm_mix": norm_mix, "norm_ffn": norm_ffn,
        "w_in_ab": jnp.pad(w_in_ab, ((0, 0), (0, 0), (0, in_pad))).astype(BF16),
        "conv_w": conv_w,
        "gk_w2": jnp.pad(gla_gk_w2, ((0, 0), (0, LANES - rank), (0, 0))).astype(BF16),
        "gk_b": gla_gk_b, "gla_onorm": gla_onorm,
        "w_out_ab": w_out_ab.astype(BF16),
        "w_qkv": w_qkv.astype(BF16),
        "q_norm": jnp.tile(q_norm, (1, heads))[:, None, :],
        "k_norm": jnp.tile(k_norm, (1, heads))[:, None, :],
        "rel_bias": rel_bias,
        "w_o_att": w_o_att.astype(BF16),
        "w_ffn_in": w_ffn_in.astype(BF16), "w_ffn_out": w_ffn_out.astype(BF16),
        "pool": jnp.asarray(pool, BF16), "poolt": jnp.asarray(pool.T, BF16),
    }
    conv0 = jnp.zeros((n_even, bsz) + state_conv.shape[2:], x_prompt.dtype)
    gla0 = jnp.zeros((n_even, bsz) + state_gla.shape[2:], F32)
    (y_p, conv_p, gla_p, k_p, v_p), (y_s, conv_s, gla_s, k_s, v_s) = _trunk(
        [_Group(x_prompt, conv0, gla0, None, None),
         _Group(x_sample, state_conv, state_gla, cache_k, cache_v)], w)
    return (y_p, y_s, conv_p, gla_p, k_p, v_p, conv_s, gla_s, k_s, v_s)
```

```python
import functools

import jax
import jax.numpy as jnp
import numpy as np
from jax import lax
from jax.experimental import pallas as pl
from jax.experimental.pallas import tpu as pltpu

F32 = jnp.float32
BF16 = jnp.bfloat16

EPS = 1e-6
NEG_INF = -1e30
CHUNK = 64
CONV_WIDTH = 3
GLA_HEADS = 4
GATE_NORM = 16.0
ATT_HEAD_DIM = 64
BAND_ROWS = 8 * CHUNK
MAX_REL = 256
LANES = 128
SUBLANES = 8
MXU_DIM = 256
VMEM_LIMIT = 56 * 1024 * 1024


def _params(*sem):
    return pltpu.CompilerParams(dimension_semantics=sem, vmem_limit_bytes=VMEM_LIMIT)


def _resident(shape):
    nd = len(shape)
    return pl.BlockSpec(shape, lambda *_: (0,) * nd, pipeline_mode=pl.Buffered(1))


def _resident_layer(stacked, layer):
    nd = stacked.ndim - 1
    return pl.BlockSpec((None,) + stacked.shape[1:], lambda *_: (layer,) + (0,) * nd,
                        pipeline_mode=pl.Buffered(1))


def _dot(a, b):
    return jnp.dot(a, b, preferred_element_type=F32)


def _dot_nt(a, b):
    return lax.dot_general(a, b, (((1,), (1,)), ((), ())), preferred_element_type=F32)


def _dot_tn(a, b):
    return lax.dot_general(a, b, (((0,), (0,)), ((), ())), preferred_element_type=F32)


def _split2(a):
    hi = a.astype(BF16)
    lo = (a - hi.astype(F32)).astype(BF16)
    return hi, lo


def _rms_rows(x, g):
    ms = jnp.mean(x * x, axis=-1, keepdims=True)
    return x * lax.rsqrt(ms + EPS) * g


def _silu(x):
    return x * jax.nn.sigmoid(x)


def _row_halves(ref, parts=2):
    rows = ref.shape[0] // parts
    return [slice(i * rows, (i + 1) * rows) for i in range(parts)]


def _proj_body(x_ref, g_ref, w_ref, oa_ref, ob_ref):
    split = oa_ref.shape[1]
    rows = _row_halves(x_ref)
    h = [_rms_rows(x_ref[r, :], g_ref[...]).astype(BF16) for r in rows]
    for r, hr in zip(rows, h):
        y = _dot(hr, w_ref[...])
        oa_ref[r, :] = y[:, :split]
        ob_ref[r, :] = y[:, split:].astype(ob_ref.dtype)


def _rowwise_call(body, row_groups, shared, outs, tm):
    n_groups, n_in, n_out = len(row_groups), len(row_groups[0]), len(outs)
    tms = [min(tm, grp[0].shape[0]) for grp in row_groups]
    nblk = [grp[0].shape[0] // t for grp, t in zip(row_groups, tms)]
    start = [sum(nblk[:g]) for g in range(n_groups)]

    def local(g):
        return lambda i: jnp.clip(i - start[g], 0, nblk[g] - 1)

    in_specs, operands, out_specs, out_shape = [], [], [], []
    for g, grp in enumerate(row_groups):
        for a in grp:
            assert a.shape[0] == nblk[g] * tms[g]
            in_specs.append(pl.BlockSpec((tms[g], a.shape[1]), lambda i, f=local(g): (f(i), 0)))
            operands.append(a)
    for a, spec in shared:
        in_specs.append(spec)
        operands.append(a)
    for g in range(n_groups):
        for cols, dtype, keep in outs:
            assert nblk[g] % keep[g] == 0
            out_specs.append(pl.BlockSpec((tms[g], cols), lambda i, f=local(g), k=keep[g]: (f(i) // k, 0)))
            out_shape.append(jax.ShapeDtypeStruct((nblk[g] // keep[g] * tms[g], cols), dtype))

    def per_group(*refs):
        ins, sh = refs[:n_groups * n_in], refs[n_groups * n_in:n_groups * n_in + len(shared)]
        out_refs = refs[n_groups * n_in + len(shared):]
        i = pl.program_id(0)
        for g in range(n_groups):
            pl.when((i >= start[g]) & (i < start[g] + nblk[g]))(functools.partial(
                body, *ins[g * n_in:(g + 1) * n_in], *sh, *out_refs[g * n_out:(g + 1) * n_out]))

    res = pl.pallas_call(
        per_group,
        grid=(sum(nblk),),
        in_specs=in_specs,
        out_specs=out_specs,
        out_shape=out_shape,
        compiler_params=_params("arbitrary"),
    )(*operands)
    return [res[g * n_out:(g + 1) * n_out] for g in range(n_groups)]


def _norm_proj(xs, g, w, layer, tm, split):
    d = xs[0].shape[1]
    every = [1] * len(xs)
    return _rowwise_call(
        _proj_body, [[x] for x in xs],
        [(g, _resident((1, d))), (w, _resident_layer(w, layer))],
        [(split, F32, every), (w.shape[2] - split, BF16, every)], tm)


def _qkv_body(x_ref, g_ref, w_ref, pool_ref, poolt_ref, gq_ref, gk_ref,
              q_ref, k_ref, v_ref, kt_ref, vt_ref, *, att_dim, q_scale, parts):
    def head_norm(a, gt):
        ms = _dot((a * a).astype(BF16), pool_ref[...]) * (1.0 / ATT_HEAD_DIM)
        rb = _dot(lax.rsqrt(ms + EPS).astype(BF16), poolt_ref[...])
        return a * rb * gt

    rows = _row_halves(x_ref, parts)
    h = [_rms_rows(x_ref[r, :], g_ref[...]).astype(BF16) for r in rows]
    qkv = [_dot(hr, w_ref[...]) for hr in h]
    for r, y in zip(rows, qkv):
        q_ref[r, :] = (head_norm(y[:, :att_dim], gq_ref[...]) * q_scale).astype(BF16)
        kn = head_norm(y[:, att_dim:2 * att_dim], gk_ref[...])
        v = y[:, 2 * att_dim:]
        k_ref[r, :] = kn.astype(BF16)
        v_ref[r, :] = v.astype(BF16)
        kt_ref[r, :] = kn
        vt_ref[r, :] = v


def _norm_qkv(xs, g, w, layer, pool, poolt, gq, gk, tm, tail_divs, parts):
    d = xs[0].shape[1]
    att_dim = w.shape[2] // 3
    every = [1] * len(xs)
    return _rowwise_call(
        functools.partial(_qkv_body, att_dim=att_dim, q_scale=ATT_HEAD_DIM ** -0.5, parts=parts),
        [[x] for x in xs],
        [(g, _resident((1, d))), (w, _resident_layer(w, layer)), (pool, _resident(pool.shape)),
         (poolt, _resident(poolt.shape)), (gq, _resident((1, att_dim))), (gk, _resident((1, att_dim)))],
        [(att_dim, BF16, every)] * 3 + [(att_dim, F32, tail_divs)] * 2, tm)


def _post_body(x_ref, a_ref, wo_ref, g_ref, win_ref, wout_ref, o_ref, *, d_ff, edges, parts):
    rows = _row_halves(x_ref, parts)
    acc = [x_ref[r, :] + _dot(a_ref[r, :], wo_ref[...]) for r in rows]
    h = [_rms_rows(x1, g_ref[...]).astype(BF16) for x1 in acc]
    for lo, hi in zip(edges[:-1], edges[1:]):
        for i in range(len(rows)):
            gate = _dot(h[i], win_ref[:, lo:hi])
            up = _dot(h[i], win_ref[:, d_ff + lo:d_ff + hi])
            act = (_silu(gate) * up).astype(BF16)
            acc[i] = acc[i] + _dot(act, wout_ref[lo:hi, :])
    for i, r in enumerate(rows):
        o_ref[r, :] = acc[i]


def _ffn_edges(d_ff, n_chunks):
    assert d_ff % MXU_DIM == 0
    tiles = d_ff // MXU_DIM
    return tuple(MXU_DIM * ((tiles * c + n_chunks - 1) // n_chunks) for c in range(n_chunks + 1))


def _post_mixer(xs, mixes, wo, wo_layer, g, win, wout, layer, tm, n_chunks, parts):
    d = xs[0].shape[1]
    d_ff = wout.shape[1]
    res = _rowwise_call(
        functools.partial(_post_body, d_ff=d_ff, edges=_ffn_edges(d_ff, n_chunks), parts=parts),
        [[x, a] for x, a in zip(xs, mixes)],
        [(wo, _resident_layer(wo, wo_layer)), (g, _resident((1, d))),
         (win, _resident_layer(win, layer)), (wout, _resident_layer(wout, layer))],
        [(d, F32, [1] * len(xs))], tm)
    return [r[0] for r in res]


def _mixer_body(pa_ref, p_ref, cw_ref, w2_ref, gb_ref, on_ref, conv0_ref, st0_ref,
                y_ref, convo_ref, sto_ref, ubuf, st_sc, *, tb, cdim, dk, dv):
    hk = dk // GLA_HEADS
    hv = dv // GLA_HEADS
    o_q = 0
    o_k = o_q + dk
    o_v = o_k + dk
    o_g = o_v + dv
    o_l = o_g + dv
    t = pl.program_id(1)

    @pl.when(t == 0)
    def _():
        ubuf[0:8, :] = jnp.zeros((8, cdim), F32)
        ubuf[6:8, :] = conv0_ref[...]
        st_sc[...] = st0_ref[...]

    u = pa_ref[:, 0:cdim] * pa_ref[:, 2 * cdim:3 * cdim]
    ubuf[8:8 + tb, :] = u
    y = (cw_ref[0:1, :] * ubuf[6:6 + tb, :] + cw_ref[1:2, :] * ubuf[7:7 + tb, :]
         + cw_ref[2:3, :] * u)
    y_ref[:, 0:cdim] = (pa_ref[:, cdim:2 * cdim] * y).astype(y_ref.dtype)
    tail = ubuf[tb + 6:tb + 8, :]
    ubuf[6:8, :] = tail
    convo_ref[...] = tail

    L = CHUNK
    assert hk == L and L & (L - 1) == 0
    sh = L.bit_length() - 1
    nst = GLA_HEADS * L
    r_i = lax.broadcasted_iota(jnp.int32, (nst, dk), 0)
    c_i = lax.broadcasted_iota(jnp.int32, (nst, dk), 1)
    same_head = (r_i >> sh) == (c_i >> sh)
    a_r = lax.broadcasted_iota(jnp.int32, (nst, nst), 0)
    a_c = lax.broadcasted_iota(jnp.int32, (nst, nst), 1)
    att_keep = ((a_r >> sh) == (a_c >> sh)) & ((a_c & (L - 1)) <= (a_r & (L - 1)))
    tri = (lax.broadcasted_iota(jnp.int32, (L, L), 1)
           <= lax.broadcasted_iota(jnp.int32, (L, L), 0)).astype(BF16)
    w2 = w2_ref[...]
    gb = gb_ref[...]
    onorm = on_ref[...]

    rows = [slice(c * L, (c + 1) * L) for c in range(tb // L)]

    def stack(a):
        return jnp.concatenate([a] * GLA_HEADS, axis=0)

    def heads_to_rows(o0, r):
        return jnp.concatenate([p_ref[r, o0 + h * hv:o0 + (h + 1) * hv] for h in range(GLA_HEADS)], axis=0)

    gk = _dot(p_ref[:, o_l:o_l + LANES], w2) + gb
    la = (jnp.minimum(gk, 0.0) - jnp.log(1.0 + jnp.exp(-jnp.abs(gk)))) * (1.0 / GATE_NORM)
    l_hi, l_lo = _split2(la)
    b = [_dot(tri, l_hi[r]) + _dot(tri, l_lo[r]) for r in rows]
    b_last = [x[L - 1:L, :] for x in b]
    q_st, k_tl, kd_bd, v_st = [], [], [], []
    for c, r in enumerate(rows):
        q = p_ref[r, o_q:o_k].astype(F32) * (hk ** -0.5)
        k = p_ref[r, o_k:o_v].astype(F32)
        q_st.append(jnp.where(same_head, stack(q * jnp.exp(b[c])), 0.0).astype(BF16))
        k_tl.append(stack((k * jnp.exp(-b[c])).astype(BF16)))
        kd_bd.append(jnp.where(same_head, stack(k * jnp.exp(b_last[c] - b[c])), 0.0).astype(BF16))
        v_st.append(heads_to_rows(o_v, r))
    att = [jnp.where(att_keep, _dot_nt(q_st[c], k_tl[c]), 0.0).astype(BF16)
           for c in range(len(rows))]
    upd = [_dot_tn(v_st[c], kd_bd[c]) for c in range(len(rows))]
    o = [_dot(att[c], v_st[c]) for c in range(len(rows))]
    st = st_sc[...]
    for c in range(len(rows)):
        o[c] = o[c] + _dot_nt(q_st[c], st.astype(BF16))
        st = st * jnp.exp(b_last[c]) + upd[c]
    for c, r in enumerate(rows):
        oc = _rms_rows(o[c], onorm) * _silu(heads_to_rows(o_g, r).astype(F32))
        for h in range(GLA_HEADS):
            y_ref[r, cdim + h * hv:cdim + (h + 1) * hv] = oc[h * L:(h + 1) * L, :].astype(y_ref.dtype)
    st_sc[...] = st
    sto_ref[...] = st


def _mixer_ab(proj_a, proj_b, conv_w, w2pad, gb, onorm, conv0, st0, tb):
    bsz, t, _ = proj_a.shape
    cdim = conv_w.shape[1]
    dk = w2pad.shape[1]
    hv = onorm.shape[1]
    dv = hv * GLA_HEADS
    return pl.pallas_call(
        functools.partial(_mixer_body, tb=tb, cdim=cdim, dk=dk, dv=dv),
        grid=(bsz, t // tb),
        in_specs=[
            pl.BlockSpec((None, tb, proj_a.shape[2]), lambda b, i: (b, i, 0)),
            pl.BlockSpec((None, tb, proj_b.shape[2]), lambda b, i: (b, i, 0)),
            _resident(conv_w.shape),
            _resident(w2pad.shape),
            _resident(gb.shape),
            _resident(onorm.shape),
            pl.BlockSpec((None, CONV_WIDTH - 1, cdim), lambda b, i: (b, 0, 0)),
            pl.BlockSpec((None, hv, dk), lambda b, i: (b, 0, 0)),
        ],
        out_specs=[
            pl.BlockSpec((None, tb, cdim + dv), lambda b, i: (b, i, 0)),
            pl.BlockSpec((None, CONV_WIDTH - 1, cdim), lambda b, i: (b, 0, 0)),
            pl.BlockSpec((None, hv, dk), lambda b, i: (b, 0, 0)),
        ],
        out_shape=[
            jax.ShapeDtypeStruct((bsz, t, cdim + dv), BF16),
            jax.ShapeDtypeStruct((bsz, CONV_WIDTH - 1, cdim), F32),
            jax.ShapeDtypeStruct((bsz, hv, dk), F32),
        ],
        scratch_shapes=[
            pltpu.VMEM((tb + 8, cdim), F32),
            pltpu.VMEM((hv, dk), F32),
        ],
        compiler_params=_params("parallel", "arbitrary"),
    )(proj_a, proj_b, conv_w, w2pad, gb, onorm, conv0, st0)


def _attn_body(q_ref, ka_ref, kb_ref, va_ref, vb_ref, brow_ref, o_ref, kw, vw, tab_ref,
               *, tq, tqc, hist_is_padding, lookahead):
    nwin = BAND_ROWS + tqc

    @pl.when((pl.program_id(0) == 0) & (pl.program_id(1) == 0))
    def _build_table():
        wrow = brow_ref.shape[1]
        sh = CHUNK.bit_length() - 1
        sub = lax.broadcasted_iota(jnp.int32, (SUBLANES, wrow), 0)
        q_chunk = lax.broadcasted_iota(jnp.int32, (tqc, nwin), 0) >> sh
        k_chunk = (lax.broadcasted_iota(jnp.int32, (tqc, nwin), 1) >> sh) - BAND_ROWS // CHUNK
        allowed = (k_chunk <= q_chunk) & (k_chunk >= q_chunk - BAND_ROWS // CHUNK)

        def one_head(h, carry):
            y = jnp.broadcast_to(brow_ref[pl.ds(h, 1), :], (SUBLANES, wrow))
            for bit in range(SUBLANES.bit_length() - 1):
                y = jnp.where(((sub >> bit) & 1) == 1, pltpu.roll(y, 1 << bit, axis=1), y)
            y = jnp.concatenate([y] + [pltpu.roll(y, SUBLANES * a, axis=1)
                                       for a in range(1, tqc // SUBLANES)], axis=0)
            t = jnp.where(allowed, y[:, LANES:LANES + nwin], NEG_INF)
            tab_ref[h >> 1, pl.ds(pl.multiple_of((h & 1) * tqc, tqc), tqc), :] = t
            return carry

        lax.fori_loop(0, 2 * tab_ref.shape[0], one_head, 0)

    kw[0:BAND_ROWS, :] = ka_ref[...].astype(BF16)
    kw[BAND_ROWS:, :] = kb_ref[...]
    vw[0:BAND_ROWS, :] = va_ref[...].astype(BF16)
    vw[BAND_ROWS:, :] = vb_ref[...]
    n_pairs = q_ref.shape[1] // LANES
    lo = lax.broadcasted_iota(jnp.int32, (tqc, LANES), 1) < ATT_HEAD_DIM
    keep_q = ((lax.broadcasted_iota(jnp.int32, (2 * tqc, LANES), 1) < ATT_HEAD_DIM)
              == (lax.broadcasted_iota(jnp.int32, (2 * tqc, LANES), 0) < tqc))

    def run(first_block):
        def window(j):
            r0 = j * tqc
            c0 = max(BAND_ROWS - r0, 0) if first_block else 0
            return r0, c0, slice(r0 + c0, r0 + nwin)

        def scores(j, p):
            r0, c0, keys = window(j)
            lanes = slice(p * LANES, (p + 1) * LANES)
            qp = q_ref[r0:r0 + tqc, lanes]
            q2 = jnp.where(keep_q, jnp.concatenate([qp, qp], axis=0), 0.0)
            return _dot_nt(q2, kw[keys, lanes]) + tab_ref[p, :, c0:]

        def weighted_values(j, p, e, den):
            r0, c0, keys = window(j)
            lanes = slice(p * LANES, (p + 1) * LANES)
            pv = _dot(e, vw[keys, lanes]) / den
            o_ref[r0:r0 + tqc, lanes] = jnp.where(lo, pv[:tqc], pv[tqc:]).astype(o_ref.dtype)

        for j in range(tq // tqc):
            ahead = [scores(j, p) for p in range(min(lookahead, n_pairs))]
            pending = None
            for p in range(n_pairs):
                s = ahead.pop(0)
                if p + lookahead < n_pairs:
                    ahead.append(scores(j, p + lookahead))
                e = jnp.exp((s - jnp.max(s, axis=-1, keepdims=True)).astype(BF16))
                den = jnp.sum(e.astype(F32), axis=-1, keepdims=True)
                if pending is not None:
                    weighted_values(*pending)
                pending = (j, p, e, den)
            weighted_values(*pending)

    if hist_is_padding:
        first = pl.program_id(1) == 0
        pl.when(first)(lambda: run(True))
        pl.when(jnp.logical_not(first))(lambda: run(False))
    else:
        run(False)


def _band_attention(q, k_hist, k_cur, v_hist, v_cur, layer, brow, tq, tqc, lookahead):
    bsz, t, width = q.shape
    assert t % tq == 0 and tq % tqc == 0
    hist_is_padding = k_hist is None
    if hist_is_padding:
        assert tq == BAND_ROWS
        k_hist, v_hist = k_cur, v_cur
        hist_spec = pl.BlockSpec((None, BAND_ROWS, width), lambda b, i: (b, jnp.maximum(i - 1, 0), 0))
    else:
        assert t == tq and k_hist.shape[2] == BAND_ROWS
        hist_spec = pl.BlockSpec((None, None, BAND_ROWS, width), lambda b, i: (layer, b, 0, 0))
    cur_map = lambda b, i: (b, i, 0)
    return pl.pallas_call(
        functools.partial(_attn_body, tq=tq, tqc=tqc, hist_is_padding=hist_is_padding,
                          lookahead=lookahead),
        grid=(bsz, t // tq),
        in_specs=[
            pl.BlockSpec((None, tq, width), cur_map),
            hist_spec,
            pl.BlockSpec((None, tq, width), cur_map),
            hist_spec,
            pl.BlockSpec((None, tq, width), cur_map),
            _resident(brow.shape),
        ],
        out_specs=pl.BlockSpec((None, tq, width), cur_map),
        out_shape=jax.ShapeDtypeStruct((bsz, t, width), BF16),
        scratch_shapes=[
            pltpu.VMEM((BAND_ROWS + tq, width), BF16),
            pltpu.VMEM((BAND_ROWS + tq, width), BF16),
            pltpu.VMEM((width // LANES, 2 * tqc, BAND_ROWS + tqc), F32),
        ],
        compiler_params=_params("arbitrary", "arbitrary"),
    )(q, k_hist, k_cur, v_hist, v_cur, brow)


def _bias_row(rel_bias, tqc):
    width = -(-(LANES + BAND_ROWS + tqc) // LANES) * LANES
    rel = np.clip(BAND_ROWS + LANES - np.arange(width), -(CHUNK - 1), MAX_REL) + (CHUNK - 1)
    return rel_bias[:, rel].astype(F32)


class _Group:
    def __init__(self, x, conv_prev, gla_prev, k_hist, v_hist):
        self.bsz, self.t, d = x.shape
        self.xf = x.reshape(-1, d)
        self.conv_prev, self.gla_prev = conv_prev, gla_prev
        self.k_hist, self.v_hist = k_hist, v_hist
        self.conv_new, self.gla_new, self.k_new, self.v_new = [], [], [], []

    @property
    def fresh(self):
        return self.k_hist is None


def _trunk(groups, w):
    d = groups[0].xf.shape[1]
    depth = w["norm_mix"].shape[0]
    heads = w["rel_bias"].shape[1]
    att_dim = heads * ATT_HEAD_DIM
    tm = BAND_ROWS
    tail_divs = []
    for grp in groups:
        if grp.fresh:
            assert grp.t % tm == 0
            tail_divs.append(grp.t // tm)
        else:
            assert grp.bsz * grp.t == tm and grp.k_hist.shape[2] == BAND_ROWS and grp.t <= BAND_ROWS
            tail_divs.append(1)
            grp.kh = grp.k_hist.reshape(grp.k_hist.shape[:3] + (att_dim,))
            grp.vh = grp.v_hist.reshape(grp.v_hist.shape[:3] + (att_dim,))
    for layer in range(depth):
        g_mix = w["norm_mix"][layer][None, :]
        g_ffn = w["norm_ffn"][layer][None, :]
        xs = [grp.xf for grp in groups]
        mixes = []
        if layer % 2 == 0:
            e = layer // 2
            projs = _norm_proj(xs, g_mix, w["w_in_ab"], e, 2 * tm, 3 * w["conv_w"].shape[2])
            for grp, (proj_a, proj_b) in zip(groups, projs):
                st0 = grp.gla_prev[e].reshape(grp.bsz, -1, grp.gla_prev.shape[-1]).swapaxes(1, 2)
                mix, cs, st = _mixer_ab(
                    proj_a.reshape(grp.bsz, grp.t, -1), proj_b.reshape(grp.bsz, grp.t, -1),
                    w["conv_w"][e], w["gk_w2"][e],
                    w["gk_b"][e][None, :], w["gla_onorm"][e][None, :], grp.conv_prev[e], st0,
                    min(BAND_ROWS, grp.t))
                grp.conv_new.append(cs)
                grp.gla_new.append(st.swapaxes(1, 2).reshape(grp.gla_prev.shape[1:]))
                mixes.append(mix.reshape(-1, mix.shape[-1]))
            w_o, w_o_layer = w["w_out_ab"], e
        else:
            o = layer // 2
            qkvs = _norm_qkv(xs, g_mix, w["w_qkv"], o, w["pool"], w["poolt"],
                             w["q_norm"][o], w["k_norm"][o], tm, tail_divs, (2, 4)[o])
            for grp, (q, k, v, kt, vt) in zip(groups, qkvs):
                shp = (grp.bsz, grp.t, att_dim)
                tqc = min(2 * CHUNK, grp.t)
                att = _band_attention(
                    q.reshape(shp), None if grp.fresh else grp.kh, k.reshape(shp),
                    None if grp.fresh else grp.vh, v.reshape(shp), o,
                    _bias_row(w["rel_bias"][o], tqc), min(BAND_ROWS, grp.t), tqc, 1 + o)
                grp.k_new.append(kt)
                grp.v_new.append(vt)
                mixes.append(att.reshape(-1, att_dim))
            w_o, w_o_layer = w["w_o_att"], o
        outs = _post_mixer(xs, mixes, w_o, w_o_layer, g_ffn, w["w_ffn_in"], w["w_ffn_out"], layer, tm, 2, (2, 1, 4, 2)[layer])
        for grp, xf in zip(groups, outs):
            grp.xf = xf

    def cache(grp, tails, hist):
        new = jnp.stack(tails).reshape(len(tails), grp.bsz, -1, heads, ATT_HEAD_DIM)
        if grp.fresh:
            return new
        return jnp.concatenate([hist[:, :, grp.t:], new], axis=2)

    return [(grp.xf.reshape(grp.bsz, grp.t, d), jnp.stack(grp.conv_new), jnp.stack(grp.gla_new),
             cache(grp, grp.k_new, grp.k_hist), cache(grp, grp.v_new, grp.v_hist)) for grp in groups]


def kernel(x_prompt, x_sample, state_conv, state_gla, cache_k, cache_v, norm_mix, norm_ffn,
           w_in_ab, conv_w, gla_gk_w2, gla_gk_b, gla_onorm, w_out_ab, w_qkv, q_norm, k_norm,
           rel_bias, w_o_att, w_ffn_in, w_ffn_out):
    bsz = x_prompt.shape[0]
    n_even = state_conv.shape[0]
    n_odd = cache_k.shape[0]
    heads, head_dim = cache_k.shape[-2:]
    att_dim = heads * head_dim
    in_ab = w_in_ab.shape[-1]
    in_pad = -in_ab % LANES
    rank = gla_gk_w2.shape[1]
    pool = (np.arange(att_dim)[:, None] // head_dim == np.arange(LANES)[None, :])
    w = {
        "norm_mix": norm_mix, "norm_ffn": norm_ffn,
        "w_in_ab": jnp.pad(w_in_ab, ((0, 0), (0, 0), (0, in_pad))).astype(BF16),
        "conv_w": conv_w,
        "gk_w2": jnp.pad(gla_gk_w2, ((0, 0), (0, LANES - rank), (0, 0))).astype(BF16),
        "gk_b": gla_gk_b, "gla_onorm": gla_onorm,
        "w_out_ab": w_out_ab.astype(BF16),
        "w_qkv": w_qkv.astype(BF16),
        "q_norm": jnp.tile(q_norm, (1, heads))[:, None, :],
        "k_norm": jnp.tile(k_norm, (1, heads))[:, None, :],
        "rel_bias": rel_bias,
        "w_o_att": w_o_att.astype(BF16),
        "w_ffn_in": w_ffn_in.astype(BF16), "w_ffn_out": w_ffn_out.astype(BF16),
        "pool": jnp.asarray(pool, BF16), "poolt": jnp.asarray(pool.T, BF16),
    }
    conv0 = jnp.zeros((n_even, bsz) + state_conv.shape[2:], x_prompt.dtype)
    gla0 = jnp.zeros((n_even, bsz) + state_gla.shape[2:], F32)
    (y_p, conv_p, gla_p, k_p, v_p), (y_s, conv_s, gla_s, k_s, v_s) = _trunk(
        [_Group(x_prompt, conv0, gla0, None, None),
         _Group(x_sample, state_conv, state_gla, cache_k, cache_v)], w)
    return (y_p, y_s, conv_p, gla_p, k_p, v_p, conv_s, gla_s, k_s, v_s)
```

```python
import functools

import jax
import jax.numpy as jnp
import numpy as np
from jax import lax
from jax.experimental import pallas as pl
from jax.experimental.pallas import tpu as pltpu

F32 = jnp.float32
BF16 = jnp.bfloat16

EPS = 1e-6
NEG_INF = -1e30
CHUNK = 64
CONV_WIDTH = 3
GLA_HEADS = 4
GATE_NORM = 16.0
ATT_HEAD_DIM = 64
BAND_ROWS = 8 * CHUNK
MAX_REL = 256
LANES = 128
SUBLANES = 8
MXU_DIM = 256
VMEM_LIMIT = 56 * 1024 * 1024


def _params(*sem):
    return pltpu.CompilerParams(dimension_semantics=sem, vmem_limit_bytes=VMEM_LIMIT)


def _resident(shape):
    nd = len(shape)
    return pl.BlockSpec(shape, lambda *_: (0,) * nd, pipeline_mode=pl.Buffered(1))


def _resident_layer(stacked, layer):
    nd = stacked.ndim - 1
    return pl.BlockSpec((None,) + stacked.shape[1:], lambda *_: (layer,) + (0,) * nd,
                        pipeline_mode=pl.Buffered(1))


def _dot(a, b):
    return jnp.dot(a, b, preferred_element_type=F32)


def _dot_nt(a, b):
    return lax.dot_general(a, b, (((1,), (1,)), ((), ())), preferred_element_type=F32)


def _dot_tn(a, b):
    return lax.dot_general(a, b, (((0,), (0,)), ((), ())), preferred_element_type=F32)


def _split2(a):
    hi = a.astype(BF16)
    lo = (a - hi.astype(F32)).astype(BF16)
    return hi, lo


def _rms_rows(x, g):
    ms = jnp.mean(x * x, axis=-1, keepdims=True)
    return x * lax.rsqrt(ms + EPS) * g


def _silu(x):
    return x * jax.nn.sigmoid(x)


def _row_halves(ref):
    half = ref.shape[0] // 2
    return [slice(0, half), slice(half, 2 * half)]


def _proj_body(x_ref, g_ref, w_ref, oa_ref, ob_ref):
    split = oa_ref.shape[1]
    rows = _row_halves(x_ref)
    h = [_rms_rows(x_ref[r, :], g_ref[...]).astype(BF16) for r in rows]
    for r, hr in zip(rows, h):
        y = _dot(hr, w_ref[...])
        oa_ref[r, :] = y[:, :split]
        ob_ref[r, :] = y[:, split:].astype(ob_ref.dtype)


def _rowwise_call(body, row_groups, shared, outs, tm):
    n_groups, n_in, n_out = len(row_groups), len(row_groups[0]), len(outs)
    tms = [min(tm, grp[0].shape[0]) for grp in row_groups]
    nblk = [grp[0].shape[0] // t for grp, t in zip(row_groups, tms)]
    start = [sum(nblk[:g]) for g in range(n_groups)]

    def local(g):
        return lambda i: jnp.clip(i - start[g], 0, nblk[g] - 1)

    in_specs, operands, out_specs, out_shape = [], [], [], []
    for g, grp in enumerate(row_groups):
        for a in grp:
            assert a.shape[0] == nblk[g] * tms[g]
            in_specs.append(pl.BlockSpec((tms[g], a.shape[1]), lambda i, f=local(g): (f(i), 0)))
            operands.append(a)
    for a, spec in shared:
        in_specs.append(spec)
        operands.append(a)
    for g in range(n_groups):
        for cols, dtype, keep in outs:
            assert nblk[g] % keep[g] == 0
            out_specs.append(pl.BlockSpec((tms[g], cols), lambda i, f=local(g), k=keep[g]: (f(i) // k, 0)))
            out_shape.append(jax.ShapeDtypeStruct((nblk[g] // keep[g] * tms[g], cols), dtype))

    def per_group(*refs):
        ins, sh = refs[:n_groups * n_in], refs[n_groups * n_in:n_groups * n_in + len(shared)]
        out_refs = refs[n_groups * n_in + len(shared):]
        i = pl.program_id(0)
        for g in range(n_groups):
            pl.when((i >= start[g]) & (i < start[g] + nblk[g]))(functools.partial(
                body, *ins[g * n_in:(g + 1) * n_in], *sh, *out_refs[g * n_out:(g + 1) * n_out]))

    res = pl.pallas_call(
        per_group,
        grid=(sum(nblk),),
        in_specs=in_specs,
        out_specs=out_specs,
        out_shape=out_shape,
        compiler_params=_params("arbitrary"),
    )(*operands)
    return [res[g * n_out:(g + 1) * n_out] for g in range(n_groups)]


def _norm_proj(xs, g, w, layer, tm, split):
    d = xs[0].shape[1]
    every = [1] * len(xs)
    return _rowwise_call(
        _proj_body, [[x] for x in xs],
        [(g, _resident((1, d))), (w, _resident_layer(w, layer))],
        [(split, F32, every), (w.shape[2] - split, BF16, every)], tm)


def _qkv_body(x_ref, g_ref, w_ref, pool_ref, poolt_ref, gq_ref, gk_ref,
              q_ref, k_ref, v_ref, kt_ref, vt_ref, *, att_dim, q_scale):
    def head_norm(a, gt):
        ms = _dot((a * a).astype(BF16), pool_ref[...]) * (1.0 / ATT_HEAD_DIM)
        rb = _dot(lax.rsqrt(ms + EPS).astype(BF16), poolt_ref[...])
        return a * rb * gt

    rows = _row_halves(x_ref)
    h = [_rms_rows(x_ref[r, :], g_ref[...]).astype(BF16) for r in rows]
    qkv = [_dot(hr, w_ref[...]) for hr in h]
    for r, y in zip(rows, qkv):
        q_ref[r, :] = (head_norm(y[:, :att_dim], gq_ref[...]) * q_scale).astype(BF16)
        kn = head_norm(y[:, att_dim:2 * att_dim], gk_ref[...])
        v = y[:, 2 * att_dim:]
        k_ref[r, :] = kn.astype(BF16)
        v_ref[r, :] = v.astype(BF16)
        kt_ref[r, :] = kn
        vt_ref[r, :] = v


def _norm_qkv(xs, g, w, layer, pool, poolt, gq, gk, tm, tail_divs):
    d = xs[0].shape[1]
    att_dim = w.shape[2] // 3
    every = [1] * len(xs)
    return _rowwise_call(
        functools.partial(_qkv_body, att_dim=att_dim, q_scale=ATT_HEAD_DIM ** -0.5),
        [[x] for x in xs],
        [(g, _resident((1, d))), (w, _resident_layer(w, layer)), (pool, _resident(pool.shape)),
         (poolt, _resident(poolt.shape)), (gq, _resident((1, att_dim))), (gk, _resident((1, att_dim)))],
        [(att_dim, BF16, every)] * 3 + [(att_dim, F32, tail_divs)] * 2, tm)


def _post_body(x_ref, a_ref, wo_ref, g_ref, win_ref, wout_ref, o_ref, *, d_ff, edges):
    rows = _row_halves(x_ref)
    acc = [x_ref[r, :] + _dot(a_ref[r, :], wo_ref[...]) for r in rows]
    h = [_rms_rows(x1, g_ref[...]).astype(BF16) for x1 in acc]
    for lo, hi in zip(edges[:-1], edges[1:]):
        for i in range(len(rows)):
            gate = _dot(h[i], win_ref[:, lo:hi])
            up = _dot(h[i], win_ref[:, d_ff + lo:d_ff + hi])
            act = (_silu(gate) * up).astype(BF16)
            acc[i] = acc[i] + _dot(act, wout_ref[lo:hi, :])
    for i, r in enumerate(rows):
        o_ref[r, :] = acc[i]


def _ffn_edges(d_ff, n_chunks):
    assert d_ff % MXU_DIM == 0
    tiles = d_ff // MXU_DIM
    return tuple(MXU_DIM * ((tiles * c + n_chunks - 1) // n_chunks) for c in range(n_chunks + 1))


def _post_mixer(xs, mixes, wo, wo_layer, g, win, wout, layer, tm, n_chunks):
    d = xs[0].shape[1]
    d_ff = wout.shape[1]
    res = _rowwise_call(
        functools.partial(_post_body, d_ff=d_ff, edges=_ffn_edges(d_ff, n_chunks)),
        [[x, a] for x, a in zip(xs, mixes)],
        [(wo, _resident_layer(wo, wo_layer)), (g, _resident((1, d))),
         (win, _resident_layer(win, layer)), (wout, _resident_layer(wout, layer))],
        [(d, F32, [1] * len(xs))], tm)
    return [r[0] for r in res]


def _mixer_body(pa_ref, p_ref, cw_ref, w2_ref, gb_ref, on_ref, conv0_ref, st0_ref,
                y_ref, convo_ref, sto_ref, ubuf, st_sc, *, tb, cdim, dk, dv):
    hk = dk // GLA_HEADS
    hv = dv // GLA_HEADS
    o_q = 0
    o_k = o_q + dk
    o_v = o_k + dk
    o_g = o_v + dv
    o_l = o_g + dv
    t = pl.program_id(1)

    @pl.when(t == 0)
    def _():
        ubuf[0:8, :] = jnp.zeros((8, cdim), F32)
        ubuf[6:8, :] = conv0_ref[...]
        st_sc[...] = st0_ref[...]

    u = pa_ref[:, 0:cdim] * pa_ref[:, 2 * cdim:3 * cdim]
    ubuf[8:8 + tb, :] = u
    y = (cw_ref[0:1, :] * ubuf[6:6 + tb, :] + cw_ref[1:2, :] * ubuf[7:7 + tb, :]
         + cw_ref[2:3, :] * u)
    y_ref[:, 0:cdim] = (pa_ref[:, cdim:2 * cdim] * y).astype(y_ref.dtype)
    tail = ubuf[tb + 6:tb + 8, :]
    ubuf[6:8, :] = tail
    convo_ref[...] = tail

    L = CHUNK
    assert hk == L and L & (L - 1) == 0
    sh = L.bit_length() - 1
    nst = GLA_HEADS * L
    r_i = lax.broadcasted_iota(jnp.int32, (nst, dk), 0)
    c_i = lax.broadcasted_iota(jnp.int32, (nst, dk), 1)
    same_head = (r_i >> sh) == (c_i >> sh)
    a_r = lax.broadcasted_iota(jnp.int32, (nst, nst), 0)
    a_c = lax.broadcasted_iota(jnp.int32, (nst, nst), 1)
    att_keep = ((a_r >> sh) == (a_c >> sh)) & ((a_c & (L - 1)) <= (a_r & (L - 1)))
    tri = (lax.broadcasted_iota(jnp.int32, (L, L), 1)
           <= lax.broadcasted_iota(jnp.int32, (L, L), 0)).astype(BF16)
    w2 = w2_ref[...]
    gb = gb_ref[...]
    onorm = on_ref[...]

    rows = [slice(c * L, (c + 1) * L) for c in range(tb // L)]

    def stack(a):
        return jnp.concatenate([a] * GLA_HEADS, axis=0)

    def heads_to_rows(o0, r):
        return jnp.concatenate([p_ref[r, o0 + h * hv:o0 + (h + 1) * hv] for h in range(GLA_HEADS)], axis=0)

    gk = _dot(p_ref[:, o_l:o_l + LANES], w2) + gb
    la = (jnp.minimum(gk, 0.0) - jnp.log(1.0 + jnp.exp(-jnp.abs(gk)))) * (1.0 / GATE_NORM)
    l_hi, l_lo = _split2(la)
    b = [_dot(tri, l_hi[r]) + _dot(tri, l_lo[r]) for r in rows]
    b_last = [x[L - 1:L, :] for x in b]
    q_st, k_tl, kd_bd, v_st = [], [], [], []
    for c, r in enumerate(rows):
        q = p_ref[r, o_q:o_k].astype(F32) * (hk ** -0.5)
        k = p_ref[r, o_k:o_v].astype(F32)
        q_st.append(jnp.where(same_head, stack(q * jnp.exp(b[c])), 0.0).astype(BF16))
        k_tl.append(stack((k * jnp.exp(-b[c])).astype(BF16)))
        kd_bd.append(jnp.where(same_head, stack(k * jnp.exp(b_last[c] - b[c])), 0.0).astype(BF16))
        v_st.append(heads_to_rows(o_v, r))
    att = [jnp.where(att_keep, _dot_nt(q_st[c], k_tl[c]), 0.0).astype(BF16)
           for c in range(len(rows))]
    upd = [_dot_tn(v_st[c], kd_bd[c]) for c in range(len(rows))]
    o = [_dot(att[c], v_st[c]) for c in range(len(rows))]
    st = st_sc[...]
    for c in range(len(rows)):
        o[c] = o[c] + _dot_nt(q_st[c], st.astype(BF16))
        st = st * jnp.exp(b_last[c]) + upd[c]
    for c, r in enumerate(rows):
        oc = _rms_rows(o[c], onorm) * _silu(heads_to_rows(o_g, r).astype(F32))
        for h in range(GLA_HEADS):
            y_ref[r, cdim + h * hv:cdim + (h + 1) * hv] = oc[h * L:(h + 1) * L, :].astype(y_ref.dtype)
    st_sc[...] = st
    sto_ref[...] = st


def _mixer_ab(proj_a, proj_b, conv_w, w2pad, gb, onorm, conv0, st0, tb):
    bsz, t, _ = proj_a.shape
    cdim = conv_w.shape[1]
    dk = w2pad.shape[1]
    hv = onorm.shape[1]
    dv = hv * GLA_HEADS
    return pl.pallas_call(
        functools.partial(_mixer_body, tb=tb, cdim=cdim, dk=dk, dv=dv),
        grid=(bsz, t // tb),
        in_specs=[
            pl.BlockSpec((None, tb, proj_a.shape[2]), lambda b, i: (b, i, 0)),
            pl.BlockSpec((None, tb, proj_b.shape[2]), lambda b, i: (b, i, 0)),
            _resident(conv_w.shape),
            _resident(w2pad.shape),
            _resident(gb.shape),
            _resident(onorm.shape),
            pl.BlockSpec((None, CONV_WIDTH - 1, cdim), lambda b, i: (b, 0, 0)),
            pl.BlockSpec((None, hv, dk), lambda b, i: (b, 0, 0)),
        ],
        out_specs=[
            pl.BlockSpec((None, tb, cdim + dv), lambda b, i: (b, i, 0)),
            pl.BlockSpec((None, CONV_WIDTH - 1, cdim), lambda b, i: (b, 0, 0)),
            pl.BlockSpec((None, hv, dk), lambda b, i: (b, 0, 0)),
        ],
        out_shape=[
            jax.ShapeDtypeStruct((bsz, t, cdim + dv), BF16),
            jax.ShapeDtypeStruct((bsz, CONV_WIDTH - 1, cdim), F32),
            jax.ShapeDtypeStruct((bsz, hv, dk), F32),
        ],
        scratch_shapes=[
            pltpu.VMEM((tb + 8, cdim), F32),
            pltpu.VMEM((hv, dk), F32),
        ],
        compiler_params=_params("parallel", "arbitrary"),
    )(proj_a, proj_b, conv_w, w2pad, gb, onorm, conv0, st0)


def _attn_body(q_ref, ka_ref, kb_ref, va_ref, vb_ref, brow_ref, o_ref, kw, vw, tab_ref,
               *, tq, tqc, hist_is_padding):
    nwin = BAND_ROWS + tqc

    @pl.when((pl.program_id(0) == 0) & (pl.program_id(1) == 0))
    def _build_table():
        wrow = brow_ref.shape[1]
        sh = CHUNK.bit_length() - 1
        sub = lax.broadcasted_iota(jnp.int32, (SUBLANES, wrow), 0)
        q_chunk = lax.broadcasted_iota(jnp.int32, (tqc, nwin), 0) >> sh
        k_chunk = (lax.broadcasted_iota(jnp.int32, (tqc, nwin), 1) >> sh) - BAND_ROWS // CHUNK
        allowed = (k_chunk <= q_chunk) & (k_chunk >= q_chunk - BAND_ROWS // CHUNK)

        def one_head(h, carry):
            y = jnp.broadcast_to(brow_ref[pl.ds(h, 1), :], (SUBLANES, wrow))
            for bit in range(SUBLANES.bit_length() - 1):
                y = jnp.where(((sub >> bit) & 1) == 1, pltpu.roll(y, 1 << bit, axis=1), y)
            y = jnp.concatenate([y] + [pltpu.roll(y, SUBLANES * a, axis=1)
                                       for a in range(1, tqc // SUBLANES)], axis=0)
            slack = _table_slack(tqc)
            t = jnp.where(allowed, y[:, slack:slack + nwin], NEG_INF)
            tab_ref[h >> 1, pl.ds(pl.multiple_of((h & 1) * tqc, tqc), tqc), :] = t
            return carry

        lax.fori_loop(0, 2 * tab_ref.shape[0], one_head, 0)

    kw[0:BAND_ROWS, :] = ka_ref[...].astype(BF16)
    kw[BAND_ROWS:, :] = kb_ref[...]
    vw[0:BAND_ROWS, :] = va_ref[...].astype(BF16)
    vw[BAND_ROWS:, :] = vb_ref[...]
    n_pairs = q_ref.shape[1] // LANES
    lo = lax.broadcasted_iota(jnp.int32, (tqc, LANES), 1) < ATT_HEAD_DIM
    keep_q = ((lax.broadcasted_iota(jnp.int32, (2 * tqc, LANES), 1) < ATT_HEAD_DIM)
              == (lax.broadcasted_iota(jnp.int32, (2 * tqc, LANES), 0) < tqc))

    def run(first_block):
        def window(j):
            r0 = j * tqc
            c0 = max(BAND_ROWS - r0, 0) if first_block else 0
            return r0, c0, slice(r0 + c0, r0 + nwin)

        def scores(j, p):
            r0, c0, keys = window(j)
            lanes = slice(p * LANES, (p + 1) * LANES)
            qp = q_ref[r0:r0 + tqc, lanes]
            q2 = jnp.where(keep_q, jnp.concatenate([qp, qp], axis=0), 0.0)
            return _dot_nt(q2, kw[keys, lanes]) + tab_ref[p, :, c0:]

        def weighted_values(j, p, e, den):
            r0, c0, keys = window(j)
            lanes = slice(p * LANES, (p + 1) * LANES)
            pv = _dot(e, vw[keys, lanes]) / den
            o_ref[r0:r0 + tqc, lanes] = jnp.where(lo, pv[:tqc], pv[tqc:]).astype(o_ref.dtype)

        for j in range(tq // tqc):
            s_next = scores(j, 0)
            pending = None
            for p in range(n_pairs):
                s = s_next
                if p + 1 < n_pairs:
                    s_next = scores(j, p + 1)
                e = jnp.exp((s - jnp.max(s, axis=-1, keepdims=True)).astype(BF16))
                den = jnp.sum(e.astype(F32), axis=-1, keepdims=True)
                if pending is not None:
                    weighted_values(*pending)
                pending = (j, p, e, den)
            weighted_values(*pending)

    if hist_is_padding:
        first = pl.program_id(1) == 0
        pl.when(first)(lambda: run(True))
        pl.when(jnp.logical_not(first))(lambda: run(False))
    else:
        run(False)


def _band_attention(q, k_hist, k_cur, v_hist, v_cur, layer, brow, tq, tqc):
    bsz, t, width = q.shape
    assert t % tq == 0 and tq % tqc == 0
    hist_is_padding = k_hist is None
    if hist_is_padding:
        assert tq == BAND_ROWS
        k_hist, v_hist = k_cur, v_cur
        hist_spec = pl.BlockSpec((None, BAND_ROWS, width), lambda b, i: (b, jnp.maximum(i - 1, 0), 0))
    else:
        assert t == tq and k_hist.shape[2] == BAND_ROWS
        hist_spec = pl.BlockSpec((None, None, BAND_ROWS, width), lambda b, i: (layer, b, 0, 0))
    cur_map = lambda b, i: (b, i, 0)
    return pl.pallas_call(
        functools.partial(_attn_body, tq=tq, tqc=tqc, hist_is_padding=hist_is_padding),
        grid=(bsz, t // tq),
        in_specs=[
            pl.BlockSpec((None, tq, width), cur_map),
            hist_spec,
            pl.BlockSpec((None, tq, width), cur_map),
            hist_spec,
            pl.BlockSpec((None, tq, width), cur_map),
            _resident(brow.shape),
        ],
        out_specs=pl.BlockSpec((None, tq, width), cur_map),
        out_shape=jax.ShapeDtypeStruct((bsz, t, width), BF16),
        scratch_shapes=[
            pltpu.VMEM((BAND_ROWS + tq, width), BF16),
            pltpu.VMEM((BAND_ROWS + tq, width), BF16),
            pltpu.VMEM((width // LANES, 2 * tqc, BAND_ROWS + tqc), F32),
        ],
        compiler_params=_params("arbitrary", "arbitrary"),
    )(q, k_hist, k_cur, v_hist, v_cur, brow)


def _table_slack(tqc):
    return -(-tqc // LANES) * LANES


def _bias_row(rel_bias, tqc):
    slack = _table_slack(tqc)
    width = -(-(slack + BAND_ROWS + tqc) // LANES) * LANES
    rel = np.clip(BAND_ROWS + slack - np.arange(width), -(CHUNK - 1), MAX_REL) + (CHUNK - 1)
    return rel_bias[:, rel].astype(F32)


class _Group:
    def __init__(self, x, conv_prev, gla_prev, k_hist, v_hist):
        self.bsz, self.t, d = x.shape
        self.xf = x.reshape(-1, d)
        self.conv_prev, self.gla_prev = conv_prev, gla_prev
        self.k_hist, self.v_hist = k_hist, v_hist
        self.conv_new, self.gla_new, self.k_new, self.v_new = [], [], [], []

    @property
    def fresh(self):
        return self.k_hist is None


def _trunk(groups, w):
    d = groups[0].xf.shape[1]
    depth = w["norm_mix"].shape[0]
    heads = w["rel_bias"].shape[1]
    att_dim = heads * ATT_HEAD_DIM
    tm = BAND_ROWS
    tail_divs = []
    for grp in groups:
        if grp.fresh:
            assert grp.t % tm == 0
            tail_divs.append(grp.t // tm)
        else:
            assert grp.bsz * grp.t == tm and grp.k_hist.shape[2] == BAND_ROWS and grp.t <= BAND_ROWS
            tail_divs.append(1)
            grp.kh = grp.k_hist.reshape(grp.k_hist.shape[:3] + (att_dim,))
            grp.vh = grp.v_hist.reshape(grp.v_hist.shape[:3] + (att_dim,))
    for layer in range(depth):
        g_mix = w["norm_mix"][layer][None, :]
        g_ffn = w["norm_ffn"][layer][None, :]
        xs = [grp.xf for grp in groups]
        mixes = []
        if layer % 2 == 0:
            e = layer // 2
            projs = _norm_proj(xs, g_mix, w["w_in_ab"], e, 2 * tm, 3 * w["conv_w"].shape[2])
            for grp, (proj_a, proj_b) in zip(groups, projs):
                st0 = grp.gla_prev[e].reshape(grp.bsz, -1, grp.gla_prev.shape[-1]).swapaxes(1, 2)
                mix, cs, st = _mixer_ab(
                    proj_a.reshape(grp.bsz, grp.t, -1), proj_b.reshape(grp.bsz, grp.t, -1),
                    w["conv_w"][e], w["gk_w2"][e],
                    w["gk_b"][e][None, :], w["gla_onorm"][e][None, :], grp.conv_prev[e], st0,
                    min((1, 2)[e] * BAND_ROWS, grp.t))
                grp.conv_new.append(cs)
                grp.gla_new.append(st.swapaxes(1, 2).reshape(grp.gla_prev.shape[1:]))
                mixes.append(mix.reshape(-1, mix.shape[-1]))
            w_o, w_o_layer = w["w_out_ab"], e
        else:
            o = layer // 2
            qkvs = _norm_qkv(xs, g_mix, w["w_qkv"], o, w["pool"], w["poolt"],
                             w["q_norm"][o], w["k_norm"][o], tm, tail_divs)
            for grp, (q, k, v, kt, vt) in zip(groups, qkvs):
                shp = (grp.bsz, grp.t, att_dim)
                tqc = min((2, 4)[o] * CHUNK, grp.t)
                att = _band_attention(
                    q.reshape(shp), None if grp.fresh else grp.kh, k.reshape(shp),
                    None if grp.fresh else grp.vh, v.reshape(shp), o,
                    _bias_row(w["rel_bias"][o], tqc), min(BAND_ROWS, grp.t), tqc)
                grp.k_new.append(kt)
                grp.v_new.append(vt)
                mixes.append(att.reshape(-1, att_dim))
            w_o, w_o_layer = w["w_o_att"], o
        outs = _post_mixer(xs, mixes, w_o, w_o_layer, g_ffn, w["w_ffn_in"], w["w_ffn_out"], layer, tm, 2)
        for grp, xf in zip(groups, outs):
            grp.xf = xf

    def cache(grp, tails, hist):
        new = jnp.stack(tails).reshape(len(tails), grp.bsz, -1, heads, ATT_HEAD_DIM)
        if grp.fresh:
            return new
        return jnp.concatenate([hist[:, :, grp.t:], new], axis=2)

    return [(grp.xf.reshape(grp.bsz, grp.t, d), jnp.stack(grp.conv_new), jnp.stack(grp.gla_new),
             cache(grp, grp.k_new, grp.k_hist), cache(grp, grp.v_new, grp.v_hist)) for grp in groups]


def kernel(x_prompt, x_sample, state_conv, state_gla, cache_k, cache_v, norm_mix, norm_ffn,
           w_in_ab, conv_w, gla_gk_w2, gla_gk_b, gla_onorm, w_out_ab, w_qkv, q_norm, k_norm,
           rel_bias, w_o_att, w_ffn_in, w_ffn_out):
    bsz = x_prompt.shape[0]
    n_even = state_conv.shape[0]
    n_odd = cache_k.shape[0]
    heads, head_dim = cache_k.shape[-2:]
    att_dim = heads * head_dim
    in_ab = w_in_ab.shape[-1]
    in_pad = -in_ab % LANES
    rank = gla_gk_w2.shape[1]
    pool = (np.arange(att_dim)[:, None] // head_dim == np.arange(LANES)[None, :])
    w = {
        "norm_mix": norm_mix, "norm_ffn": norm_ffn,
        "w_in_ab": jnp.pad(w_in_ab, ((0, 0), (0, 0), (0, in_pad))).astype(BF16),
        "conv_w": conv_w,
        "gk_w2": jnp.pad(gla_gk_w2, ((0, 0), (0, LANES - rank), (0, 0))).astype(BF16),
        "gk_b": gla_gk_b, "gla_onorm": gla_onorm,
        "w_out_ab": w_out_ab.astype(BF16),
        "w_qkv": w_qkv.astype(BF16),
        "q_norm": jnp.tile(q_norm, (1, heads))[:, None, :],
        "k_norm": jnp.tile(k_norm, (1, heads))[:, None, :],
        "rel_bias": rel_bias,
        "w_o_att": w_o_att.astype(BF16),
        "w_ffn_in": w_ffn_in.astype(BF16), "w_ffn_out": w_ffn_out.astype(BF16),
        "pool": jnp.asarray(pool, BF16), "poolt": jnp.asarray(pool.T, BF16),
    }
    conv0 = jnp.zeros((n_even, bsz) + state_conv.shape[2:], x_prompt.dtype)
    gla0 = jnp.zeros((n_even, bsz) + state_gla.shape[2:], F32)
    (y_p, conv_p, gla_p, k_p, v_p), (y_s, conv_s, gla_s, k_s, v_s) = _trunk(
        [_Group(x_prompt, conv0, gla0, None, None),
         _Group(x_sample, state_conv, state_gla, cache_k, cache_v)], w)
    return (y_p, y_s, conv_p, gla_p, k_p, v_p, conv_s, gla_s, k_s, v_s)
```

```python
import functools

import jax
import jax.numpy as jnp
import numpy as np
from jax import lax
from jax.experimental import pallas as pl
from jax.experimental.pallas import tpu as pltpu

F32 = jnp.float32
BF16 = jnp.bfloat16

EPS = 1e-6
NEG_INF = -1e30
CHUNK = 64
CONV_WIDTH = 3
GLA_HEADS = 4
GATE_NORM = 16.0
ATT_HEAD_DIM = 64
BAND_ROWS = 8 * CHUNK
MAX_REL = 256
LANES = 128
SUBLANES = 8
MXU_DIM = 256
VMEM_LIMIT = 56 * 1024 * 1024


def _params(*sem):
    return pltpu.CompilerParams(dimension_semantics=sem, vmem_limit_bytes=VMEM_LIMIT)


def _resident(shape):
    nd = len(shape)
    return pl.BlockSpec(shape, lambda *_: (0,) * nd, pipeline_mode=pl.Buffered(1))


def _resident_layer(stacked, layer):
    nd = stacked.ndim - 1
    return pl.BlockSpec((None,) + stacked.shape[1:], lambda *_: (layer,) + (0,) * nd,
                        pipeline_mode=pl.Buffered(1))


def _dot(a, b):
    return jnp.dot(a, b, preferred_element_type=F32)


def _dot_nt(a, b):
    return lax.dot_general(a, b, (((1,), (1,)), ((), ())), preferred_element_type=F32)


def _dot_tn(a, b):
    return lax.dot_general(a, b, (((0,), (0,)), ((), ())), preferred_element_type=F32)


def _split2(a):
    hi = a.astype(BF16)
    lo = (a - hi.astype(F32)).astype(BF16)
    return hi, lo


def _rms_rows(x, g):
    ms = jnp.mean(x * x, axis=-1, keepdims=True)
    return x * lax.rsqrt(ms + EPS) * g


def _silu(x):
    return x * jax.nn.sigmoid(x)


def _row_halves(ref):
    half = ref.shape[0] // 2
    return [slice(0, half), slice(half, 2 * half)]


def _proj_body(x_ref, g_ref, w_ref, oa_ref, ob_ref):
    split = oa_ref.shape[1]
    rows = _row_halves(x_ref)
    h = [_rms_rows(x_ref[r, :], g_ref[...]).astype(BF16) for r in rows]
    for r, hr in zip(rows, h):
        y = _dot(hr, w_ref[...])
        oa_ref[r, :] = y[:, :split]
        ob_ref[r, :] = y[:, split:].astype(ob_ref.dtype)


def _rowwise_call(body, row_groups, shared, outs, tm):
    n_groups, n_in, n_out = len(row_groups), len(row_groups[0]), len(outs)
    tms = [min(tm, grp[0].shape[0]) for grp in row_groups]
    nblk = [grp[0].shape[0] // t for grp, t in zip(row_groups, tms)]
    start = [sum(nblk[:g]) for g in range(n_groups)]

    def local(g):
        return lambda i: jnp.clip(i - start[g], 0, nblk[g] - 1)

    in_specs, operands, out_specs, out_shape = [], [], [], []
    for g, grp in enumerate(row_groups):
        for a in grp:
            assert a.shape[0] == nblk[g] * tms[g]
            in_specs.append(pl.BlockSpec((tms[g], a.shape[1]), lambda i, f=local(g): (f(i), 0)))
            operands.append(a)
    for a, spec in shared:
        in_specs.append(spec)
        operands.append(a)
    for g in range(n_groups):
        for cols, dtype, keep in outs:
            assert nblk[g] % keep[g] == 0
            out_specs.append(pl.BlockSpec((tms[g], cols), lambda i, f=local(g), k=keep[g]: (f(i) // k, 0)))
            out_shape.append(jax.ShapeDtypeStruct((nblk[g] // keep[g] * tms[g], cols), dtype))

    def per_group(*refs):
        ins, sh = refs[:n_groups * n_in], refs[n_groups * n_in:n_groups * n_in + len(shared)]
        out_refs = refs[n_groups * n_in + len(shared):]
        i = pl.program_id(0)
        for g in range(n_groups):
            pl.when((i >= start[g]) & (i < start[g] + nblk[g]))(functools.partial(
                body, *ins[g * n_in:(g + 1) * n_in], *sh, *out_refs[g * n_out:(g + 1) * n_out]))

    res = pl.pallas_call(
        per_group,
        grid=(sum(nblk),),
        in_specs=in_specs,
        out_specs=out_specs,
        out_shape=out_shape,
        compiler_params=_params("arbitrary"),
    )(*operands)
    return [res[g * n_out:(g + 1) * n_out] for g in range(n_groups)]


def _norm_proj(xs, g, w, layer, tm, split):
    d = xs[0].shape[1]
    every = [1] * len(xs)
    return _rowwise_call(
        _proj_body, [[x] for x in xs],
        [(g, _resident((1, d))), (w, _resident_layer(w, layer))],
        [(split, F32, every), (w.shape[2] - split, BF16, every)], tm)


def _qkv_body(x_ref, g_ref, w_ref, pool_ref, poolt_ref, gq_ref, gk_ref,
              q_ref, k_ref, v_ref, kt_ref, vt_ref, *, att_dim, q_scale):
    def head_norm(a, gt):
        ms = _dot((a * a).astype(BF16), pool_ref[...]) * (1.0 / ATT_HEAD_DIM)
        rb = _dot(lax.rsqrt(ms + EPS).astype(BF16), poolt_ref[...])
        return a * rb * gt

    rows = _row_halves(x_ref)
    h = [_rms_rows(x_ref[r, :], g_ref[...]).astype(BF16) for r in rows]
    qkv = [_dot(hr, w_ref[...]) for hr in h]
    for r, y in zip(rows, qkv):
        q_ref[r, :] = (head_norm(y[:, :att_dim], gq_ref[...]) * q_scale).astype(BF16)
        kn = head_norm(y[:, att_dim:2 * att_dim], gk_ref[...])
        v = y[:, 2 * att_dim:]
        k_ref[r, :] = kn.astype(BF16)
        v_ref[r, :] = v.astype(BF16)
        kt_ref[r, :] = kn
        vt_ref[r, :] = v


def _norm_qkv(xs, g, w, layer, pool, poolt, gq, gk, tm, tail_divs):
    d = xs[0].shape[1]
    att_dim = w.shape[2] // 3
    every = [1] * len(xs)
    return _rowwise_call(
        functools.partial(_qkv_body, att_dim=att_dim, q_scale=ATT_HEAD_DIM ** -0.5),
        [[x] for x in xs],
        [(g, _resident((1, d))), (w, _resident_layer(w, layer)), (pool, _resident(pool.shape)),
         (poolt, _resident(poolt.shape)), (gq, _resident((1, att_dim))), (gk, _resident((1, att_dim)))],
        [(att_dim, BF16, every)] * 3 + [(att_dim, F32, tail_divs)] * 2, tm)


def _post_body(x_ref, a_ref, wo_ref, g_ref, win_ref, wout_ref, o_ref, *, d_ff, edges):
    rows = _row_halves(x_ref)
    acc = [x_ref[r, :] + _dot(a_ref[r, :], wo_ref[...]) for r in rows]
    h = [_rms_rows(x1, g_ref[...]).astype(BF16) for x1 in acc]
    for lo, hi in zip(edges[:-1], edges[1:]):
        for i in range(len(rows)):
            gate = _dot(h[i], win_ref[:, lo:hi])
            up = _dot(h[i], win_ref[:, d_ff + lo:d_ff + hi])
            act = (_silu(gate) * up).astype(BF16)
            acc[i] = acc[i] + _dot(act, wout_ref[lo:hi, :])
    for i, r in enumerate(rows):
        o_ref[r, :] = acc[i]


def _ffn_edges(d_ff, n_chunks):
    assert d_ff % MXU_DIM == 0
    tiles = d_ff // MXU_DIM
    return tuple(MXU_DIM * ((tiles * c + n_chunks - 1) // n_chunks) for c in range(n_chunks + 1))


def _post_mixer(xs, mixes, wo, wo_layer, g, win, wout, layer, tm, n_chunks):
    d = xs[0].shape[1]
    d_ff = wout.shape[1]
    res = _rowwise_call(
        functools.partial(_post_body, d_ff=d_ff, edges=_ffn_edges(d_ff, n_chunks)),
        [[x, a] for x, a in zip(xs, mixes)],
        [(wo, _resident_layer(wo, wo_layer)), (g, _resident((1, d))),
         (win, _resident_layer(win, layer)), (wout, _resident_layer(wout, layer))],
        [(d, F32, [1] * len(xs))], tm)
    return [r[0] for r in res]


def _mixer_body(pa_ref, p_ref, cw_ref, w2_ref, gb_ref, on_ref, conv0_ref, st0_ref,
                y_ref, convo_ref, sto_ref, ubuf, st_sc, *, tb, cdim, dk, dv):
    hk = dk // GLA_HEADS
    hv = dv // GLA_HEADS
    o_q = 0
    o_k = o_q + dk
    o_v = o_k + dk
    o_g = o_v + dv
    o_l = o_g + dv
    t = pl.program_id(1)

    @pl.when(t == 0)
    def _():
        ubuf[0:8, :] = jnp.zeros((8, cdim), F32)
        ubuf[6:8, :] = conv0_ref[...]
        st_sc[...] = st0_ref[...]

    u = pa_ref[:, 0:cdim] * pa_ref[:, 2 * cdim:3 * cdim]
    ubuf[8:8 + tb, :] = u
    y = (cw_ref[0:1, :] * ubuf[6:6 + tb, :] + cw_ref[1:2, :] * ubuf[7:7 + tb, :]
         + cw_ref[2:3, :] * u)
    y_ref[:, 0:cdim] = (pa_ref[:, cdim:2 * cdim] * y).astype(y_ref.dtype)
    tail = ubuf[tb + 6:tb + 8, :]
    ubuf[6:8, :] = tail
    convo_ref[...] = tail

    L = CHUNK
    assert hk == L and L & (L - 1) == 0
    sh = L.bit_length() - 1
    nst = GLA_HEADS * L
    r_i = lax.broadcasted_iota(jnp.int32, (nst, dk), 0)
    c_i = lax.broadcasted_iota(jnp.int32, (nst, dk), 1)
    same_head = (r_i >> sh) == (c_i >> sh)
    a_r = lax.broadcasted_iota(jnp.int32, (nst, nst), 0)
    a_c = lax.broadcasted_iota(jnp.int32, (nst, nst), 1)
    att_keep = ((a_r >> sh) == (a_c >> sh)) & ((a_c & (L - 1)) <= (a_r & (L - 1)))
    tri = (lax.broadcasted_iota(jnp.int32, (L, L), 1)
           <= lax.broadcasted_iota(jnp.int32, (L, L), 0)).astype(BF16)
    w2 = w2_ref[...]
    gb = gb_ref[...]
    onorm = on_ref[...]

    rows = [slice(c * L, (c + 1) * L) for c in range(tb // L)]

    def stack(a):
        return jnp.concatenate([a] * GLA_HEADS, axis=0)

    def heads_to_rows(o0, r):
        return jnp.concatenate([p_ref[r, o0 + h * hv:o0 + (h + 1) * hv] for h in range(GLA_HEADS)], axis=0)

    gk = _dot(p_ref[:, o_l:o_l + LANES], w2) + gb
    la = (jnp.minimum(gk, 0.0) - jnp.log(1.0 + jnp.exp(-jnp.abs(gk)))) * (1.0 / GATE_NORM)
    l_hi, l_lo = _split2(la)
    b = [_dot(tri, l_hi[r]) + _dot(tri, l_lo[r]) for r in rows]
    b_last = [x[L - 1:L, :] for x in b]
    q_st, k_tl, kd_bd, v_st = [], [], [], []
    for c, r in enumerate(rows):
        q = p_ref[r, o_q:o_k].astype(F32) * (hk ** -0.5)
        k = p_ref[r, o_k:o_v].astype(F32)
        q_st.append(jnp.where(same_head, stack(q * jnp.exp(b[c])), 0.0).astype(BF16))
        k_tl.append(stack((k * jnp.exp(-b[c])).astype(BF16)))
        kd_bd.append(jnp.where(same_head, stack(k * jnp.exp(b_last[c] - b[c])), 0.0).astype(BF16))
        v_st.append(heads_to_rows(o_v, r))
    att = [jnp.where(att_keep, _dot_nt(q_st[c], k_tl[c]), 0.0).astype(BF16)
           for c in range(len(rows))]
    upd = [_dot_tn(v_st[c], kd_bd[c]) for c in range(len(rows))]
    o = [_dot(att[c], v_st[c]) for c in range(len(rows))]
    st = st_sc[...]
    for c in range(len(rows)):
        o[c] = o[c] + _dot_nt(q_st[c], st.astype(BF16))
        st = st * jnp.exp(b_last[c]) + upd[c]
    for c, r in enumerate(rows):
        oc = _rms_rows(o[c], onorm) * _silu(heads_to_rows(o_g, r).astype(F32))
        for h in range(GLA_HEADS):
            y_ref[r, cdim + h * hv:cdim + (h + 1) * hv] = oc[h * L:(h + 1) * L, :].astype(y_ref.dtype)
    st_sc[...] = st
    sto_ref[...] = st


def _mixer_ab(proj_a, proj_b, conv_w, w2pad, gb, onorm, conv0, st0, tb):
    bsz, t, _ = proj_a.shape
    cdim = conv_w.shape[1]
    dk = w2pad.shape[1]
    hv = onorm.shape[1]
    dv = hv * GLA_HEADS
    return pl.pallas_call(
        functools.partial(_mixer_body, tb=tb, cdim=cdim, dk=dk, dv=dv),
        grid=(bsz, t // tb),
        in_specs=[
            pl.BlockSpec((None, tb, proj_a.shape[2]), lambda b, i: (b, i, 0)),
            pl.BlockSpec((None, tb, proj_b.shape[2]), lambda b, i: (b, i, 0)),
            _resident(conv_w.shape),
            _resident(w2pad.shape),
            _resident(gb.shape),
            _resident(onorm.shape),
            pl.BlockSpec((None, CONV_WIDTH - 1, cdim), lambda b, i: (b, 0, 0)),
            pl.BlockSpec((None, hv, dk), lambda b, i: (b, 0, 0)),
        ],
        out_specs=[
            pl.BlockSpec((None, tb, cdim + dv), lambda b, i: (b, i, 0)),
            pl.BlockSpec((None, CONV_WIDTH - 1, cdim), lambda b, i: (b, 0, 0)),
            pl.BlockSpec((None, hv, dk), lambda b, i: (b, 0, 0)),
        ],
        out_shape=[
            jax.ShapeDtypeStruct((bsz, t, cdim + dv), BF16),
            jax.ShapeDtypeStruct((bsz, CONV_WIDTH - 1, cdim), F32),
            jax.ShapeDtypeStruct((bsz, hv, dk), F32),
        ],
        scratch_shapes=[
            pltpu.VMEM((tb + 8, cdim), F32),
            pltpu.VMEM((hv, dk), F32),
        ],
        compiler_params=_params("parallel", "arbitrary"),
    )(proj_a, proj_b, conv_w, w2pad, gb, onorm, conv0, st0)


def _attn_body(q_ref, ka_ref, kb_ref, va_ref, vb_ref, brow_ref, o_ref, tab_ref,
               *, tq, tqc, hist_is_padding):
    nwin = BAND_ROWS + tqc

    @pl.when((pl.program_id(0) == 0) & (pl.program_id(1) == 0))
    def _build_table():
        wrow = brow_ref.shape[1]
        sh = CHUNK.bit_length() - 1
        sub = lax.broadcasted_iota(jnp.int32, (SUBLANES, wrow), 0)
        q_chunk = lax.broadcasted_iota(jnp.int32, (tqc, nwin), 0) >> sh
        k_chunk = (lax.broadcasted_iota(jnp.int32, (tqc, nwin), 1) >> sh) - BAND_ROWS // CHUNK
        allowed = (k_chunk <= q_chunk) & (k_chunk >= q_chunk - BAND_ROWS // CHUNK)

        def one_head(h, carry):
            y = jnp.broadcast_to(brow_ref[pl.ds(h, 1), :], (SUBLANES, wrow))
            for bit in range(SUBLANES.bit_length() - 1):
                y = jnp.where(((sub >> bit) & 1) == 1, pltpu.roll(y, 1 << bit, axis=1), y)
            y = jnp.concatenate([y] + [pltpu.roll(y, SUBLANES * a, axis=1)
                                       for a in range(1, tqc // SUBLANES)], axis=0)
            slack = _table_slack(tqc)
            t = jnp.where(allowed, y[:, slack:slack + nwin], NEG_INF)
            tab_ref[h >> 1, pl.ds(pl.multiple_of((h & 1) * tqc, tqc), tqc), :] = t
            return carry

        lax.fori_loop(0, 2 * tab_ref.shape[0], one_head, 0)

    n_pairs = q_ref.shape[1] // LANES
    lo = lax.broadcasted_iota(jnp.int32, (tqc, LANES), 1) < ATT_HEAD_DIM
    keep_q = ((lax.broadcasted_iota(jnp.int32, (2 * tqc, LANES), 1) < ATT_HEAD_DIM)
              == (lax.broadcasted_iota(jnp.int32, (2 * tqc, LANES), 0) < tqc))

    def run(first_block):
        def window(j):
            r0 = j * tqc
            hist = slice(BAND_ROWS, BAND_ROWS) if first_block else slice(min(r0, BAND_ROWS), BAND_ROWS)
            cur = slice(max(r0 - BAND_ROWS, 0), r0 + tqc)
            c0 = max(BAND_ROWS - r0, 0) if first_block else 0
            return r0, hist, cur, c0

        def scores(j, p):
            r0, hist, cur, c0 = window(j)
            lanes = slice(p * LANES, (p + 1) * LANES)
            qp = q_ref[r0:r0 + tqc, lanes]
            q2 = jnp.where(keep_q, jnp.concatenate([qp, qp], axis=0), 0.0)
            s = _dot_nt(q2, kb_ref[cur, lanes])
            if hist.stop > hist.start:
                s = jnp.concatenate([_dot_nt(q2, ka_ref[hist, lanes].astype(BF16)), s], axis=1)
            return s + tab_ref[p, :, c0:]

        def weighted_values(j, p, e, den):
            r0, hist, cur, c0 = window(j)
            lanes = slice(p * LANES, (p + 1) * LANES)
            n_hist = hist.stop - hist.start
            pv = _dot(e[:, n_hist:], vb_ref[cur, lanes])
            if n_hist:
                pv = pv + _dot(e[:, :n_hist], va_ref[hist, lanes].astype(BF16))
            pv = pv / den
            o_ref[r0:r0 + tqc, lanes] = jnp.where(lo, pv[:tqc], pv[tqc:]).astype(o_ref.dtype)

        for j in range(tq // tqc):
            s_next = scores(j, 0)
            pending = None
            for p in range(n_pairs):
                s = s_next
                if p + 1 < n_pairs:
                    s_next = scores(j, p + 1)
                e = jnp.exp((s - jnp.max(s, axis=-1, keepdims=True)).astype(BF16))
                den = jnp.sum(e.astype(F32), axis=-1, keepdims=True)
                if pending is not None:
                    weighted_values(*pending)
                pending = (j, p, e, den)
            weighted_values(*pending)

    if hist_is_padding:
        first = pl.program_id(1) == 0
        pl.when(first)(lambda: run(True))
        pl.when(jnp.logical_not(first))(lambda: run(False))
    else:
        run(False)


def _band_attention(q, k_hist, k_cur, v_hist, v_cur, layer, brow, tq, tqc):
    bsz, t, width = q.shape
    assert t % tq == 0 and tq % tqc == 0
    hist_is_padding = k_hist is None
    if hist_is_padding:
        assert tq == BAND_ROWS
        k_hist, v_hist = k_cur, v_cur
        hist_spec = pl.BlockSpec((None, BAND_ROWS, width), lambda b, i: (b, jnp.maximum(i - 1, 0), 0))
    else:
        assert t == tq and k_hist.shape[2] == BAND_ROWS
        hist_spec = pl.BlockSpec((None, None, BAND_ROWS, width), lambda b, i: (layer, b, 0, 0))
    cur_map = lambda b, i: (b, i, 0)
    return pl.pallas_call(
        functools.partial(_attn_body, tq=tq, tqc=tqc, hist_is_padding=hist_is_padding),
        grid=(bsz, t // tq),
        in_specs=[
            pl.BlockSpec((None, tq, width), cur_map),
            hist_spec,
            pl.BlockSpec((None, tq, width), cur_map),
            hist_spec,
            pl.BlockSpec((None, tq, width), cur_map),
            _resident(brow.shape),
        ],
        out_specs=pl.BlockSpec((None, tq, width), cur_map),
        out_shape=jax.ShapeDtypeStruct((bsz, t, width), BF16),
        scratch_shapes=[
            pltpu.VMEM((width // LANES, 2 * tqc, BAND_ROWS + tqc), F32),
        ],
        compiler_params=_params("arbitrary", "arbitrary"),
    )(q, k_hist, k_cur, v_hist, v_cur, brow)


def _table_slack(tqc):
    return -(-tqc // LANES) * LANES


def _bias_row(rel_bias, tqc):
    slack = _table_slack(tqc)
    width = -(-(slack + BAND_ROWS + tqc) // LANES) * LANES
    rel = np.clip(BAND_ROWS + slack - np.arange(width), -(CHUNK - 1), MAX_REL) + (CHUNK - 1)
    return rel_bias[:, rel].astype(F32)


class _Group:
    def __init__(self, x, conv_prev, gla_prev, k_hist, v_hist):
        self.bsz, self.t, d = x.shape
        self.xf = x.reshape(-1, d)
        self.conv_prev, self.gla_prev = conv_prev, gla_prev
        self.k_hist, self.v_hist = k_hist, v_hist
        self.conv_new, self.gla_new, self.k_new, self.v_new = [], [], [], []

    @property
    def fresh(self):
        return self.k_hist is None


def _trunk(groups, w):
    d = groups[0].xf.shape[1]
    depth = w["norm_mix"].shape[0]
    heads = w["rel_bias"].shape[1]
    att_dim = heads * ATT_HEAD_DIM
    tm = BAND_ROWS
    tail_divs = []
    for grp in groups:
        if grp.fresh:
            assert grp.t % tm == 0
            tail_divs.append(grp.t // tm)
        else:
            assert grp.bsz * grp.t == tm and grp.k_hist.shape[2] == BAND_ROWS and grp.t <= BAND_ROWS
            tail_divs.append(1)
            grp.kh = grp.k_hist.reshape(grp.k_hist.shape[:3] + (att_dim,))
            grp.vh = grp.v_hist.reshape(grp.v_hist.shape[:3] + (att_dim,))
    for layer in range(depth):
        g_mix = w["norm_mix"][layer][None, :]
        g_ffn = w["norm_ffn"][layer][None, :]
        xs = [grp.xf for grp in groups]
        mixes = []
        if layer % 2 == 0:
            e = layer // 2
            projs = _norm_proj(xs, g_mix, w["w_in_ab"], e, 2 * tm, 3 * w["conv_w"].shape[2])
            for grp, (proj_a, proj_b) in zip(groups, projs):
                st0 = grp.gla_prev[e].reshape(grp.bsz, -1, grp.gla_prev.shape[-1]).swapaxes(1, 2)
                mix, cs, st = _mixer_ab(
                    proj_a.reshape(grp.bsz, grp.t, -1), proj_b.reshape(grp.bsz, grp.t, -1),
                    w["conv_w"][e], w["gk_w2"][e],
                    w["gk_b"][e][None, :], w["gla_onorm"][e][None, :], grp.conv_prev[e], st0,
                    min(2 * BAND_ROWS, grp.t))
                grp.conv_new.append(cs)
                grp.gla_new.append(st.swapaxes(1, 2).reshape(grp.gla_prev.shape[1:]))
                mixes.append(mix.reshape(-1, mix.shape[-1]))
            w_o, w_o_layer = w["w_out_ab"], e
        else:
            o = layer // 2
            qkvs = _norm_qkv(xs, g_mix, w["w_qkv"], o, w["pool"], w["poolt"],
                             w["q_norm"][o], w["k_norm"][o], tm, tail_divs)
            for grp, (q, k, v, kt, vt) in zip(groups, qkvs):
                shp = (grp.bsz, grp.t, att_dim)
                tqc = min(2 * CHUNK, grp.t)
                att = _band_attention(
                    q.reshape(shp), None if grp.fresh else grp.kh, k.reshape(shp),
                    None if grp.fresh else grp.vh, v.reshape(shp), o,
                    _bias_row(w["rel_bias"][o], tqc), min(BAND_ROWS, grp.t), tqc)
                grp.k_new.append(kt)
                grp.v_new.append(vt)
                mixes.append(att.reshape(-1, att_dim))
            w_o, w_o_layer = w["w_o_att"], o
        outs = _post_mixer(xs, mixes, w_o, w_o_layer, g_ffn, w["w_ffn_in"], w["w_ffn_out"], layer, tm, 2)
        for grp, xf in zip(groups, outs):
            grp.xf = xf

    def cache(grp, tails, hist):
        new = jnp.stack(tails).reshape(len(tails), grp.bsz, -1, heads, ATT_HEAD_DIM)
        if grp.fresh:
            return new
        return jnp.concatenate([hist[:, :, grp.t:], new], axis=2)

    return [(grp.xf.reshape(grp.bsz, grp.t, d), jnp.stack(grp.conv_new), jnp.stack(grp.gla_new),
             cache(grp, grp.k_new, grp.k_hist), cache(grp, grp.v_new, grp.v_hist)) for grp in groups]


def kernel(x_prompt, x_sample, state_conv, state_gla, cache_k, cache_v, norm_mix, norm_ffn,
           w_in_ab, conv_w, gla_gk_w2, gla_gk_b, gla_onorm, w_out_ab, w_qkv, q_norm, k_norm,
           rel_bias, w_o_att, w_ffn_in, w_ffn_out):
    bsz = x_prompt.shape[0]
    n_even = state_conv.shape[0]
    n_odd = cache_k.shape[0]
    heads, head_dim = cache_k.shape[-2:]
    att_dim = heads * head_dim
    in_ab = w_in_ab.shape[-1]
    in_pad = -in_ab % LANES
    rank = gla_gk_w2.shape[1]
    pool = (np.arange(att_dim)[:, None] // head_dim == np.arange(LANES)[None, :])
    w = {
        "norm_mix": norm_mix, "norm_ffn": norm_ffn,
        "w_in_ab": jnp.pad(w_in_ab, ((0, 0), (0, 0), (0, in_pad))).astype(BF16),
        "conv_w": conv_w,
        "gk_w2": jnp.pad(gla_gk_w2, ((0, 0), (0, LANES - rank), (0, 0))).astype(BF16),
        "gk_b": gla_gk_b, "gla_onorm": gla_onorm,
        "w_out_ab": w_out_ab.astype(BF16),
        "w_qkv": w_qkv.astype(BF16),
        "q_norm": jnp.tile(q_norm, (1, heads))[:, None, :],
        "k_norm": jnp.tile(k_norm, (1, heads))[:, None, :],
        "rel_bias": rel_bias,
        "w_o_att": w_o_att.astype(BF16),
        "w_ffn_in": w_ffn_in.astype(BF16), "w_ffn_out": w_ffn_out.astype(BF16),
        "pool": jnp.asarray(pool, BF16), "poolt": jnp.asarray(pool.T, BF16),
    }
    conv0 = jnp.zeros((n_even, bsz) + state_conv.shape[2:], x_prompt.dtype)
    gla0 = jnp.zeros((n_even, bsz) + state_gla.shape[2:], F32)
    (y_p, conv_p, gla_p, k_p, v_p), (y_s, conv_s, gla_s, k_s, v_s) = _trunk(
        [_Group(x_prompt, conv0, gla0, None, None),
         _Group(x_sample, state_conv, state_gla, cache_k, cache_v)], w)
    return (y_p, y_s, conv_p, gla_p, k_p, v_p, conv_s, gla_s, k_s, v_s)
```

```python
import functools

import jax
import jax.numpy as jnp
import numpy as np
from jax import lax
from jax.experimental import pallas as pl
from jax.experimental.pallas import tpu as pltpu

F32 = jnp.float32
BF16 = jnp.bfloat16

EPS = 1e-6
NEG_INF = -1e30
CHUNK = 64
CONV_WIDTH = 3
GLA_HEADS = 4
GATE_NORM = 16.0
ATT_HEAD_DIM = 64
BAND_ROWS = 8 * CHUNK
MAX_REL = 256
LANES = 128
SUBLANES = 8
MXU_DIM = 256
VMEM_LIMIT = 56 * 1024 * 1024


def _params(*sem):
    return pltpu.CompilerParams(dimension_semantics=sem, vmem_limit_bytes=VMEM_LIMIT)


def _resident(shape):
    nd = len(shape)
    return pl.BlockSpec(shape, lambda *_: (0,) * nd, pipeline_mode=pl.Buffered(1))


def _resident_layer(stacked, layer):
    nd = stacked.ndim - 1
    return pl.BlockSpec((None,) + stacked.shape[1:], lambda *_: (layer,) + (0,) * nd,
                        pipeline_mode=pl.Buffered(1))


def _dot(a, b):
    return jnp.dot(a, b, preferred_element_type=F32)


def _dot_nt(a, b):
    return lax.dot_general(a, b, (((1,), (1,)), ((), ())), preferred_element_type=F32)


def _dot_tn(a, b):
    return lax.dot_general(a, b, (((0,), (0,)), ((), ())), preferred_element_type=F32)


def _split2(a):
    hi = a.astype(BF16)
    lo = (a - hi.astype(F32)).astype(BF16)
    return hi, lo


def _rms_rows(x, g):
    ms = jnp.mean(x * x, axis=-1, keepdims=True)
    return x * lax.rsqrt(ms + EPS) * g


def _silu(x):
    return x * jax.nn.sigmoid(x)


def _row_halves(ref):
    half = ref.shape[0] // 2
    return [slice(0, half), slice(half, 2 * half)]


def _proj_body(x_ref, g_ref, w_ref, oa_ref, ob_ref):
    split = oa_ref.shape[1]
    rows = _row_halves(x_ref)
    h = [_rms_rows(x_ref[r, :], g_ref[...]).astype(BF16) for r in rows]
    for r, hr in zip(rows, h):
        y = _dot(hr, w_ref[...])
        oa_ref[r, :] = y[:, :split]
        ob_ref[r, :] = y[:, split:].astype(ob_ref.dtype)


def _rowwise_call(body, row_groups, shared, outs, tm):
    n_groups, n_in, n_out = len(row_groups), len(row_groups[0]), len(outs)
    tms = [min(tm, grp[0].shape[0]) for grp in row_groups]
    nblk = [grp[0].shape[0] // t for grp, t in zip(row_groups, tms)]
    start = [sum(nblk[:g]) for g in range(n_groups)]

    def local(g):
        return lambda i: jnp.clip(i - start[g], 0, nblk[g] - 1)

    in_specs, operands, out_specs, out_shape = [], [], [], []
    for g, grp in enumerate(row_groups):
        for a in grp:
            assert a.shape[0] == nblk[g] * tms[g]
            in_specs.append(pl.BlockSpec((tms[g], a.shape[1]), lambda i, f=local(g): (f(i), 0)))
            operands.append(a)
    for a, spec in shared:
        in_specs.append(spec)
        operands.append(a)
    for g in range(n_groups):
        for cols, dtype, keep in outs:
            assert nblk[g] % keep[g] == 0
            out_specs.append(pl.BlockSpec((tms[g], cols), lambda i, f=local(g), k=keep[g]: (f(i) // k, 0)))
            out_shape.append(jax.ShapeDtypeStruct((nblk[g] // keep[g] * tms[g], cols), dtype))

    def per_group(*refs):
        ins, sh = refs[:n_groups * n_in], refs[n_groups * n_in:n_groups * n_in + len(shared)]
        out_refs = refs[n_groups * n_in + len(shared):]
        i = pl.program_id(0)
        for g in range(n_groups):
            pl.when((i >= start[g]) & (i < start[g] + nblk[g]))(functools.partial(
                body, *ins[g * n_in:(g + 1) * n_in], *sh, *out_refs[g * n_out:(g + 1) * n_out]))

    res = pl.pallas_call(
        per_group,
        grid=(sum(nblk),),
        in_specs=in_specs,
        out_specs=out_specs,
        out_shape=out_shape,
        compiler_params=_params("arbitrary"),
    )(*operands)
    return [res[g * n_out:(g + 1) * n_out] for g in range(n_groups)]


def _norm_proj(xs, g, w, layer, tm, split):
    d = xs[0].shape[1]
    every = [1] * len(xs)
    return _rowwise_call(
        _proj_body, [[x] for x in xs],
        [(g, _resident((1, d))), (w, _resident_layer(w, layer))],
        [(split, F32, every), (w.shape[2] - split, BF16, every)], tm)


def _qkv_body(x_ref, g_ref, w_ref, pool_ref, poolt_ref, gq_ref, gk_ref,
              q_ref, k_ref, v_ref, kt_ref, vt_ref, *, att_dim, q_scale):
    def head_norm(a, gt):
        ms = _dot((a * a).astype(BF16), pool_ref[...]) * (1.0 / ATT_HEAD_DIM)
        rb = _dot(lax.rsqrt(ms + EPS).astype(BF16), poolt_ref[...])
        return a * rb * gt

    rows = _row_halves(x_ref)
    h = [_rms_rows(x_ref[r, :], g_ref[...]).astype(BF16) for r in rows]
    qkv = [_dot(hr, w_ref[...]) for hr in h]
    for r, y in zip(rows, qkv):
        q_ref[r, :] = (head_norm(y[:, :att_dim], gq_ref[...]) * q_scale).astype(BF16)
        kn = head_norm(y[:, att_dim:2 * att_dim], gk_ref[...])
        v = y[:, 2 * att_dim:]
        k_ref[r, :] = kn.astype(BF16)
        v_ref[r, :] = v.astype(BF16)
        kt_ref[r, :] = kn
        vt_ref[r, :] = v


def _norm_qkv(xs, g, w, layer, pool, poolt, gq, gk, tm, tail_divs):
    d = xs[0].shape[1]
    att_dim = w.shape[2] // 3
    every = [1] * len(xs)
    return _rowwise_call(
        functools.partial(_qkv_body, att_dim=att_dim, q_scale=ATT_HEAD_DIM ** -0.5),
        [[x] for x in xs],
        [(g, _resident((1, d))), (w, _resident_layer(w, layer)), (pool, _resident(pool.shape)),
         (poolt, _resident(poolt.shape)), (gq, _resident((1, att_dim))), (gk, _resident((1, att_dim)))],
        [(att_dim, BF16, every)] * 3 + [(att_dim, F32, tail_divs)] * 2, tm)


def _post_body(x_ref, a_ref, wo_ref, g_ref, win_ref, wout_ref, o_ref, *, d_ff, edges):
    rows = _row_halves(x_ref)
    acc = [x_ref[r, :] + _dot(a_ref[r, :], wo_ref[...]) for r in rows]
    h = [_rms_rows(x1, g_ref[...]).astype(BF16) for x1 in acc]
    for lo, hi in zip(edges[:-1], edges[1:]):
        for i in range(len(rows)):
            gate = _dot(h[i], win_ref[:, lo:hi])
            up = _dot(h[i], win_ref[:, d_ff + lo:d_ff + hi])
            act = (_silu(gate) * up).astype(BF16)
            acc[i] = acc[i] + _dot(act, wout_ref[lo:hi, :])
    for i, r in enumerate(rows):
        o_ref[r, :] = acc[i]


def _ffn_edges(d_ff, n_chunks):
    assert d_ff % MXU_DIM == 0
    tiles = d_ff // MXU_DIM
    return tuple(MXU_DIM * ((tiles * c + n_chunks - 1) // n_chunks) for c in range(n_chunks + 1))


def _post_mixer(xs, mixes, wo, wo_layer, g, win, wout, layer, tm, n_chunks):
    d = xs[0].shape[1]
    d_ff = wout.shape[1]
    res = _rowwise_call(
        functools.partial(_post_body, d_ff=d_ff, edges=_ffn_edges(d_ff, n_chunks)),
        [[x, a] for x, a in zip(xs, mixes)],
        [(wo, _resident_layer(wo, wo_layer)), (g, _resident((1, d))),
         (win, _resident_layer(win, layer)), (wout, _resident_layer(wout, layer))],
        [(d, F32, [1] * len(xs))], tm)
    return [r[0] for r in res]


def _mixer_body(pa_ref, p_ref, cw_ref, w2_ref, gb_ref, on_ref, conv0_ref, st0_ref,
                y_ref, convo_ref, sto_ref, ubuf, st_sc, *, tb, cdim, dk, dv):
    hk = dk // GLA_HEADS
    hv = dv // GLA_HEADS
    o_q = 0
    o_k = o_q + dk
    o_v = o_k + dk
    o_g = o_v + dv
    o_l = o_g + dv
    t = pl.program_id(1)

    @pl.when(t == 0)
    def _():
        ubuf[0:8, :] = jnp.zeros((8, cdim), F32)
        ubuf[6:8, :] = conv0_ref[...]
        st_sc[...] = st0_ref[...]

    u = pa_ref[:, 0:cdim] * pa_ref[:, 2 * cdim:3 * cdim]
    ubuf[8:8 + tb, :] = u
    y = (cw_ref[0:1, :] * ubuf[6:6 + tb, :] + cw_ref[1:2, :] * ubuf[7:7 + tb, :]
         + cw_ref[2:3, :] * u)
    y_ref[:, 0:cdim] = (pa_ref[:, cdim:2 * cdim] * y).astype(y_ref.dtype)
    tail = ubuf[tb + 6:tb + 8, :]
    ubuf[6:8, :] = tail
    convo_ref[...] = tail

    L = CHUNK
    assert hk == L and L & (L - 1) == 0
    sh = L.bit_length() - 1
    nst = GLA_HEADS * L
    r_i = lax.broadcasted_iota(jnp.int32, (nst, dk), 0)
    c_i = lax.broadcasted_iota(jnp.int32, (nst, dk), 1)
    same_head = (r_i >> sh) == (c_i >> sh)
    a_r = lax.broadcasted_iota(jnp.int32, (nst, nst), 0)
    a_c = lax.broadcasted_iota(jnp.int32, (nst, nst), 1)
    att_keep = ((a_r >> sh) == (a_c >> sh)) & ((a_c & (L - 1)) <= (a_r & (L - 1)))
    tri = (lax.broadcasted_iota(jnp.int32, (L, L), 1)
           <= lax.broadcasted_iota(jnp.int32, (L, L), 0)).astype(BF16)
    w2 = w2_ref[...]
    gb = gb_ref[...]
    onorm = on_ref[...]

    rows = [slice(c * L, (c + 1) * L) for c in range(tb // L)]

    def stack(a):
        return jnp.concatenate([a] * GLA_HEADS, axis=0)

    def heads_to_rows(o0, r):
        return jnp.concatenate([p_ref[r, o0 + h * hv:o0 + (h + 1) * hv] for h in range(GLA_HEADS)], axis=0)

    gk = _dot(p_ref[:, o_l:o_l + LANES], w2) + gb
    la = (jnp.minimum(gk, 0.0) - jnp.log(1.0 + jnp.exp(-jnp.abs(gk)))) * (1.0 / GATE_NORM)
    l_hi, l_lo = _split2(la)
    b = [_dot(tri, l_hi[r]) + _dot(tri, l_lo[r]) for r in rows]
    b_last = [x[L - 1:L, :] for x in b]
    q_st, k_tl, kd_bd, v_st = [], [], [], []
    for c, r in enumerate(rows):
        q = p_ref[r, o_q:o_k].astype(F32) * (hk ** -0.5)
        k = p_ref[r, o_k:o_v].astype(F32)
        q_st.append(jnp.where(same_head, stack(q * jnp.exp(b[c])), 0.0).astype(BF16))
        k_tl.append(stack((k * jnp.exp(-b[c])).astype(BF16)))
        kd_bd.append(jnp.where(same_head, stack(k * jnp.exp(b_last[c] - b[c])), 0.0).astype(BF16))
        v_st.append(heads_to_rows(o_v, r))
    att = [jnp.where(att_keep, _dot_nt(q_st[c], k_tl[c]), 0.0).astype(BF16)
           for c in range(len(rows))]
    upd = [_dot_tn(v_st[c], kd_bd[c]) for c in range(len(rows))]
    o = [_dot(att[c], v_st[c]) for c in range(len(rows))]
    st = st_sc[...]
    for c in range(len(rows)):
        o[c] = o[c] + _dot_nt(q_st[c], st.astype(BF16))
        st = st * jnp.exp(b_last[c]) + upd[c]
    for c, r in enumerate(rows):
        oc = _rms_rows(o[c], onorm) * _silu(heads_to_rows(o_g, r).astype(F32))
        for h in range(GLA_HEADS):
            y_ref[r, cdim + h * hv:cdim + (h + 1) * hv] = oc[h * L:(h + 1) * L, :].astype(y_ref.dtype)
    st_sc[...] = st
    sto_ref[...] = st


def _mixer_ab(proj_a, proj_b, conv_w, w2pad, gb, onorm, conv0, st0, tb):
    bsz, t, _ = proj_a.shape
    cdim = conv_w.shape[1]
    dk = w2pad.shape[1]
    hv = onorm.shape[1]
    dv = hv * GLA_HEADS
    return pl.pallas_call(
        functools.partial(_mixer_body, tb=tb, cdim=cdim, dk=dk, dv=dv),
        grid=(bsz, t // tb),
        in_specs=[
            pl.BlockSpec((None, tb, proj_a.shape[2]), lambda b, i: (b, i, 0)),
            pl.BlockSpec((None, tb, proj_b.shape[2]), lambda b, i: (b, i, 0)),
            _resident(conv_w.shape),
            _resident(w2pad.shape),
            _resident(gb.shape),
            _resident(onorm.shape),
            pl.BlockSpec((None, CONV_WIDTH - 1, cdim), lambda b, i: (b, 0, 0)),
            pl.BlockSpec((None, hv, dk), lambda b, i: (b, 0, 0)),
        ],
        out_specs=[
            pl.BlockSpec((None, tb, cdim + dv), lambda b, i: (b, i, 0)),
            pl.BlockSpec((None, CONV_WIDTH - 1, cdim), lambda b, i: (b, 0, 0)),
            pl.BlockSpec((None, hv, dk), lambda b, i: (b, 0, 0)),
        ],
        out_shape=[
            jax.ShapeDtypeStruct((bsz, t, cdim + dv), BF16),
            jax.ShapeDtypeStruct((bsz, CONV_WIDTH - 1, cdim), F32),
            jax.ShapeDtypeStruct((bsz, hv, dk), F32),
        ],
        scratch_shapes=[
            pltpu.VMEM((tb + 8, cdim), F32),
            pltpu.VMEM((hv, dk), F32),
        ],
        compiler_params=_params("parallel", "arbitrary"),
    )(proj_a, proj_b, conv_w, w2pad, gb, onorm, conv0, st0)


def _attn_body(q_ref, ka_ref, kb_ref, va_ref, vb_ref, brow_ref, o_ref, tab_ref,
               *, tq, tqc, hist_is_padding):
    nwin = BAND_ROWS + tqc

    @pl.when((pl.program_id(0) == 0) & (pl.program_id(1) == 0))
    def _build_table():
        wrow = brow_ref.shape[1]
        sh = CHUNK.bit_length() - 1
        sub = lax.broadcasted_iota(jnp.int32, (SUBLANES, wrow), 0)
        q_chunk = lax.broadcasted_iota(jnp.int32, (tqc, nwin), 0) >> sh
        k_chunk = (lax.broadcasted_iota(jnp.int32, (tqc, nwin), 1) >> sh) - BAND_ROWS // CHUNK
        allowed = (k_chunk <= q_chunk) & (k_chunk >= q_chunk - BAND_ROWS // CHUNK)

        def one_head(h, carry):
            y = jnp.broadcast_to(brow_ref[pl.ds(h, 1), :], (SUBLANES, wrow))
            for bit in range(SUBLANES.bit_length() - 1):
                y = jnp.where(((sub >> bit) & 1) == 1, pltpu.roll(y, 1 << bit, axis=1), y)
            y = jnp.concatenate([y] + [pltpu.roll(y, SUBLANES * a, axis=1)
                                       for a in range(1, tqc // SUBLANES)], axis=0)
            slack = _table_slack(tqc)
            t = jnp.where(allowed, y[:, slack:slack + nwin], NEG_INF)
            tab_ref[h >> 1, pl.ds(pl.multiple_of((h & 1) * tqc, tqc), tqc), :] = t
            return carry

        lax.fori_loop(0, 2 * tab_ref.shape[0], one_head, 0)

    n_pairs = q_ref.shape[1] // LANES
    lo = lax.broadcasted_iota(jnp.int32, (tqc, LANES), 1) < ATT_HEAD_DIM
    keep_q = ((lax.broadcasted_iota(jnp.int32, (2 * tqc, LANES), 1) < ATT_HEAD_DIM)
              == (lax.broadcasted_iota(jnp.int32, (2 * tqc, LANES), 0) < tqc))

    def run(first_block):
        def window(j):
            r0 = j * tqc
            hist = slice(BAND_ROWS, BAND_ROWS) if first_block else slice(min(r0, BAND_ROWS), BAND_ROWS)
            cur = slice(max(r0 - BAND_ROWS, 0), r0 + tqc)
            c0 = max(BAND_ROWS - r0, 0) if first_block else 0
            return r0, hist, cur, c0

        def scores(j, p):
            r0, hist, cur, c0 = window(j)
            lanes = slice(p * LANES, (p + 1) * LANES)
            qp = q_ref[r0:r0 + tqc, lanes]
            q2 = jnp.where(keep_q, jnp.concatenate([qp, qp], axis=0), 0.0)
            s = _dot_nt(q2, kb_ref[cur, lanes])
            if hist.stop > hist.start:
                s = jnp.concatenate([_dot_nt(q2, ka_ref[hist, lanes].astype(BF16)), s], axis=1)
            return s + tab_ref[p, :, c0:]

        def weighted_values(j, p, e, den):
            r0, hist, cur, c0 = window(j)
            lanes = slice(p * LANES, (p + 1) * LANES)
            n_hist = hist.stop - hist.start
            pv = _dot(e[:, n_hist:], vb_ref[cur, lanes])
            if n_hist:
                pv = pv + _dot(e[:, :n_hist], va_ref[hist, lanes].astype(BF16))
            pv = pv / den
            o_ref[r0:r0 + tqc, lanes] = jnp.where(lo, pv[:tqc], pv[tqc:]).astype(o_ref.dtype)

        for j in range(tq // tqc):
            s_next = scores(j, 0)
            pending = None
            for p in range(n_pairs):
                s = s_next
                if p + 1 < n_pairs:
                    s_next = scores(j, p + 1)
                e = jnp.exp((s - jnp.max(s, axis=-1, keepdims=True)).astype(BF16))
                den = jnp.sum(e.astype(F32), axis=-1, keepdims=True)
                if pending is not None:
                    weighted_values(*pending)
                pending = (j, p, e, den)
            weighted_values(*pending)

    if hist_is_padding:
        first = pl.program_id(1) == 0
        pl.when(first)(lambda: run(True))
        pl.when(jnp.logical_not(first))(lambda: run(False))
    else:
        run(False)


def _band_attention(q, k_hist, k_cur, v_hist, v_cur, layer, brow, tq, tqc):
    bsz, t, width = q.shape
    assert t % tq == 0 and tq % tqc == 0
    hist_is_padding = k_hist is None
    if hist_is_padding:
        assert tq % BAND_ROWS == 0
        k_hist, v_hist = k_cur, v_cur
        per_block = tq // BAND_ROWS
        hist_spec = pl.BlockSpec((None, BAND_ROWS, width),
                                 lambda b, i: (b, jnp.maximum(i * per_block - 1, 0), 0))
    else:
        assert t == tq and k_hist.shape[2] == BAND_ROWS
        hist_spec = pl.BlockSpec((None, None, BAND_ROWS, width), lambda b, i: (layer, b, 0, 0))
    cur_map = lambda b, i: (b, i, 0)
    return pl.pallas_call(
        functools.partial(_attn_body, tq=tq, tqc=tqc, hist_is_padding=hist_is_padding),
        grid=(bsz, t // tq),
        in_specs=[
            pl.BlockSpec((None, tq, width), cur_map),
            hist_spec,
            pl.BlockSpec((None, tq, width), cur_map),
            hist_spec,
            pl.BlockSpec((None, tq, width), cur_map),
            _resident(brow.shape),
        ],
        out_specs=pl.BlockSpec((None, tq, width), cur_map),
        out_shape=jax.ShapeDtypeStruct((bsz, t, width), BF16),
        scratch_shapes=[
            pltpu.VMEM((width // LANES, 2 * tqc, BAND_ROWS + tqc), F32),
        ],
        compiler_params=_params("arbitrary", "arbitrary"),
    )(q, k_hist, k_cur, v_hist, v_cur, brow)


def _table_slack(tqc):
    return -(-tqc // LANES) * LANES


def _bias_row(rel_bias, tqc):
    slack = _table_slack(tqc)
    width = -(-(slack + BAND_ROWS + tqc) // LANES) * LANES
    rel = np.clip(BAND_ROWS + slack - np.arange(width), -(CHUNK - 1), MAX_REL) + (CHUNK - 1)
    return rel_bias[:, rel].astype(F32)


class _Group:
    def __init__(self, x, conv_prev, gla_prev, k_hist, v_hist):
        self.bsz, self.t, d = x.shape
        self.xf = x.reshape(-1, d)
        self.conv_prev, self.gla_prev = conv_prev, gla_prev
        self.k_hist, self.v_hist = k_hist, v_hist
        self.conv_new, self.gla_new, self.k_new, self.v_new = [], [], [], []

    @property
    def fresh(self):
        return self.k_hist is None


def _trunk(groups, w):
    d = groups[0].xf.shape[1]
    depth = w["norm_mix"].shape[0]
    heads = w["rel_bias"].shape[1]
    att_dim = heads * ATT_HEAD_DIM
    tm = BAND_ROWS
    tail_divs = []
    for grp in groups:
        if grp.fresh:
            assert grp.t % tm == 0
            tail_divs.append(grp.t // tm)
        else:
            assert grp.bsz * grp.t == tm and grp.k_hist.shape[2] == BAND_ROWS and grp.t <= BAND_ROWS
            tail_divs.append(1)
            grp.kh = grp.k_hist.reshape(grp.k_hist.shape[:3] + (att_dim,))
            grp.vh = grp.v_hist.reshape(grp.v_hist.shape[:3] + (att_dim,))
    for layer in range(depth):
        g_mix = w["norm_mix"][layer][None, :]
        g_ffn = w["norm_ffn"][layer][None, :]
        xs = [grp.xf for grp in groups]
        mixes = []
        if layer % 2 == 0:
            e = layer // 2
            projs = _norm_proj(xs, g_mix, w["w_in_ab"], e, 2 * tm, 3 * w["conv_w"].shape[2])
            for grp, (proj_a, proj_b) in zip(groups, projs):
                st0 = grp.gla_prev[e].reshape(grp.bsz, -1, grp.gla_prev.shape[-1]).swapaxes(1, 2)
                mix, cs, st = _mixer_ab(
                    proj_a.reshape(grp.bsz, grp.t, -1), proj_b.reshape(grp.bsz, grp.t, -1),
                    w["conv_w"][e], w["gk_w2"][e],
                    w["gk_b"][e][None, :], w["gla_onorm"][e][None, :], grp.conv_prev[e], st0,
                    min(2 * BAND_ROWS, grp.t))
                grp.conv_new.append(cs)
                grp.gla_new.append(st.swapaxes(1, 2).reshape(grp.gla_prev.shape[1:]))
                mixes.append(mix.reshape(-1, mix.shape[-1]))
            w_o, w_o_layer = w["w_out_ab"], e
        else:
            o = layer // 2
            qkvs = _norm_qkv(xs, g_mix, w["w_qkv"], o, w["pool"], w["poolt"],
                             w["q_norm"][o], w["k_norm"][o], tm, tail_divs)
            for grp, (q, k, v, kt, vt) in zip(groups, qkvs):
                shp = (grp.bsz, grp.t, att_dim)
                tqc = min(2 * CHUNK, grp.t)
                att = _band_attention(
                    q.reshape(shp), None if grp.fresh else grp.kh, k.reshape(shp),
                    None if grp.fresh else grp.vh, v.reshape(shp), o,
                    _bias_row(w["rel_bias"][o], tqc), min(BAND_ROWS, grp.t), tqc)
                grp.k_new.append(kt)
                grp.v_new.append(vt)
                mixes.append(att.reshape(-1, att_dim))
            w_o, w_o_layer = w["w_o_att"], o
        outs = _post_mixer(xs, mixes, w_o, w_o_layer, g_ffn, w["w_ffn_in"], w["w_ffn_out"], layer, tm, 2)
        for grp, xf in zip(groups, outs):
            grp.xf = xf

    def cache(grp, tails, hist):
        new = jnp.stack(tails).reshape(len(tails), grp.bsz, -1, heads, ATT_HEAD_DIM)
        if grp.fresh:
            return new
        return jnp.concatenate([hist[:, :, grp.t:], new], axis=2)

    return [(grp.xf.reshape(grp.bsz, grp.t, d), jnp.stack(grp.conv_new), jnp.stack(grp.gla_new),
             cache(grp, grp.k_new, grp.k_hist), cache(grp, grp.v_new, grp.v_hist)) for grp in groups]


def kernel(x_prompt, x_sample, state_conv, state_gla, cache_k, cache_v, norm_mix, norm_ffn,
           w_in_ab, conv_w, gla_gk_w2, gla_gk_b, gla_onorm, w_out_ab, w_qkv, q_norm, k_norm,
           rel_bias, w_o_att, w_ffn_in, w_ffn_out):
    bsz = x_prompt.shape[0]
    n_even = state_conv.shape[0]
    n_odd = cache_k.shape[0]
    heads, head_dim = cache_k.shape[-2:]
    att_dim = heads * head_dim
    in_ab = w_in_ab.shape[-1]
    in_pad = -in_ab % LANES
    rank = gla_gk_w2.shape[1]
    pool = (np.arange(att_dim)[:, None] // head_dim == np.arange(LANES)[None, :])
    w = {
        "norm_mix": norm_mix, "norm_ffn": norm_ffn,
        "w_in_ab": jnp.pad(w_in_ab.astype(BF16), ((0, 0), (0, 0), (0, in_pad))),
        "conv_w": conv_w,
        "gk_w2": jnp.pad(gla_gk_w2, ((0, 0), (0, LANES - rank), (0, 0))).astype(BF16),
        "gk_b": gla_gk_b, "gla_onorm": gla_onorm,
        "w_out_ab": w_out_ab.astype(BF16),
        "w_qkv": w_qkv.astype(BF16),
        "q_norm": jnp.tile(q_norm, (1, heads))[:, None, :],
        "k_norm": jnp.tile(k_norm, (1, heads))[:, None, :],
        "rel_bias": rel_bias,
        "w_o_att": w_o_att.astype(BF16),
        "w_ffn_in": w_ffn_in.astype(BF16), "w_ffn_out": w_ffn_out.astype(BF16),
        "pool": jnp.asarray(pool, BF16), "poolt": jnp.asarray(pool.T, BF16),
    }
    conv0 = jnp.zeros((n_even, bsz) + state_conv.shape[2:], x_prompt.dtype)
    gla0 = jnp.zeros((n_even, bsz) + state_gla.shape[2:], F32)
    (y_p, conv_p, gla_p, k_p, v_p), (y_s, conv_s, gla_s, k_s, v_s) = _trunk(
        [_Group(x_prompt, conv0, gla0, None, None),
         _Group(x_sample, state_conv, state_gla, cache_k, cache_v)], w)
    return (y_p, y_s, conv_p, gla_p, k_p, v_p, conv_s, gla_s, k_s, v_s)
```

```python
import functools

import jax
import jax.numpy as jnp
import numpy as np
from jax import lax
from jax.experimental import pallas as pl
from jax.experimental.pallas import tpu as pltpu

F32 = jnp.float32
BF16 = jnp.bfloat16

EPS = 1e-6
NEG_INF = -1e30
CHUNK = 64
CONV_WIDTH = 3
GLA_HEADS = 4
GATE_NORM = 16.0
ATT_HEAD_DIM = 64
BAND_ROWS = 8 * CHUNK
MAX_REL = 256
LANES = 128
SUBLANES = 8
MXU_DIM = 256
VMEM_LIMIT = 56 * 1024 * 1024


def _params(*sem):
    return pltpu.CompilerParams(dimension_semantics=sem, vmem_limit_bytes=VMEM_LIMIT)


def _resident(shape):
    nd = len(shape)
    return pl.BlockSpec(shape, lambda *_: (0,) * nd, pipeline_mode=pl.Buffered(1))


def _resident_layer(stacked, layer):
    nd = stacked.ndim - 1
    return pl.BlockSpec((None,) + stacked.shape[1:], lambda *_: (layer,) + (0,) * nd,
                        pipeline_mode=pl.Buffered(1))


def _dot(a, b):
    return jnp.dot(a, b, preferred_element_type=F32)


def _dot_nt(a, b):
    return lax.dot_general(a, b, (((1,), (1,)), ((), ())), preferred_element_type=F32)


def _dot_tn(a, b):
    return lax.dot_general(a, b, (((0,), (0,)), ((), ())), preferred_element_type=F32)


def _split2(a):
    hi = a.astype(BF16)
    lo = (a - hi.astype(F32)).astype(BF16)
    return hi, lo


def _rms_rows(x, g):
    ms = jnp.mean(x * x, axis=-1, keepdims=True)
    return x * lax.rsqrt(ms + EPS) * g


def _silu(x):
    return x * jax.nn.sigmoid(x)


def _row_halves(ref):
    half = ref.shape[0] // 2
    return [slice(0, half), slice(half, 2 * half)]


def _proj_body(x_ref, g_ref, w_ref, oa_ref, ob_ref):
    split = oa_ref.shape[1]
    rows = _row_halves(x_ref)
    h = [_rms_rows(x_ref[r, :], g_ref[...]).astype(BF16) for r in rows]
    for r, hr in zip(rows, h):
        y = _dot(hr, w_ref[...])
        oa_ref[r, :] = y[:, :split].astype(oa_ref.dtype)
        ob_ref[r, :] = y[:, split:].astype(ob_ref.dtype)


def _rowwise_call(body, row_groups, shared, outs, tm):
    n_groups, n_in, n_out = len(row_groups), len(row_groups[0]), len(outs)
    tms = [min(tm, grp[0].shape[0]) for grp in row_groups]
    nblk = [grp[0].shape[0] // t for grp, t in zip(row_groups, tms)]
    start = [sum(nblk[:g]) for g in range(n_groups)]

    def local(g):
        return lambda i: jnp.clip(i - start[g], 0, nblk[g] - 1)

    in_specs, operands, out_specs, out_shape = [], [], [], []
    for g, grp in enumerate(row_groups):
        for a in grp:
            assert a.shape[0] == nblk[g] * tms[g]
            in_specs.append(pl.BlockSpec((tms[g], a.shape[1]), lambda i, f=local(g): (f(i), 0)))
            operands.append(a)
    for a, spec in shared:
        in_specs.append(spec)
        operands.append(a)
    for g in range(n_groups):
        for cols, dtype, keep in outs:
            assert nblk[g] % keep[g] == 0
            out_specs.append(pl.BlockSpec((tms[g], cols), lambda i, f=local(g), k=keep[g]: (f(i) // k, 0)))
            out_shape.append(jax.ShapeDtypeStruct((nblk[g] // keep[g] * tms[g], cols), dtype))

    def per_group(*refs):
        ins, sh = refs[:n_groups * n_in], refs[n_groups * n_in:n_groups * n_in + len(shared)]
        out_refs = refs[n_groups * n_in + len(shared):]
        i = pl.program_id(0)
        for g in range(n_groups):
            pl.when((i >= start[g]) & (i < start[g] + nblk[g]))(functools.partial(
                body, *ins[g * n_in:(g + 1) * n_in], *sh, *out_refs[g * n_out:(g + 1) * n_out]))

    res = pl.pallas_call(
        per_group,
        grid=(sum(nblk),),
        in_specs=in_specs,
        out_specs=out_specs,
        out_shape=out_shape,
        compiler_params=_params("arbitrary"),
    )(*operands)
    return [res[g * n_out:(g + 1) * n_out] for g in range(n_groups)]


def _norm_proj(xs, g, w, layer, tm, split):
    d = xs[0].shape[1]
    every = [1] * len(xs)
    return _rowwise_call(
        _proj_body, [[x] for x in xs],
        [(g, _resident((1, d))), (w, _resident_layer(w, layer))],
        [(split, BF16, every), (w.shape[2] - split, BF16, every)], tm)


def _qkv_body(x_ref, g_ref, w_ref, pool_ref, poolt_ref, gq_ref, gk_ref,
              q_ref, k_ref, v_ref, kt_ref, vt_ref, *, att_dim, q_scale):
    def head_norm(a, gt):
        ms = _dot((a * a).astype(BF16), pool_ref[...]) * (1.0 / ATT_HEAD_DIM)
        rb = _dot(lax.rsqrt(ms + EPS).astype(BF16), poolt_ref[...])
        return a * rb * gt

    rows = _row_halves(x_ref)
    h = [_rms_rows(x_ref[r, :], g_ref[...]).astype(BF16) for r in rows]
    qkv = [_dot(hr, w_ref[...]) for hr in h]
    for r, y in zip(rows, qkv):
        q_ref[r, :] = (head_norm(y[:, :att_dim], gq_ref[...]) * q_scale).astype(BF16)
        kn = head_norm(y[:, att_dim:2 * att_dim], gk_ref[...])
        v = y[:, 2 * att_dim:]
        k_ref[r, :] = kn.astype(BF16)
        v_ref[r, :] = v.astype(BF16)
        kt_ref[r, :] = kn
        vt_ref[r, :] = v


def _norm_qkv(xs, g, w, layer, pool, poolt, gq, gk, tm, tail_divs):
    d = xs[0].shape[1]
    att_dim = w.shape[2] // 3
    every = [1] * len(xs)
    return _rowwise_call(
        functools.partial(_qkv_body, att_dim=att_dim, q_scale=ATT_HEAD_DIM ** -0.5),
        [[x] for x in xs],
        [(g, _resident((1, d))), (w, _resident_layer(w, layer)), (pool, _resident(pool.shape)),
         (poolt, _resident(poolt.shape)), (gq, _resident((1, att_dim))), (gk, _resident((1, att_dim)))],
        [(att_dim, BF16, every)] * 3 + [(att_dim, F32, tail_divs)] * 2, tm)


def _post_body(x_ref, a_ref, wo_ref, g_ref, win_ref, wout_ref, o_ref, *, d_ff, edges):
    rows = _row_halves(x_ref)
    acc = [x_ref[r, :] + _dot(a_ref[r, :], wo_ref[...]) for r in rows]
    h = [_rms_rows(x1, g_ref[...]).astype(BF16) for x1 in acc]
    for lo, hi in zip(edges[:-1], edges[1:]):
        for i in range(len(rows)):
            gate = _dot(h[i], win_ref[:, lo:hi])
            up = _dot(h[i], win_ref[:, d_ff + lo:d_ff + hi])
            act = (_silu(gate) * up).astype(BF16)
            acc[i] = acc[i] + _dot(act, wout_ref[lo:hi, :])
    for i, r in enumerate(rows):
        o_ref[r, :] = acc[i]


def _ffn_edges(d_ff, n_chunks):
    assert d_ff % MXU_DIM == 0
    tiles = d_ff // MXU_DIM
    return tuple(MXU_DIM * ((tiles * c + n_chunks - 1) // n_chunks) for c in range(n_chunks + 1))


def _post_mixer(xs, mixes, wo, wo_layer, g, win, wout, layer, tm, n_chunks):
    d = xs[0].shape[1]
    d_ff = wout.shape[1]
    res = _rowwise_call(
        functools.partial(_post_body, d_ff=d_ff, edges=_ffn_edges(d_ff, n_chunks)),
        [[x, a] for x, a in zip(xs, mixes)],
        [(wo, _resident_layer(wo, wo_layer)), (g, _resident((1, d))),
         (win, _resident_layer(win, layer)), (wout, _resident_layer(wout, layer))],
        [(d, F32, [1] * len(xs))], tm)
    return [r[0] for r in res]


def _mixer_body(pa_ref, p_ref, cw_ref, w2_ref, gb_ref, on_ref, conv0_ref, st0_ref,
                y_ref, convo_ref, sto_ref, ubuf, st_sc, *, tb, cdim, dk, dv):
    hk = dk // GLA_HEADS
    hv = dv // GLA_HEADS
    o_q = 0
    o_k = o_q + dk
    o_v = o_k + dk
    o_g = o_v + dv
    o_l = o_g + dv
    t = pl.program_id(1)

    @pl.when(t == 0)
    def _():
        ubuf[0:8, :] = jnp.zeros((8, cdim), F32)
        ubuf[6:8, :] = conv0_ref[...]
        st_sc[...] = st0_ref[...]

    u = pa_ref[:, 0:cdim].astype(F32) * pa_ref[:, 2 * cdim:3 * cdim].astype(F32)
    ubuf[8:8 + tb, :] = u
    y = (cw_ref[0:1, :] * ubuf[6:6 + tb, :] + cw_ref[1:2, :] * ubuf[7:7 + tb, :]
         + cw_ref[2:3, :] * u)
    y_ref[:, 0:cdim] = (pa_ref[:, cdim:2 * cdim].astype(F32) * y).astype(y_ref.dtype)
    tail = ubuf[tb + 6:tb + 8, :]
    ubuf[6:8, :] = tail
    convo_ref[...] = tail

    L = CHUNK
    assert hk == L and L & (L - 1) == 0
    sh = L.bit_length() - 1
    nst = GLA_HEADS * L
    r_i = lax.broadcasted_iota(jnp.int32, (nst, dk), 0)
    c_i = lax.broadcasted_iota(jnp.int32, (nst, dk), 1)
    same_head = (r_i >> sh) == (c_i >> sh)
    a_r = lax.broadcasted_iota(jnp.int32, (nst, nst), 0)
    a_c = lax.broadcasted_iota(jnp.int32, (nst, nst), 1)
    att_keep = ((a_r >> sh) == (a_c >> sh)) & ((a_c & (L - 1)) <= (a_r & (L - 1)))
    tri = (lax.broadcasted_iota(jnp.int32, (L, L), 1)
           <= lax.broadcasted_iota(jnp.int32, (L, L), 0)).astype(BF16)
    w2 = w2_ref[...]
    gb = gb_ref[...]
    onorm = on_ref[...]

    rows = [slice(c * L, (c + 1) * L) for c in range(tb // L)]

    def stack(a):
        return jnp.concatenate([a] * GLA_HEADS, axis=0)

    def heads_to_rows(o0, r):
        return jnp.concatenate([p_ref[r, o0 + h * hv:o0 + (h + 1) * hv] for h in range(GLA_HEADS)], axis=0)

    gk = _dot(p_ref[:, o_l:o_l + LANES], w2) + gb
    la = (jnp.minimum(gk, 0.0) - jnp.log(1.0 + jnp.exp(-jnp.abs(gk)))) * (1.0 / GATE_NORM)
    l_hi, l_lo = _split2(la)
    b = [_dot(tri, l_hi[r]) + _dot(tri, l_lo[r]) for r in rows]
    b_last = [x[L - 1:L, :] for x in b]
    q_st, k_tl, kd_bd, v_st = [], [], [], []
    for c, r in enumerate(rows):
        q = p_ref[r, o_q:o_k].astype(F32) * (hk ** -0.5)
        k = p_ref[r, o_k:o_v].astype(F32)
        q_st.append(jnp.where(same_head, stack(q * jnp.exp(b[c])), 0.0).astype(BF16))
        k_tl.append(stack((k * jnp.exp(-b[c])).astype(BF16)))
        kd_bd.append(jnp.where(same_head, stack(k * jnp.exp(b_last[c] - b[c])), 0.0).astype(BF16))
        v_st.append(heads_to_rows(o_v, r))
    att = [jnp.where(att_keep, _dot_nt(q_st[c], k_tl[c]), 0.0).astype(BF16)
           for c in range(len(rows))]
    upd = [_dot_tn(v_st[c], kd_bd[c]) for c in range(len(rows))]
    o = [_dot(att[c], v_st[c]) for c in range(len(rows))]
    st = st_sc[...]
    for c in range(len(rows)):
        o[c] = o[c] + _dot_nt(q_st[c], st.astype(BF16))
        st = st * jnp.exp(b_last[c]) + upd[c]
    for c, r in enumerate(rows):
        oc = _rms_rows(o[c], onorm) * _silu(heads_to_rows(o_g, r).astype(F32))
        for h in range(GLA_HEADS):
            y_ref[r, cdim + h * hv:cdim + (h + 1) * hv] = oc[h * L:(h + 1) * L, :].astype(y_ref.dtype)
    st_sc[...] = st
    sto_ref[...] = st


def _mixer_ab(proj_a, proj_b, conv_w, w2pad, gb, onorm, conv0, st0, tb):
    bsz, t, _ = proj_a.shape
    cdim = conv_w.shape[1]
    dk = w2pad.shape[1]
    hv = onorm.shape[1]
    dv = hv * GLA_HEADS
    return pl.pallas_call(
        functools.partial(_mixer_body, tb=tb, cdim=cdim, dk=dk, dv=dv),
        grid=(bsz, t // tb),
        in_specs=[
            pl.BlockSpec((None, tb, proj_a.shape[2]), lambda b, i: (b, i, 0)),
            pl.BlockSpec((None, tb, proj_b.shape[2]), lambda b, i: (b, i, 0)),
            _resident(conv_w.shape),
            _resident(w2pad.shape),
            _resident(gb.shape),
            _resident(onorm.shape),
            pl.BlockSpec((None, CONV_WIDTH - 1, cdim), lambda b, i: (b, 0, 0)),
            pl.BlockSpec((None, hv, dk), lambda b, i: (b, 0, 0)),
        ],
        out_specs=[
            pl.BlockSpec((None, tb, cdim + dv), lambda b, i: (b, i, 0)),
            pl.BlockSpec((None, CONV_WIDTH - 1, cdim), lambda b, i: (b, 0, 0)),
            pl.BlockSpec((None, hv, dk), lambda b, i: (b, 0, 0)),
        ],
        out_shape=[
            jax.ShapeDtypeStruct((bsz, t, cdim + dv), BF16),
            jax.ShapeDtypeStruct((bsz, CONV_WIDTH - 1, cdim), F32),
            jax.ShapeDtypeStruct((bsz, hv, dk), F32),
        ],
        scratch_shapes=[
            pltpu.VMEM((tb + 8, cdim), F32),
            pltpu.VMEM((hv, dk), F32),
        ],
        compiler_params=_params("parallel", "arbitrary"),
    )(proj_a, proj_b, conv_w, w2pad, gb, onorm, conv0, st0)


def _attn_body(q_ref, ka_ref, kb_ref, va_ref, vb_ref, brow_ref, o_ref, tab_ref,
               *, tq, tqc, hist_is_padding):
    nwin = BAND_ROWS + tqc

    @pl.when((pl.program_id(0) == 0) & (pl.program_id(1) == 0))
    def _build_table():
        wrow = brow_ref.shape[1]
        sh = CHUNK.bit_length() - 1
        sub = lax.broadcasted_iota(jnp.int32, (SUBLANES, wrow), 0)
        q_chunk = lax.broadcasted_iota(jnp.int32, (tqc, nwin), 0) >> sh
        k_chunk = (lax.broadcasted_iota(jnp.int32, (tqc, nwin), 1) >> sh) - BAND_ROWS // CHUNK
        allowed = (k_chunk <= q_chunk) & (k_chunk >= q_chunk - BAND_ROWS // CHUNK)

        def one_head(h, carry):
            y = jnp.broadcast_to(brow_ref[pl.ds(h, 1), :], (SUBLANES, wrow))
            for bit in range(SUBLANES.bit_length() - 1):
                y = jnp.where(((sub >> bit) & 1) == 1, pltpu.roll(y, 1 << bit, axis=1), y)
            y = jnp.concatenate([y] + [pltpu.roll(y, SUBLANES * a, axis=1)
                                       for a in range(1, tqc // SUBLANES)], axis=0)
            slack = _table_slack(tqc)
            t = jnp.where(allowed, y[:, slack:slack + nwin], NEG_INF)
            tab_ref[h >> 1, pl.ds(pl.multiple_of((h & 1) * tqc, tqc), tqc), :] = t
            return carry

        lax.fori_loop(0, 2 * tab_ref.shape[0], one_head, 0)

    n_pairs = q_ref.shape[1] // LANES
    lo = lax.broadcasted_iota(jnp.int32, (tqc, LANES), 1) < ATT_HEAD_DIM
    keep_q = ((lax.broadcasted_iota(jnp.int32, (2 * tqc, LANES), 1) < ATT_HEAD_DIM)
              == (lax.broadcasted_iota(jnp.int32, (2 * tqc, LANES), 0) < tqc))

    def run(first_block):
        def window(j):
            r0 = j * tqc
            hist = slice(BAND_ROWS, BAND_ROWS) if first_block else slice(min(r0, BAND_ROWS), BAND_ROWS)
            cur = slice(max(r0 - BAND_ROWS, 0), r0 + tqc)
            c0 = max(BAND_ROWS - r0, 0) if first_block else 0
            return r0, hist, cur, c0

        def scores(j, p):
            r0, hist, cur, c0 = window(j)
            lanes = slice(p * LANES, (p + 1) * LANES)
            qp = q_ref[r0:r0 + tqc, lanes]
            q2 = jnp.where(keep_q, jnp.concatenate([qp, qp], axis=0), 0.0)
            s = _dot_nt(q2, kb_ref[cur, lanes])
            if hist.stop > hist.start:
                s = jnp.concatenate([_dot_nt(q2, ka_ref[hist, lanes].astype(BF16)), s], axis=1)
            return s + tab_ref[p, :, c0:]

        def weighted_values(j, p, e, den):
            r0, hist, cur, c0 = window(j)
            lanes = slice(p * LANES, (p + 1) * LANES)
            n_hist = hist.stop - hist.start
            pv = _dot(e[:, n_hist:], vb_ref[cur, lanes])
            if n_hist:
                pv = pv + _dot(e[:, :n_hist], va_ref[hist, lanes].astype(BF16))
            pv = pv / den
            o_ref[r0:r0 + tqc, lanes] = jnp.where(lo, pv[:tqc], pv[tqc:]).astype(o_ref.dtype)

        for j in range(tq // tqc):
            s_next = scores(j, 0)
            pending = None
            for p in range(n_pairs):
                s = s_next
                if p + 1 < n_pairs:
                    s_next = scores(j, p + 1)
                e = jnp.exp((s - jnp.max(s, axis=-1, keepdims=True)).astype(BF16))
                den = jnp.sum(e.astype(F32), axis=-1, keepdims=True)
                if pending is not None:
                    weighted_values(*pending)
                pending = (j, p, e, den)
            weighted_values(*pending)

    if hist_is_padding:
        first = pl.program_id(1) == 0
        pl.when(first)(lambda: run(True))
        pl.when(jnp.logical_not(first))(lambda: run(False))
    else:
        run(False)


def _band_attention(q, k_hist, k_cur, v_hist, v_cur, layer, brow, tq, tqc):
    bsz, t, width = q.shape
    assert t % tq == 0 and tq % tqc == 0
    hist_is_padding = k_hist is None
    if hist_is_padding:
        assert tq == BAND_ROWS
        k_hist, v_hist = k_cur, v_cur
        hist_spec = pl.BlockSpec((None, BAND_ROWS, width), lambda b, i: (b, jnp.maximum(i - 1, 0), 0))
    else:
        assert t == tq and k_hist.shape[2] == BAND_ROWS
        hist_spec = pl.BlockSpec((None, None, BAND_ROWS, width), lambda b, i: (layer, b, 0, 0))
    cur_map = lambda b, i: (b, i, 0)
    return pl.pallas_call(
        functools.partial(_attn_body, tq=tq, tqc=tqc, hist_is_padding=hist_is_padding),
        grid=(bsz, t // tq),
        in_specs=[
            pl.BlockSpec((None, tq, width), cur_map),
            hist_spec,
            pl.BlockSpec((None, tq, width), cur_map),
            hist_spec,
            pl.BlockSpec((None, tq, width), cur_map),
            _resident(brow.shape),
        ],
        out_specs=pl.BlockSpec((None, tq, width), cur_map),
        out_shape=jax.ShapeDtypeStruct((bsz, t, width), BF16),
        scratch_shapes=[
            pltpu.VMEM((width // LANES, 2 * tqc, BAND_ROWS + tqc), F32),
        ],
        compiler_params=_params("arbitrary", "arbitrary"),
    )(q, k_hist, k_cur, v_hist, v_cur, brow)


def _table_slack(tqc):
    return -(-tqc // LANES) * LANES


def _bias_row(rel_bias, tqc):
    slack = _table_slack(tqc)
    width = -(-(slack + BAND_ROWS + tqc) // LANES) * LANES
    rel = np.clip(BAND_ROWS + slack - np.arange(width), -(CHUNK - 1), MAX_REL) + (CHUNK - 1)
    return rel_bias[:, rel].astype(F32)


class _Group:
    def __init__(self, x, conv_prev, gla_prev, k_hist, v_hist):
        self.bsz, self.t, d = x.shape
        self.xf = x.reshape(-1, d)
        self.conv_prev, self.gla_prev = conv_prev, gla_prev
        self.k_hist, self.v_hist = k_hist, v_hist
        self.conv_new, self.gla_new, self.k_new, self.v_new = [], [], [], []

    @property
    def fresh(self):
        return self.k_hist is None


def _trunk(groups, w):
    d = groups[0].xf.shape[1]
    depth = w["norm_mix"].shape[0]
    heads = w["rel_bias"].shape[1]
    att_dim = heads * ATT_HEAD_DIM
    tm = BAND_ROWS
    tail_divs = []
    for grp in groups:
        if grp.fresh:
            assert grp.t % tm == 0
            tail_divs.append(grp.t // tm)
        else:
            assert grp.bsz * grp.t == tm and grp.k_hist.shape[2] == BAND_ROWS and grp.t <= BAND_ROWS
            tail_divs.append(1)
            grp.kh = grp.k_hist.reshape(grp.k_hist.shape[:3] + (att_dim,))
            grp.vh = grp.v_hist.reshape(grp.v_hist.shape[:3] + (att_dim,))
    for layer in range(depth):
        g_mix = w["norm_mix"][layer][None, :]
        g_ffn = w["norm_ffn"][layer][None, :]
        xs = [grp.xf for grp in groups]
        mixes = []
        if layer % 2 == 0:
            e = layer // 2
            projs = _norm_proj(xs, g_mix, w["w_in_ab"], e, 2 * tm, 3 * w["conv_w"].shape[2])
            for grp, (proj_a, proj_b) in zip(groups, projs):
                st0 = grp.gla_prev[e].reshape(grp.bsz, -1, grp.gla_prev.shape[-1]).swapaxes(1, 2)
                mix, cs, st = _mixer_ab(
                    proj_a.reshape(grp.bsz, grp.t, -1), proj_b.reshape(grp.bsz, grp.t, -1),
                    w["conv_w"][e], w["gk_w2"][e],
                    w["gk_b"][e][None, :], w["gla_onorm"][e][None, :], grp.conv_prev[e], st0,
                    min(2 * BAND_ROWS, grp.t))
                grp.conv_new.append(cs)
                grp.gla_new.append(st.swapaxes(1, 2).reshape(grp.gla_prev.shape[1:]))
                mixes.append(mix.reshape(-1, mix.shape[-1]))
            w_o, w_o_layer = w["w_out_ab"], e
        else:
            o = layer // 2
            qkvs = _norm_qkv(xs, g_mix, w["w_qkv"], o, w["pool"], w["poolt"],
                             w["q_norm"][o], w["k_norm"][o], tm, tail_divs)
            for grp, (q, k, v, kt, vt) in zip(groups, qkvs):
                shp = (grp.bsz, grp.t, att_dim)
                tqc = min(2 * CHUNK, grp.t)
                att = _band_attention(
                    q.reshape(shp), None if grp.fresh else grp.kh, k.reshape(shp),
                    None if grp.fresh else grp.vh, v.reshape(shp), o,
                    _bias_row(w["rel_bias"][o], tqc), min(BAND_ROWS, grp.t), tqc)
                grp.k_new.append(kt)
                grp.v_new.append(vt)
                mixes.append(att.reshape(-1, att_dim))
            w_o, w_o_layer = w["w_o_att"], o
        outs = _post_mixer(xs, mixes, w_o, w_o_layer, g_ffn, w["w_ffn_in"], w["w_ffn_out"], layer, tm, 2)
        for grp, xf in zip(groups, outs):
            grp.xf = xf

    def cache(grp, tails, hist):
        new = jnp.stack(tails).reshape(len(tails), grp.bsz, -1, heads, ATT_HEAD_DIM)
        if grp.fresh:
            return new
        return jnp.concatenate([hist[:, :, grp.t:], new], axis=2)

    return [(grp.xf.reshape(grp.bsz, grp.t, d), jnp.stack(grp.conv_new), jnp.stack(grp.gla_new),
             cache(grp, grp.k_new, grp.k_hist), cache(grp, grp.v_new, grp.v_hist)) for grp in groups]


def kernel(x_prompt, x_sample, state_conv, state_gla, cache_k, cache_v, norm_mix, norm_ffn,
           w_in_ab, conv_w, gla_gk_w2, gla_gk_b, gla_onorm, w_out_ab, w_qkv, q_norm, k_norm,
           rel_bias, w_o_att, w_ffn_in, w_ffn_out):
    bsz = x_prompt.shape[0]
    n_even = state_conv.shape[0]
    n_odd = cache_k.shape[0]
    heads, head_dim = cache_k.shape[-2:]
    att_dim = heads * head_dim
    in_ab = w_in_ab.shape[-1]
    in_pad = -in_ab % LANES
    rank = gla_gk_w2.shape[1]
    pool = (np.arange(att_dim)[:, None] // head_dim == np.arange(LANES)[None, :])
    w = {
        "norm_mix": norm_mix, "norm_ffn": norm_ffn,
        "w_in_ab": jnp.pad(w_in_ab, ((0, 0), (0, 0), (0, in_pad))).astype(BF16),
        "conv_w": conv_w,
        "gk_w2": jnp.pad(gla_gk_w2, ((0, 0), (0, LANES - rank), (0, 0))).astype(BF16),
        "gk_b": gla_gk_b, "gla_onorm": gla_onorm,
        "w_out_ab": w_out_ab.astype(BF16),
        "w_qkv": w_qkv.astype(BF16),
        "q_norm": jnp.tile(q_norm, (1, heads))[:, None, :],
        "k_norm": jnp.tile(k_norm, (1, heads))[:, None, :],
        "rel_bias": rel_bias,
        "w_o_att": w_o_att.astype(BF16),
        "w_ffn_in": w_ffn_in.astype(BF16), "w_ffn_out": w_ffn_out.astype(BF16),
        "pool": jnp.asarray(pool, BF16), "poolt": jnp.asarray(pool.T, BF16),
    }
    conv0 = jnp.zeros((n_even, bsz) + state_conv.shape[2:], x_prompt.dtype)
    gla0 = jnp.zeros((n_even, bsz) + state_gla.shape[2:], F32)
    (y_p, conv_p, gla_p, k_p, v_p), (y_s, conv_s, gla_s, k_s, v_s) = _trunk(
        [_Group(x_prompt, conv0, gla0, None, None),
         _Group(x_sample, state_conv, state_gla, cache_k, cache_v)], w)
    return (y_p, y_s, conv_p, gla_p, k_p, v_p, conv_s, gla_s, k_s, v_s)
```

```python
import functools

import jax
import jax.numpy as jnp
import numpy as np
from jax import lax
from jax.experimental import pallas as pl
from jax.experimental.pallas import tpu as pltpu

F32 = jnp.float32
BF16 = jnp.bfloat16

EPS = 1e-6
NEG_INF = -1e30
CHUNK = 64
CONV_WIDTH = 3
GLA_HEADS = 4
GATE_NORM = 16.0
ATT_HEAD_DIM = 64
BAND_ROWS = 8 * CHUNK
MAX_REL = 256
LANES = 128
SUBLANES = 8
MXU_DIM = 256
VMEM_LIMIT = 56 * 1024 * 1024


def _params(*sem):
    return pltpu.CompilerParams(dimension_semantics=sem, vmem_limit_bytes=VMEM_LIMIT)


def _resident(shape):
    nd = len(shape)
    return pl.BlockSpec(shape, lambda *_: (0,) * nd, pipeline_mode=pl.Buffered(1))


def _resident_layer(stacked, layer):
    nd = stacked.ndim - 1
    return pl.BlockSpec((None,) + stacked.shape[1:], lambda *_: (layer,) + (0,) * nd,
                        pipeline_mode=pl.Buffered(1))


def _dot(a, b):
    return jnp.dot(a, b, preferred_element_type=F32)


def _dot_nt(a, b):
    return lax.dot_general(a, b, (((1,), (1,)), ((), ())), preferred_element_type=F32)


def _dot_tn(a, b):
    return lax.dot_general(a, b, (((0,), (0,)), ((), ())), preferred_element_type=F32)


def _split2(a):
    hi = a.astype(BF16)
    lo = (a - hi.astype(F32)).astype(BF16)
    return hi, lo


def _rms_rows(x, g):
    ms = jnp.mean(x * x, axis=-1, keepdims=True)
    return x * lax.rsqrt(ms + EPS) * g


def _silu(x):
    return x * jax.nn.sigmoid(x)


def _row_halves(ref):
    half = ref.shape[0] // 2
    return [slice(0, half), slice(half, 2 * half)]


def _proj_body(x_ref, g_ref, w_ref, oa_ref, ob_ref):
    split = oa_ref.shape[1]
    rows = _row_halves(x_ref)
    h = [_rms_rows(x_ref[r, :], g_ref[...]).astype(BF16) for r in rows]
    for r, hr in zip(rows, h):
        y = _dot(hr, w_ref[...])
        oa_ref[r, :] = y[:, :split]
        ob_ref[r, :] = y[:, split:].astype(ob_ref.dtype)


def _rowwise_call(body, row_groups, shared, outs, tm):
    n_groups, n_in, n_out = len(row_groups), len(row_groups[0]), len(outs)
    tms = [min(tm, grp[0].shape[0]) for grp in row_groups]
    nblk = [grp[0].shape[0] // t for grp, t in zip(row_groups, tms)]
    start = [sum(nblk[:g]) for g in range(n_groups)]

    def local(g):
        return lambda i: jnp.clip(i - start[g], 0, nblk[g] - 1)

    in_specs, operands, out_specs, out_shape = [], [], [], []
    for g, grp in enumerate(row_groups):
        for a in grp:
            assert a.shape[0] == nblk[g] * tms[g]
            in_specs.append(pl.BlockSpec((tms[g], a.shape[1]), lambda i, f=local(g): (f(i), 0)))
            operands.append(a)
    for a, spec in shared:
        in_specs.append(spec)
        operands.append(a)
    for g in range(n_groups):
        for cols, dtype, keep in outs:
            assert nblk[g] % keep[g] == 0
            out_specs.append(pl.BlockSpec((tms[g], cols), lambda i, f=local(g), k=keep[g]: (f(i) // k, 0)))
            out_shape.append(jax.ShapeDtypeStruct((nblk[g] // keep[g] * tms[g], cols), dtype))

    def per_group(*refs):
        ins, sh = refs[:n_groups * n_in], refs[n_groups * n_in:n_groups * n_in + len(shared)]
        out_refs = refs[n_groups * n_in + len(shared):]
        i = pl.program_id(0)
        for g in range(n_groups):
            pl.when((i >= start[g]) & (i < start[g] + nblk[g]))(functools.partial(
                body, *ins[g * n_in:(g + 1) * n_in], *sh, *out_refs[g * n_out:(g + 1) * n_out]))

    res = pl.pallas_call(
        per_group,
        grid=(sum(nblk),),
        in_specs=in_specs,
        out_specs=out_specs,
        out_shape=out_shape,
        compiler_params=_params("arbitrary"),
    )(*operands)
    return [res[g * n_out:(g + 1) * n_out] for g in range(n_groups)]


def _norm_proj(xs, g, w, layer, tm, split):
    d = xs[0].shape[1]
    every = [1] * len(xs)
    return _rowwise_call(
        _proj_body, [[x] for x in xs],
        [(g, _resident((1, d))), (w, _resident_layer(w, layer))],
        [(split, F32, every), (w.shape[2] - split, BF16, every)], tm)


def _qkv_body(x_ref, g_ref, w_ref, pool_ref, poolt_ref, gq_ref, gk_ref,
              q_ref, k_ref, v_ref, kt_ref, vt_ref, *, att_dim, q_scale):
    def head_norm(a, gt):
        ms = _dot((a * a).astype(BF16), pool_ref[...]) * (1.0 / ATT_HEAD_DIM)
        rb = _dot(lax.rsqrt(ms + EPS).astype(BF16), poolt_ref[...])
        return a * rb * gt

    rows = _row_halves(x_ref)
    h = [_rms_rows(x_ref[r, :], g_ref[...]).astype(BF16) for r in rows]
    qkv = [_dot(hr, w_ref[...]) for hr in h]
    for r, y in zip(rows, qkv):
        q_ref[r, :] = (head_norm(y[:, :att_dim], gq_ref[...]) * q_scale).astype(BF16)
        kn = head_norm(y[:, att_dim:2 * att_dim], gk_ref[...])
        v = y[:, 2 * att_dim:]
        k_ref[r, :] = kn.astype(BF16)
        v_ref[r, :] = v.astype(BF16)
        kt_ref[r, :] = kn
        vt_ref[r, :] = v


def _norm_qkv(xs, g, w, layer, pool, poolt, gq, gk, tm, tail_divs):
    d = xs[0].shape[1]
    att_dim = w.shape[2] // 3
    every = [1] * len(xs)
    return _rowwise_call(
        functools.partial(_qkv_body, att_dim=att_dim, q_scale=ATT_HEAD_DIM ** -0.5),
        [[x] for x in xs],
        [(g, _resident((1, d))), (w, _resident_layer(w, layer)), (pool, _resident(pool.shape)),
         (poolt, _resident(poolt.shape)), (gq, _resident((1, att_dim))), (gk, _resident((1, att_dim)))],
        [(att_dim, BF16, every)] * 3 + [(att_dim, F32, tail_divs)] * 2, tm)


def _post_body(x_ref, a_ref, wo_ref, g_ref, win_ref, wout_ref, o_ref, *, d_ff, edges):
    rows = _row_halves(x_ref)
    acc = [x_ref[r, :] + _dot(a_ref[r, :], wo_ref[...]) for r in rows]
    h = [_rms_rows(x1, g_ref[...]).astype(BF16) for x1 in acc]
    for lo, hi in zip(edges[:-1], edges[1:]):
        for i in range(len(rows)):
            gate = _dot(h[i], win_ref[:, lo:hi])
            up = _dot(h[i], win_ref[:, d_ff + lo:d_ff + hi])
            act = (_silu(gate) * up).astype(BF16)
            acc[i] = acc[i] + _dot(act, wout_ref[lo:hi, :])
    for i, r in enumerate(rows):
        o_ref[r, :] = acc[i]


def _ffn_edges(d_ff, n_chunks):
    assert d_ff % MXU_DIM == 0
    tiles = d_ff // MXU_DIM
    return tuple(MXU_DIM * ((tiles * c + n_chunks - 1) // n_chunks) for c in range(n_chunks + 1))


def _post_mixer(xs, mixes, wo, wo_layer, g, win, wout, layer, tm, n_chunks):
    d = xs[0].shape[1]
    d_ff = wout.shape[1]
    res = _rowwise_call(
        functools.partial(_post_body, d_ff=d_ff, edges=_ffn_edges(d_ff, n_chunks)),
        [[x, a] for x, a in zip(xs, mixes)],
        [(wo, _resident_layer(wo, wo_layer)), (g, _resident((1, d))),
         (win, _resident_layer(win, layer)), (wout, _resident_layer(wout, layer))],
        [(d, F32, [1] * len(xs))], tm)
    return [r[0] for r in res]


def _mixer_body(pa_ref, p_ref, cw_ref, w2_ref, gb_ref, on_ref, conv0_ref, st0_ref,
                y_ref, convo_ref, sto_ref, ubuf, st_sc, *, tb, cdim, dk, dv):
    hk = dk // GLA_HEADS
    hv = dv // GLA_HEADS
    o_q = 0
    o_k = o_q + dk
    o_v = o_k + dk
    o_g = o_v + dv
    o_l = o_g + dv
    t = pl.program_id(1)

    @pl.when(t == 0)
    def _():
        ubuf[0:8, :] = jnp.zeros((8, cdim), F32)
        ubuf[6:8, :] = conv0_ref[...]
        st_sc[...] = st0_ref[...]

    u = pa_ref[:, 0:cdim] * pa_ref[:, 2 * cdim:3 * cdim]
    ubuf[8:8 + tb, :] = u
    y = (cw_ref[0:1, :] * ubuf[6:6 + tb, :] + cw_ref[1:2, :] * ubuf[7:7 + tb, :]
         + cw_ref[2:3, :] * u)
    y_ref[:, 0:cdim] = (pa_ref[:, cdim:2 * cdim] * y).astype(y_ref.dtype)
    tail = ubuf[tb + 6:tb + 8, :]
    ubuf[6:8, :] = tail
    convo_ref[...] = tail

    L = CHUNK
    assert hk == L and L & (L - 1) == 0
    sh = L.bit_length() - 1
    nst = GLA_HEADS * L
    r_i = lax.broadcasted_iota(jnp.int32, (nst, dk), 0)
    c_i = lax.broadcasted_iota(jnp.int32, (nst, dk), 1)
    same_head = (r_i >> sh) == (c_i >> sh)
    a_r = lax.broadcasted_iota(jnp.int32, (nst, nst), 0)
    a_c = lax.broadcasted_iota(jnp.int32, (nst, nst), 1)
    att_keep = ((a_r >> sh) == (a_c >> sh)) & ((a_c & (L - 1)) <= (a_r & (L - 1)))
    tri = (lax.broadcasted_iota(jnp.int32, (L, L), 1)
           <= lax.broadcasted_iota(jnp.int32, (L, L), 0)).astype(BF16)
    w2 = w2_ref[...]
    gb = gb_ref[...]
    onorm = on_ref[...]

    rows = [slice(c * L, (c + 1) * L) for c in range(tb // L)]

    def stack(a):
        return jnp.concatenate([a] * GLA_HEADS, axis=0)

    def heads_to_rows(o0, r):
        return jnp.concatenate([p_ref[r, o0 + h * hv:o0 + (h + 1) * hv] for h in range(GLA_HEADS)], axis=0)

    gk = _dot(p_ref[:, o_l:o_l + LANES], w2) + gb
    la = (jnp.minimum(gk, 0.0) - jnp.log(1.0 + jnp.exp(-jnp.abs(gk)))) * (1.0 / GATE_NORM)
    l_hi, l_lo = _split2(la)
    b = [_dot(tri, l_hi[r]) + _dot(tri, l_lo[r]) for r in rows]
    b_last = [x[L - 1:L, :] for x in b]
    q_st, k_tl, kd_bd, v_st = [], [], [], []
    for c, r in enumerate(rows):
        q = p_ref[r, o_q:o_k].astype(F32) * (hk ** -0.5)
        k = p_ref[r, o_k:o_v].astype(F32)
        q_st.append(jnp.where(same_head, stack(q * jnp.exp(b[c])), 0.0).astype(BF16))
        k_tl.append(stack((k * jnp.exp(-b[c])).astype(BF16)))
        kd_bd.append(jnp.where(same_head, stack(k * jnp.exp(b_last[c] - b[c])), 0.0).astype(BF16))
        v_st.append(heads_to_rows(o_v, r))
    att = [jnp.where(att_keep, _dot_nt(q_st[c], k_tl[c]), 0.0).astype(BF16)
           for c in range(len(rows))]
    upd = [_dot_tn(v_st[c], kd_bd[c]) for c in range(len(rows))]
    o = [_dot(att[c], v_st[c]) for c in range(len(rows))]
    st = st_sc[...]
    for c in range(len(rows)):
        o[c] = o[c] + _dot_nt(q_st[c], st.astype(BF16))
        st = st * jnp.exp(b_last[c]) + upd[c]
    for c, r in enumerate(rows):
        oc = _rms_rows(o[c], onorm) * _silu(heads_to_rows(o_g, r).astype(F32))
        for h in range(GLA_HEADS):
            y_ref[r, cdim + h * hv:cdim + (h + 1) * hv] = oc[h * L:(h + 1) * L, :].astype(y_ref.dtype)
    st_sc[...] = st
    sto_ref[...] = st


def _mixer_ab(proj_a, proj_b, conv_w, w2pad, gb, onorm, conv0, st0, tb):
    bsz, t, _ = proj_a.shape
    cdim = conv_w.shape[1]
    dk = w2pad.shape[1]
    hv = onorm.shape[1]
    dv = hv * GLA_HEADS
    return pl.pallas_call(
        functools.partial(_mixer_body, tb=tb, cdim=cdim, dk=dk, dv=dv),
        grid=(bsz, t // tb),
        in_specs=[
            pl.BlockSpec((None, tb, proj_a.shape[2]), lambda b, i: (b, i, 0)),
            pl.BlockSpec((None, tb, proj_b.shape[2]), lambda b, i: (b, i, 0)),
            _resident(conv_w.shape),
            _resident(w2pad.shape),
            _resident(gb.shape),
            _resident(onorm.shape),
            pl.BlockSpec((None, CONV_WIDTH - 1, cdim), lambda b, i: (b, 0, 0)),
            pl.BlockSpec((None, hv, dk), lambda b, i: (b, 0, 0)),
        ],
        out_specs=[
            pl.BlockSpec((None, tb, cdim + dv), lambda b, i: (b, i, 0)),
            pl.BlockSpec((None, CONV_WIDTH - 1, cdim), lambda b, i: (b, 0, 0)),
            pl.BlockSpec((None, hv, dk), lambda b, i: (b, 0, 0)),
        ],
        out_shape=[
            jax.ShapeDtypeStruct((bsz, t, cdim + dv), BF16),
            jax.ShapeDtypeStruct((bsz, CONV_WIDTH - 1, cdim), F32),
            jax.ShapeDtypeStruct((bsz, hv, dk), F32),
        ],
        scratch_shapes=[
            pltpu.VMEM((tb + 8, cdim), F32),
            pltpu.VMEM((hv, dk), F32),
        ],
        compiler_params=_params("parallel", "arbitrary"),
    )(proj_a, proj_b, conv_w, w2pad, gb, onorm, conv0, st0)


def _proj_mixer_body(x_ref, g_ref, w_ref, cw_ref, w2_ref, gb_ref, on_ref, conv0_ref, st0_ref,
                     y_ref, convo_ref, sto_ref, pa_sc, pb_sc, ubuf, st_sc, **dims):
    _proj_body(x_ref, g_ref, w_ref, pa_sc, pb_sc)
    _mixer_body(pa_sc, pb_sc, cw_ref, w2_ref, gb_ref, on_ref, conv0_ref, st0_ref,
                y_ref, convo_ref, sto_ref, ubuf, st_sc, **dims)


def _proj_mixer_ab(x, g, w, layer, split, conv_w, w2pad, gb, onorm, conv0, st0, tb):
    bsz, t, d = x.shape
    cdim = conv_w.shape[1]
    dk = w2pad.shape[1]
    hv = onorm.shape[1]
    dv = hv * GLA_HEADS
    return pl.pallas_call(
        functools.partial(_proj_mixer_body, tb=tb, cdim=cdim, dk=dk, dv=dv),
        grid=(bsz, t // tb),
        in_specs=[
            pl.BlockSpec((None, tb, d), lambda b, i: (b, i, 0)),
            _resident((1, d)),
            _resident_layer(w, layer),
            _resident(conv_w.shape),
            _resident(w2pad.shape),
            _resident(gb.shape),
            _resident(onorm.shape),
            pl.BlockSpec((None, CONV_WIDTH - 1, cdim), lambda b, i: (b, 0, 0)),
            pl.BlockSpec((None, hv, dk), lambda b, i: (b, 0, 0)),
        ],
        out_specs=[
            pl.BlockSpec((None, tb, cdim + dv), lambda b, i: (b, i, 0)),
            pl.BlockSpec((None, CONV_WIDTH - 1, cdim), lambda b, i: (b, 0, 0)),
            pl.BlockSpec((None, hv, dk), lambda b, i: (b, 0, 0)),
        ],
        out_shape=[
            jax.ShapeDtypeStruct((bsz, t, cdim + dv), BF16),
            jax.ShapeDtypeStruct((bsz, CONV_WIDTH - 1, cdim), F32),
            jax.ShapeDtypeStruct((bsz, hv, dk), F32),
        ],
        scratch_shapes=[
            pltpu.VMEM((tb, split), F32),
            pltpu.VMEM((tb, w.shape[2] - split), BF16),
            pltpu.VMEM((tb + 8, cdim), F32),
            pltpu.VMEM((hv, dk), F32),
        ],
        compiler_params=_params("parallel", "arbitrary"),
    )(x, g, w, conv_w, w2pad, gb, onorm, conv0, st0)


def _attn_body(q_ref, ka_ref, kb_ref, va_ref, vb_ref, brow_ref, o_ref, tab_ref,
               *, tq, tqc, hist_is_padding):
    nwin = BAND_ROWS + tqc

    @pl.when((pl.program_id(0) == 0) & (pl.program_id(1) == 0))
    def _build_table():
        wrow = brow_ref.shape[1]
        sh = CHUNK.bit_length() - 1
        sub = lax.broadcasted_iota(jnp.int32, (SUBLANES, wrow), 0)
        q_chunk = lax.broadcasted_iota(jnp.int32, (tqc, nwin), 0) >> sh
        k_chunk = (lax.broadcasted_iota(jnp.int32, (tqc, nwin), 1) >> sh) - BAND_ROWS // CHUNK
        allowed = (k_chunk <= q_chunk) & (k_chunk >= q_chunk - BAND_ROWS // CHUNK)

        def one_head(h, carry):
            y = jnp.broadcast_to(brow_ref[pl.ds(h, 1), :], (SUBLANES, wrow))
            for bit in range(SUBLANES.bit_length() - 1):
                y = jnp.where(((sub >> bit) & 1) == 1, pltpu.roll(y, 1 << bit, axis=1), y)
            y = jnp.concatenate([y] + [pltpu.roll(y, SUBLANES * a, axis=1)
                                       for a in range(1, tqc // SUBLANES)], axis=0)
            slack = _table_slack(tqc)
            t = jnp.where(allowed, y[:, slack:slack + nwin], NEG_INF)
            tab_ref[h >> 1, pl.ds(pl.multiple_of((h & 1) * tqc, tqc), tqc), :] = t
            return carry

        lax.fori_loop(0, 2 * tab_ref.shape[0], one_head, 0)

    n_pairs = q_ref.shape[1] // LANES
    lo = lax.broadcasted_iota(jnp.int32, (tqc, LANES), 1) < ATT_HEAD_DIM
    keep_q = ((lax.broadcasted_iota(jnp.int32, (2 * tqc, LANES), 1) < ATT_HEAD_DIM)
              == (lax.broadcasted_iota(jnp.int32, (2 * tqc, LANES), 0) < tqc))

    def run(first_block):
        def window(j):
            r0 = j * tqc
            hist = slice(BAND_ROWS, BAND_ROWS) if first_block else slice(min(r0, BAND_ROWS), BAND_ROWS)
            cur = slice(max(r0 - BAND_ROWS, 0), r0 + tqc)
            c0 = max(BAND_ROWS - r0, 0) if first_block else 0
            return r0, hist, cur, c0

        def scores(j, p):
            r0, hist, cur, c0 = window(j)
            lanes = slice(p * LANES, (p + 1) * LANES)
            qp = q_ref[r0:r0 + tqc, lanes]
            q2 = jnp.where(keep_q, jnp.concatenate([qp, qp], axis=0), 0.0)
            s = _dot_nt(q2, kb_ref[cur, lanes])
            if hist.stop > hist.start:
                s = jnp.concatenate([_dot_nt(q2, ka_ref[hist, lanes].astype(BF16)), s], axis=1)
            return s + tab_ref[p, :, c0:]

        def weighted_values(j, p, e, den):
            r0, hist, cur, c0 = window(j)
            lanes = slice(p * LANES, (p + 1) * LANES)
            n_hist = hist.stop - hist.start
            pv = _dot(e[:, n_hist:], vb_ref[cur, lanes])
            if n_hist:
                pv = pv + _dot(e[:, :n_hist], va_ref[hist, lanes].astype(BF16))
            pv = pv / den
            o_ref[r0:r0 + tqc, lanes] = jnp.where(lo, pv[:tqc], pv[tqc:]).astype(o_ref.dtype)

        for j in range(tq // tqc):
            s_next = scores(j, 0)
            pending = None
            for p in range(n_pairs):
                s = s_next
                if p + 1 < n_pairs:
                    s_next = scores(j, p + 1)
                e = jnp.exp((s - jnp.max(s, axis=-1, keepdims=True)).astype(BF16))
                den = jnp.sum(e.astype(F32), axis=-1, keepdims=True)
                if pending is not None:
                    weighted_values(*pending)
                pending = (j, p, e, den)
            weighted_values(*pending)

    if hist_is_padding:
        first = pl.program_id(1) == 0
        pl.when(first)(lambda: run(True))
        pl.when(jnp.logical_not(first))(lambda: run(False))
    else:
        run(False)


def _band_attention(q, k_hist, k_cur, v_hist, v_cur, layer, brow, tq, tqc):
    bsz, t, width = q.shape
    assert t % tq == 0 and tq % tqc == 0
    hist_is_padding = k_hist is None
    if hist_is_padding:
        assert tq == BAND_ROWS
        k_hist, v_hist = k_cur, v_cur
        hist_spec = pl.BlockSpec((None, BAND_ROWS, width), lambda b, i: (b, jnp.maximum(i - 1, 0), 0))
    else:
        assert t == tq and k_hist.shape[2] == BAND_ROWS
        hist_spec = pl.BlockSpec((None, None, BAND_ROWS, width), lambda b, i: (layer, b, 0, 0))
    cur_map = lambda b, i: (b, i, 0)
    return pl.pallas_call(
        functools.partial(_attn_body, tq=tq, tqc=tqc, hist_is_padding=hist_is_padding),
        grid=(bsz, t // tq),
        in_specs=[
            pl.BlockSpec((None, tq, width), cur_map),
            hist_spec,
            pl.BlockSpec((None, tq, width), cur_map),
            hist_spec,
            pl.BlockSpec((None, tq, width), cur_map),
            _resident(brow.shape),
        ],
        out_specs=pl.BlockSpec((None, tq, width), cur_map),
        out_shape=jax.ShapeDtypeStruct((bsz, t, width), BF16),
        scratch_shapes=[
            pltpu.VMEM((width // LANES, 2 * tqc, BAND_ROWS + tqc), F32),
        ],
        compiler_params=_params("arbitrary", "arbitrary"),
    )(q, k_hist, k_cur, v_hist, v_cur, brow)


def _table_slack(tqc):
    return -(-tqc // LANES) * LANES


def _bias_row(rel_bias, tqc):
    slack = _table_slack(tqc)
    width = -(-(slack + BAND_ROWS + tqc) // LANES) * LANES
    rel = np.clip(BAND_ROWS + slack - np.arange(width), -(CHUNK - 1), MAX_REL) + (CHUNK - 1)
    return rel_bias[:, rel].astype(F32)


class _Group:
    def __init__(self, x, conv_prev, gla_prev, k_hist, v_hist):
        self.bsz, self.t, d = x.shape
        self.xf = x.reshape(-1, d)
        self.conv_prev, self.gla_prev = conv_prev, gla_prev
        self.k_hist, self.v_hist = k_hist, v_hist
        self.conv_new, self.gla_new, self.k_new, self.v_new = [], [], [], []

    @property
    def fresh(self):
        return self.k_hist is None


def _trunk(groups, w):
    d = groups[0].xf.shape[1]
    depth = w["norm_mix"].shape[0]
    heads = w["rel_bias"].shape[1]
    att_dim = heads * ATT_HEAD_DIM
    tm = BAND_ROWS
    tail_divs = []
    for grp in groups:
        if grp.fresh:
            assert grp.t % tm == 0
            tail_divs.append(grp.t // tm)
        else:
            assert grp.bsz * grp.t == tm and grp.k_hist.shape[2] == BAND_ROWS and grp.t <= BAND_ROWS
            tail_divs.append(1)
            grp.kh = grp.k_hist.reshape(grp.k_hist.shape[:3] + (att_dim,))
            grp.vh = grp.v_hist.reshape(grp.v_hist.shape[:3] + (att_dim,))
    for layer in range(depth):
        g_mix = w["norm_mix"][layer][None, :]
        g_ffn = w["norm_ffn"][layer][None, :]
        xs = [grp.xf for grp in groups]
        mixes = []
        if layer % 2 == 0:
            e = layer // 2
            split = 3 * w["conv_w"].shape[2]
            short = [grp for grp in groups if grp.t < BAND_ROWS]
            projs = dict(zip(map(id, short), _norm_proj([grp.xf for grp in short], g_mix,
                                                        w["w_in_ab"], e, 2 * tm, split))) if short else {}
            for grp in groups:
                st0 = grp.gla_prev[e].reshape(grp.bsz, -1, grp.gla_prev.shape[-1]).swapaxes(1, 2)
                rest = (w["conv_w"][e], w["gk_w2"][e], w["gk_b"][e][None, :],
                        w["gla_onorm"][e][None, :], grp.conv_prev[e], st0, min(2 * BAND_ROWS, grp.t))
                if id(grp) in projs:
                    proj_a, proj_b = projs[id(grp)]
                    mix, cs, st = _mixer_ab(proj_a.reshape(grp.bsz, grp.t, -1),
                                            proj_b.reshape(grp.bsz, grp.t, -1), *rest)
                else:
                    mix, cs, st = _proj_mixer_ab(grp.xf.reshape(grp.bsz, grp.t, -1), g_mix,
                                                 w["w_in_ab"], e, split, *rest)
                grp.conv_new.append(cs)
                grp.gla_new.append(st.swapaxes(1, 2).reshape(grp.gla_prev.shape[1:]))
                mixes.append(mix.reshape(-1, mix.shape[-1]))
            w_o, w_o_layer = w["w_out_ab"], e
        else:
            o = layer // 2
            qkvs = _norm_qkv(xs, g_mix, w["w_qkv"], o, w["pool"], w["poolt"],
                             w["q_norm"][o], w["k_norm"][o], tm, tail_divs)
            for grp, (q, k, v, kt, vt) in zip(groups, qkvs):
                shp = (grp.bsz, grp.t, att_dim)
                tqc = min(2 * CHUNK, grp.t)
                att = _band_attention(
                    q.reshape(shp), None if grp.fresh else grp.kh, k.reshape(shp),
                    None if grp.fresh else grp.vh, v.reshape(shp), o,
                    _bias_row(w["rel_bias"][o], tqc), min(BAND_ROWS, grp.t), tqc)
                grp.k_new.append(kt)
                grp.v_new.append(vt)
                mixes.append(att.reshape(-1, att_dim))
            w_o, w_o_layer = w["w_o_att"], o
        outs = _post_mixer(xs, mixes, w_o, w_o_layer, g_ffn, w["w_ffn_in"], w["w_ffn_out"], layer, tm, 2)
        for grp, xf in zip(groups, outs):
            grp.xf = xf

    def cache(grp, tails, hist):
        new = jnp.stack(tails).reshape(len(tails), grp.bsz, -1, heads, ATT_HEAD_DIM)
        if grp.fresh:
            return new
        return jnp.concatenate([hist[:, :, grp.t:], new], axis=2)

    return [(grp.xf.reshape(grp.bsz, grp.t, d), jnp.stack(grp.conv_new), jnp.stack(grp.gla_new),
             cache(grp, grp.k_new, grp.k_hist), cache(grp, grp.v_new, grp.v_hist)) for grp in groups]


def kernel(x_prompt, x_sample, state_conv, state_gla, cache_k, cache_v, norm_mix, norm_ffn,
           w_in_ab, conv_w, gla_gk_w2, gla_gk_b, gla_onorm, w_out_ab, w_qkv, q_norm, k_norm,
           rel_bias, w_o_att, w_ffn_in, w_ffn_out):
    bsz = x_prompt.shape[0]
    n_even = state_conv.shape[0]
    n_odd = cache_k.shape[0]
    heads, head_dim = cache_k.shape[-2:]
    att_dim = heads * head_dim
    in_ab = w_in_ab.shape[-1]
    in_pad = -in_ab % LANES
    rank = gla_gk_w2.shape[1]
    pool = (np.arange(att_dim)[:, None] // head_dim == np.arange(LANES)[None, :])
    w = {
        "norm_mix": norm_mix, "norm_ffn": norm_ffn,
        "w_in_ab": jnp.pad(w_in_ab, ((0, 0), (0, 0), (0, in_pad))).astype(BF16),
        "conv_w": conv_w,
        "gk_w2": jnp.pad(gla_gk_w2, ((0, 0), (0, LANES - rank), (0, 0))).astype(BF16),
        "gk_b": gla_gk_b, "gla_onorm": gla_onorm,
        "w_out_ab": w_out_ab.astype(BF16),
        "w_qkv": w_qkv.astype(BF16),
        "q_norm": jnp.tile(q_norm, (1, heads))[:, None, :],
        "k_norm": jnp.tile(k_norm, (1, heads))[:, None, :],
        "rel_bias": rel_bias,
        "w_o_att": w_o_att.astype(BF16),
        "w_ffn_in": w_ffn_in.astype(BF16), "w_ffn_out": w_ffn_out.astype(BF16),
        "pool": jnp.asarray(pool, BF16), "poolt": jnp.asarray(pool.T, BF16),
    }
    conv0 = jnp.zeros((n_even, bsz) + state_conv.shape[2:], x_prompt.dtype)
    gla0 = jnp.zeros((n_even, bsz) + state_gla.shape[2:], F32)
    (y_p, conv_p, gla_p, k_p, v_p), (y_s, conv_s, gla_s, k_s, v_s) = _trunk(
        [_Group(x_prompt, conv0, gla0, None, None),
         _Group(x_sample, state_conv, state_gla, cache_k, cache_v)], w)
    return (y_p, y_s, conv_p, gla_p, k_p, v_p, conv_s, gla_s, k_s, v_s)
```
